```python
import math
import jax, jax.numpy as jnp
from jax import lax
import numpy as np

D_MODEL = 1024
BATCH = 16
SEQ = 2048
DEPTH = 4

N_MLA_HEADS = 4
MLA_NOPE_DIM = 64
MLA_ROPE_DIM = 32
MLA_QK_DIM = MLA_NOPE_DIM + MLA_ROPE_DIM
MLA_V_DIM = 64
MLA_Q_RANK = 384
MLA_KV_RANK = 256
MLA_WIDTH = N_MLA_HEADS * MLA_V_DIM
N_FOX_HEADS = 4
FOX_HEAD_DIM = 64
FOX_WIDTH = N_FOX_HEADS * FOX_HEAD_DIM
FOX_FORGET_BIAS_INIT = 3.0
S5_GROUPS = 16
S5_GROUP_CH = 16
S5_STATE = 64
S5_WIDTH = S5_GROUPS * S5_GROUP_CH
N_BRANCHES = 3
BRANCH_WIDTH = 256
D_FF = 2816
CONV_WIDTH = 3
Q_BLOCK = 128
ROPE_THETA = 10000.0
NORM_EPS = 1e-6
NEG_INF = -1e30
IN_WIDTHS = (MLA_Q_RANK, MLA_KV_RANK, MLA_ROPE_DIM,
             FOX_WIDTH, FOX_WIDTH, FOX_WIDTH, N_FOX_HEADS,
             S5_WIDTH, N_BRANCHES * D_MODEL)
D_IN = sum(IN_WIDTHS)

kernel_name = "hybrid_mla_fox_s5_gated_trunk"


def rms_norm(x, gain):
    x32 = x.astype(jnp.float32)
    y = x32 * lax.rsqrt(jnp.mean(x32 * x32, axis=-1, keepdims=True) + NORM_EPS)
    return (y * gain.astype(jnp.float32)).astype(x.dtype)


def rope_tables(positions):
    inv_freq = ROPE_THETA ** (-jnp.arange(0, MLA_ROPE_DIM, 2, dtype=jnp.float32) / MLA_ROPE_DIM)
    ang = positions.astype(jnp.float32)[..., None] * inv_freq
    return jnp.cos(ang)[:, :, None, :], jnp.sin(ang)[:, :, None, :]


def apply_rope_tail(x, cos, sin):
    x_pass, x_rot = x[..., :MLA_NOPE_DIM], x[..., MLA_NOPE_DIM:]
    x1, x2 = jnp.split(x_rot, 2, axis=-1)
    rot = jnp.concatenate([x1 * cos - x2 * sin, x2 * cos + x1 * sin], axis=-1)
    return jnp.concatenate([x_pass, rot.astype(x.dtype)], axis=-1)


def causal_block_attention(q, k, v, log_forget_cum=None):
    seq = q.shape[1]
    scale = q.shape[-1] ** -0.5
    outs = []
    for i in range(seq // Q_BLOCK):
        q_lo, q_hi = i * Q_BLOCK, (i + 1) * Q_BLOCK
        qb, kb, vb = q[:, q_lo:q_hi], k[:, :q_hi], v[:, :q_hi]
        s = jnp.einsum('bqhd,bkhd->bhqk', qb, kb).astype(jnp.float32) * scale
        if log_forget_cum is not None:
            c_q = jnp.transpose(log_forget_cum[:, q_lo:q_hi], (0, 2, 1))[..., :, None]
            c_k = jnp.transpose(log_forget_cum[:, :q_hi], (0, 2, 1))[..., None, :]
            s = s + (c_q - c_k)
        mask = (q_lo + jnp.arange(Q_BLOCK))[:, None] >= jnp.arange(q_hi)[None, :]
        p = jax.nn.softmax(jnp.where(mask, s, NEG_INF), axis=-1).astype(v.dtype)
        outs.append(jnp.einsum('bhqk,bkhd->bqhd', p, vb))
    return jnp.concatenate(outs, axis=1)


def s5_branch(u, lam_re, lam_im, b_re, b_im, c_re, c_im, d, log_step, w_glu, b_glu):
    bsz, seq, _ = u.shape
    u32 = u.astype(jnp.float32).reshape(bsz, seq, S5_GROUPS, S5_GROUP_CH)
    lam = lax.complex(lam_re.astype(jnp.float32), lam_im.astype(jnp.float32))
    step = jnp.exp(log_step.astype(jnp.float32))[:, None]
    lam_bar = jnp.exp(lam * step)
    b_mat = lax.complex(b_re.astype(jnp.float32), b_im.astype(jnp.float32))
    b_bar = ((lam_bar - 1.0) / lam)[..., None] * b_mat
    bu = jnp.einsum('gph,bsgh->bsgp', b_bar, u32.astype(jnp.complex64))
    a = jnp.broadcast_to(lam_bar, bu.shape)

    def combine(e1, e2):
        a1, x1 = e1
        a2, x2 = e2
        return a1 * a2, a2 * x1 + x2

    _, state = lax.associative_scan(combine, (a, bu), axis=1)
    c_mat = lax.complex(c_re.astype(jnp.float32), c_im.astype(jnp.float32))
    y = jnp.real(jnp.einsum('ghp,bsgp->bsgh', c_mat, state)) + d.astype(jnp.float32) * u32
    y = jax.nn.gelu(y.reshape(bsz, seq, S5_WIDTH))
    y = y * jax.nn.sigmoid(y @ w_glu.astype(jnp.float32) + b_glu.astype(jnp.float32))
    return y.astype(u.dtype)


def causal_depthwise_conv(h, w):
    seq = h.shape[1]
    hp = jnp.pad(h, ((0, 0), (CONV_WIDTH - 1, 0), (0, 0)))
    return sum(w[j] * hp[:, j:j + seq] for j in range(CONV_WIDTH))


def mixer_block(h, cos, sin, w_in, q_lat_norm_g, w_uq, kv_lat_norm_g, w_ukv,
                mla_q_norm_g, mla_k_norm_g, fox_q_norm_g, fox_k_norm_g, fox_f_bias,
                s5_lambda_re, s5_lambda_im, s5_b_re, s5_b_im, s5_c_re, s5_c_im, s5_d,
                s5_log_step, s5_w_glu, s5_b_glu, w_branch, w_out):
    bsz, seq, _ = h.shape
    proj = h @ w_in
    split_points = np.cumsum(IN_WIDTHS)[:-1].tolist()
    c_q, c_kv, k_r, fq, fk, fv, f_logit, u, gate_logits = jnp.split(proj, split_points, axis=-1)

    q = jnp.einsum('bsr,rhd->bshd', rms_norm(c_q, q_lat_norm_g), w_uq)
    kv = jnp.einsum('bsr,rhd->bshd', rms_norm(c_kv, kv_lat_norm_g), w_ukv)
    k_nope, v_mla = kv[..., :MLA_NOPE_DIM], kv[..., MLA_NOPE_DIM:]
    k_rope = jnp.broadcast_to(k_r[:, :, None, :], (bsz, seq, N_MLA_HEADS, MLA_ROPE_DIM))
    k_mla = jnp.concatenate([k_nope, k_rope], axis=-1)
    q_mla = apply_rope_tail(rms_norm(q, mla_q_norm_g), cos, sin)
    k_mla = apply_rope_tail(rms_norm(k_mla, mla_k_norm_g), cos, sin)
    o_mla = causal_block_attention(q_mla, k_mla, v_mla).reshape(bsz, seq, MLA_WIDTH)

    q_fox = rms_norm(fq.reshape(bsz, seq, N_FOX_HEADS, FOX_HEAD_DIM), fox_q_norm_g)
    k_fox = rms_norm(fk.reshape(bsz, seq, N_FOX_HEADS, FOX_HEAD_DIM), fox_k_norm_g)
    v_fox = fv.reshape(bsz, seq, N_FOX_HEADS, FOX_HEAD_DIM)
    log_f = jax.nn.log_sigmoid((f_logit + fox_f_bias).astype(jnp.float32))
    cum_log_f = lax.cumsum(log_f, axis=1)
    o_fox = causal_block_attention(q_fox, k_fox, v_fox, cum_log_f).reshape(bsz, seq, FOX_WIDTH)

    o_s5 = s5_branch(u, s5_lambda_re, s5_lambda_im, s5_b_re, s5_b_im, s5_c_re, s5_c_im,
                     s5_d, s5_log_step, s5_w_glu, s5_b_glu)

    branches = jnp.stack([o_mla, o_fox, o_s5], axis=2)
    gates = jax.nn.sigmoid(gate_logits.reshape(bsz, seq, N_BRANCHES, D_MODEL))
    projected = jnp.einsum('bsnw,nwd->bsnd', branches, w_branch)
    merged = jnp.einsum('bsnd,bsnd->bsd', gates, projected)
    return merged @ w_out


def conv_gated_ffn(h, w_up, conv_w, w_down):
    up = causal_depthwise_conv(h @ w_up, conv_w)
    gate, val = jnp.split(up, 2, axis=-1)
    return (jax.nn.silu(gate) * val) @ w_down


def setup_inputs(seed: int = 0) -> dict:
    key = jax.random.key(seed)
    ks = iter(jax.random.split(key, 40))
    f32 = jnp.float32
    L = DEPTH

    def nrm(shape, scale):
        return scale * jax.random.normal(next(ks), shape, f32)

    def gain(shape):
        return 1.0 + nrm(shape, 0.02)

    x = nrm((BATCH, SEQ, D_MODEL), 1.0)
    offset = jax.random.randint(next(ks), (BATCH, 1), 0, 4096, dtype=jnp.int32)
    positions = offset + jnp.arange(SEQ, dtype=jnp.int32)[None, :]

    attn_norm_g = gain((L, D_MODEL))
    w_in = nrm((L, D_MODEL, D_IN), D_MODEL ** -0.5)
    q_lat_norm_g = gain((L, MLA_Q_RANK))
    w_uq = nrm((L, MLA_Q_RANK, N_MLA_HEADS, MLA_QK_DIM), MLA_Q_RANK ** -0.5)
    kv_lat_norm_g = gain((L, MLA_KV_RANK))
    w_ukv = nrm((L, MLA_KV_RANK, N_MLA_HEADS, MLA_NOPE_DIM + MLA_V_DIM), MLA_KV_RANK ** -0.5)
    mla_q_norm_g = gain((L, MLA_QK_DIM))
    mla_k_norm_g = gain((L, MLA_QK_DIM))
    fox_q_norm_g = gain((L, FOX_HEAD_DIM))
    fox_k_norm_g = gain((L, FOX_HEAD_DIM))
    fox_f_bias = FOX_FORGET_BIAS_INIT + nrm((L, N_FOX_HEADS), 0.5)
    s5_lambda_re = -0.5 + nrm((L, S5_GROUPS, S5_STATE), 0.01)
    s5_lambda_im = math.pi * jnp.arange(S5_STATE, dtype=f32) + nrm((L, S5_GROUPS, S5_STATE), 0.01)
    s5_b_re = nrm((L, S5_GROUPS, S5_STATE, S5_GROUP_CH), (2 * S5_GROUP_CH) ** -0.5)
    s5_b_im = nrm((L, S5_GROUPS, S5_STATE, S5_GROUP_CH), (2 * S5_GROUP_CH) ** -0.5)
    s5_c_re = nrm((L, S5_GROUPS, S5_GROUP_CH, S5_STATE), (2 * S5_STATE) ** -0.5)
    s5_c_im = nrm((L, S5_GROUPS, S5_GROUP_CH, S5_STATE), (2 * S5_STATE) ** -0.5)
    s5_d = nrm((L, S5_GROUPS, S5_GROUP_CH), 0.5)
    s5_log_step = jnp.log(jax.random.uniform(next(ks), (L, S5_GROUPS), f32, minval=0.001, maxval=0.1))
    s5_w_glu = nrm((L, S5_WIDTH, S5_WIDTH), S5_WIDTH ** -0.5)
    s5_b_glu = nrm((L, S5_WIDTH), 0.02)
    w_branch = nrm((L, N_BRANCHES, BRANCH_WIDTH, D_MODEL), BRANCH_WIDTH ** -0.5)
    w_out = nrm((L, D_MODEL, D_MODEL), D_MODEL ** -0.5)
    ffn_norm_g = gain((L, D_MODEL))
    w_up = nrm((L, D_MODEL, 2 * D_FF), D_MODEL ** -0.5)
    ffn_conv_w = nrm((L, CONV_WIDTH, 2 * D_FF), 0.2).at[:, CONV_WIDTH - 1].add(1.0)
    w_down = nrm((L, D_FF, D_MODEL), D_FF ** -0.5)
    return {"x": x, "positions": positions, "attn_norm_g": attn_norm_g, "w_in": w_in,
            "q_lat_norm_g": q_lat_norm_g, "w_uq": w_uq, "kv_lat_norm_g": kv_lat_norm_g,
            "w_ukv": w_ukv, "mla_q_norm_g": mla_q_norm_g, "mla_k_norm_g": mla_k_norm_g,
            "fox_q_norm_g": fox_q_norm_g, "fox_k_norm_g": fox_k_norm_g, "fox_f_bias": fox_f_bias,
            "s5_lambda_re": s5_lambda_re, "s5_lambda_im": s5_lambda_im, "s5_b_re": s5_b_re,
            "s5_b_im": s5_b_im, "s5_c_re": s5_c_re, "s5_c_im": s5_c_im, "s5_d": s5_d,
            "s5_log_step": s5_log_step, "s5_w_glu": s5_w_glu, "s5_b_glu": s5_b_glu,
            "w_branch": w_branch, "w_out": w_out, "ffn_norm_g": ffn_norm_g, "w_up": w_up,
            "ffn_conv_w": ffn_conv_w, "w_down": w_down}


def reference(x, positions, attn_norm_g, w_in, q_lat_norm_g, w_uq, kv_lat_norm_g, w_ukv,
              mla_q_norm_g, mla_k_norm_g, fox_q_norm_g, fox_k_norm_g, fox_f_bias,
              s5_lambda_re, s5_lambda_im, s5_b_re, s5_b_im, s5_c_re, s5_c_im, s5_d,
              s5_log_step, s5_w_glu, s5_b_glu, w_branch, w_out, ffn_norm_g, w_up,
              ffn_conv_w, w_down):
    cos, sin = rope_tables(positions)
    for l in range(DEPTH):
        h = rms_norm(x, attn_norm_g[l])
        x = x + mixer_block(h, cos, sin, w_in[l], q_lat_norm_g[l], w_uq[l], kv_lat_norm_g[l],
                            w_ukv[l], mla_q_norm_g[l], mla_k_norm_g[l], fox_q_norm_g[l],
                            fox_k_norm_g[l], fox_f_bias[l], s5_lambda_re[l], s5_lambda_im[l],
                            s5_b_re[l], s5_b_im[l], s5_c_re[l], s5_c_im[l], s5_d[l],
                            s5_log_step[l], s5_w_glu[l], s5_b_glu[l], w_branch[l], w_out[l])
        h = rms_norm(x, ffn_norm_g[l])
        x = x + conv_gated_ffn(h, w_up[l], ffn_conv_w[l], w_down[l])
    return x
```

```python
import functools
import math

import numpy as np
import jax
import jax.numpy as jnp
from jax import lax
from jax.experimental import pallas as pl
from jax.experimental.pallas import tpu as pltpu

F32 = jnp.float32
BF16 = jnp.bfloat16

D_MODEL = 1024
N_HEADS = 4
MLA_NOPE = 64
MLA_ROPE = 32
MLA_QK = MLA_NOPE + MLA_ROPE
MLA_V = 64
MLA_Q_RANK = 384
MLA_KV_RANK = 256
FOX_HD = 64
S5_G = 16
S5_H = 16
S5_P = 64
S5_W = S5_G * S5_H
S5_N = S5_G * S5_P
BRANCH_W = 256
N_BRANCH = 3
D_FF = 2816
CONV_W = 3
ROPE_THETA = 10000.0
EPS = 1e-6
NEG = -1e30
LOG2E = math.log2(math.e)

LANE = 128
HEAD_PAD = 128
QK_W = N_HEADS * HEAD_PAD
C_CQ, C_CKV, C_FQ, C_FK, C_FV, C_U, C_MISC = 0, 384, 640, 896, 1152, 1408, 1664
WA_COLS = 1792
KR_LANE = MLA_NOPE
AUG = FOX_HD

VMEM_LIMIT = 56 * 1024 * 1024

TM_PROJ = 512
TQ = 512
TT_S5 = 64
TM_MERGE = 512
TM_FFN = 512
CONV_HALO = 8


def _const_spec(shape):
    nd = len(shape)
    return pl.BlockSpec(shape, lambda *_: (0,) * nd, pipeline_mode=pl.Buffered(1))


def _params(sem):
    return pltpu.CompilerParams(dimension_semantics=sem, vmem_limit_bytes=VMEM_LIMIT)


def _rms(x, gain):
    ms = jnp.mean(x * x, axis=-1, keepdims=True)
    return x * lax.rsqrt(ms + EPS) * gain


def _split2(x):
    hi = x.astype(BF16)
    lo = (x - hi.astype(F32)).astype(BF16)
    return hi, lo


def _split3(x):
    hi = x.astype(BF16)
    r = x - hi.astype(F32)
    mid = r.astype(BF16)
    lo = (r - mid.astype(F32)).astype(BF16)
    return hi, mid, lo


def _head_norm(x, ones_ref, inv_width):
    sq = x * x
    hi, lo = _split2(sq)
    ones = ones_ref[...]
    ssum = (jnp.dot(hi, ones, preferred_element_type=F32)
            + jnp.dot(lo, ones, preferred_element_type=F32))
    return x * lax.rsqrt(ssum * inv_width + EPS)


def _rope_tiles(x, cos, sin):
    lane = lax.broadcasted_iota(jnp.int32, cos.shape, 1)
    first_half = lane < (MLA_NOPE + MLA_ROPE // 2)
    outs = []
    for h in range(N_HEADS):
        blk = x[:, h * HEAD_PAD:(h + 1) * HEAD_PAD]
        up = pltpu.roll(blk, HEAD_PAD - MLA_ROPE // 2, 1)
        dn = pltpu.roll(blk, MLA_ROPE // 2, 1)
        outs.append(blk * cos + jnp.where(first_half, up, dn) * sin)
    return jnp.concatenate(outs, axis=1)


def _rope_table_kernel(pos_ref, inv_ref, sign_ref, cos_ref, sin_ref):
    ang = pos_ref[...] * inv_ref[...]
    cos_ref[...] = jnp.cos(ang)
    sin_ref[...] = jnp.sin(ang) * sign_ref[...]


def _rope_tables(positions):
    m = positions.size
    tm = 1024
    pos = positions.astype(F32).reshape(m, 1)
    inv_freq = ROPE_THETA ** (-jnp.arange(0, MLA_ROPE, 2, dtype=F32) / MLA_ROPE)
    half = MLA_ROPE // 2
    inv_lane = jnp.zeros((LANE,), F32)
    inv_lane = inv_lane.at[MLA_NOPE:MLA_NOPE + half].set(inv_freq)
    inv_lane = inv_lane.at[MLA_NOPE + half:MLA_QK].set(inv_freq)
    sign = np.zeros((LANE,), np.float32)
    sign[MLA_NOPE:MLA_NOPE + half] = -1.0
    sign[MLA_NOPE + half:MLA_QK] = 1.0
    return pl.pallas_call(
        _rope_table_kernel,
        grid=(m // tm,),
        in_specs=[pl.BlockSpec((tm, 1), lambda i: (i, 0)),
                  pl.BlockSpec((1, LANE), lambda i: (0, 0)),
                  pl.BlockSpec((1, LANE), lambda i: (0, 0))],
        out_specs=[pl.BlockSpec((tm, LANE), lambda i: (i, 0))] * 2,
        out_shape=[jax.ShapeDtypeStruct((m, LANE), F32)] * 2,
        compiler_params=_params(("arbitrary",)),
        name="rope_tables",
    )(pos, inv_lane.reshape(1, LANE), jnp.asarray(sign).reshape(1, LANE))


def _proj_kernel(x_ref, cos_ref, sin_ref, g_ref, wa_ref, fbias_ref,
                 qlat_g_ref, wuq_ref, kvlat_g_ref, wukv_ref, gq_ref, gk_ref, ones_qk_ref,
                 gfq_ref, gfk_ref, ones_fox_ref, tri_ref,
                 qm_ref, km_ref, vm_ref, qf_ref, kf_ref, vf_ref, u_ref,
                 carry_ref):
    t = pl.program_id(1)
    tm = x_ref.shape[0]

    h = _rms(x_ref[...], g_ref[...]).astype(BF16)
    p = jnp.dot(h, wa_ref[...], preferred_element_type=F32)
    misc = p[:, C_MISC:C_MISC + LANE]
    lane = lax.broadcasted_iota(jnp.int32, (tm, LANE), 1)
    cos = cos_ref[...]
    sin = sin_ref[...]

    cq = _rms(p[:, C_CQ:C_CQ + MLA_Q_RANK], qlat_g_ref[...]).astype(BF16)
    q = jnp.dot(cq, wuq_ref[...], preferred_element_type=F32)
    q = _head_norm(q, ones_qk_ref, 1.0 / MLA_QK) * gq_ref[...]
    qm_ref[...] = _rope_tiles(q, cos, sin).astype(BF16)

    ckv = _rms(p[:, C_CKV:C_CKV + MLA_KV_RANK], kvlat_g_ref[...]).astype(BF16)
    kv = jnp.dot(ckv, wukv_ref[...], preferred_element_type=F32)
    k_rope = jnp.where((lane >= KR_LANE) & (lane < MLA_QK), misc, 0.0)
    k = kv[:, :QK_W] + jnp.concatenate([k_rope] * N_HEADS, axis=1)
    k = _head_norm(k, ones_qk_ref, 1.0 / MLA_QK) * gk_ref[...]
    km_ref[...] = _rope_tiles(k, cos, sin).astype(BF16)
    vm_ref[...] = kv[:, QK_W:].astype(BF16)

    @pl.when(t == 0)
    def _():
        carry_ref[...] = jnp.zeros_like(carry_ref)

    z = misc + fbias_ref[...]
    log_f = jnp.minimum(z, 0.0) - jnp.log1p(jnp.exp(-jnp.abs(z)))
    tri = tri_ref[...]
    c = carry_ref[0:1, :]
    for part in _split3(log_f):
        c = c + jnp.dot(tri, part, preferred_element_type=F32)
    carry_ref[0:1, :] = c[tm - 1:tm, :]
    c = c * LOG2E

    fq = _head_norm(p[:, C_FQ:C_FQ + BRANCH_W], ones_fox_ref, 1.0 / FOX_HD) * gfq_ref[...]
    fk = _head_norm(p[:, C_FK:C_FK + BRANCH_W], ones_fox_ref, 1.0 / FOX_HD) * gfk_ref[...]
    q_tiles, k_tiles = [], []
    for hd in range(N_HEADS):
        pair = slice((hd // 2) * LANE, (hd // 2 + 1) * LANE)
        qb, kb = fq[:, pair], fk[:, pair]
        if hd % 2:
            qb = pltpu.roll(qb, FOX_HD, 1)
            kb = pltpu.roll(kb, FOX_HD, 1)
        ch = c[:, hd:hd + 1]
        hi = ch.astype(BF16).astype(F32)
        r = ch - hi
        mid = r.astype(BF16).astype(F32)
        lo = r - mid
        qx = jnp.where(lane < AUG, qb,
             jnp.where(lane == AUG, hi,
             jnp.where(lane == AUG + 1, mid,
             jnp.where(lane == AUG + 2, lo,
             jnp.where(lane < AUG + 6, 1.0, 0.0)))))
        kx = jnp.where(lane < AUG, kb,
             jnp.where(lane < AUG + 3, 1.0,
             jnp.where(lane == AUG + 3, -hi,
             jnp.where(lane == AUG + 4, -mid,
             jnp.where(lane == AUG + 5, -lo, 0.0)))))
        q_tiles.append(qx)
        k_tiles.append(kx)
    qf_ref[...] = jnp.concatenate(q_tiles, axis=1).astype(BF16)
    kf_ref[...] = jnp.concatenate(k_tiles, axis=1).astype(BF16)
    vf_ref[...] = p[:, C_FV:C_FV + BRANCH_W].astype(BF16)

    u_ref[...] = p[:, C_U:C_U + S5_W]


def _proj(x, cos, sin, w, batch, seq):
    m = x.shape[0]
    tm = TM_PROJ
    nt = seq // tm
    row = lambda b, t: (b * nt + t, 0)
    consts = [w["attn_g"], w["wa"], w["fbias"], w["qlat_g"], w["wuq"], w["kvlat_g"], w["wukv"],
              w["gq"], w["gk"], w["ones_qk"], w["gfq"], w["gfk"], w["ones_fox"], w["tri"]]
    out_shape = [jax.ShapeDtypeStruct((m, QK_W), BF16), jax.ShapeDtypeStruct((m, QK_W), BF16),
                 jax.ShapeDtypeStruct((m, BRANCH_W), BF16),
                 jax.ShapeDtypeStruct((m, QK_W), BF16), jax.ShapeDtypeStruct((m, QK_W), BF16),
                 jax.ShapeDtypeStruct((m, BRANCH_W), BF16),
                 jax.ShapeDtypeStruct((seq, batch * S5_W), F32)]
    out_specs = [pl.BlockSpec((tm, QK_W), row), pl.BlockSpec((tm, QK_W), row),
                 pl.BlockSpec((tm, BRANCH_W), row),
                 pl.BlockSpec((tm, QK_W), row), pl.BlockSpec((tm, QK_W), row),
                 pl.BlockSpec((tm, BRANCH_W), row),
                 pl.BlockSpec((tm, S5_W), lambda b, t: (t, b))]
    return pl.pallas_call(
        _proj_kernel,
        grid=(batch, nt),
        in_specs=[pl.BlockSpec((tm, D_MODEL), row), pl.BlockSpec((tm, LANE), row),
                  pl.BlockSpec((tm, LANE), row)] + [_const_spec(c.shape) for c in consts],
        out_specs=out_specs,
        out_shape=out_shape,
        scratch_shapes=[pltpu.VMEM((8, LANE), F32)],
        compiler_params=_params(("arbitrary", "arbitrary")),
        name="proj",
    )(x, cos, sin, *consts)


def _attn_kernel(q_ref, k_ref, v_ref, o_ref, acc_ref, m_ref, l_ref):
    i = pl.program_id(1)
    tq = q_ref.shape[0]
    tk = tq
    lane = lax.broadcasted_iota(jnp.int32, (tq, LANE), 1)
    low_half = lane < MLA_V
    vlane_low = lax.broadcasted_iota(jnp.int32, (tk, LANE), 1) < MLA_V
    row = lax.broadcasted_iota(jnp.int32, (tq, tk), 0)
    col = lax.broadcasted_iota(jnp.int32, (tq, tk), 1)

    acc_ref[...] = jnp.zeros_like(acc_ref)
    m_ref[...] = jnp.full_like(m_ref, NEG)
    l_ref[...] = jnp.zeros_like(l_ref)

    def step(j, masked):
        koff = pl.multiple_of(j * tk, tk)
        for h in range(N_HEADS):
            hs = slice(h * HEAD_PAD, (h + 1) * HEAD_PAD)
            pair = h // 2
            mine = low_half if h % 2 == 0 else jnp.logical_not(low_half)
            vmine = vlane_low if h % 2 == 0 else jnp.logical_not(vlane_low)
            s = lax.dot_general(q_ref[:, hs], k_ref[pl.ds(koff, tk), hs],
                                (((1,), (1,)), ((), ())), preferred_element_type=F32)
            if masked:
                s = jnp.where(row >= col, s, NEG)
            m_prev = m_ref[h]
            m_new = jnp.maximum(m_prev, jnp.max(s, axis=-1, keepdims=True))
            alpha = jnp.exp2(m_prev - m_new)
            p = jnp.exp2(s - m_new)
            l_ref[h] = alpha * l_ref[h] + jnp.sum(p, axis=-1, keepdims=True)
            m_ref[h] = m_new
            v_pair = v_ref[pl.ds(koff, tk), pair * LANE:(pair + 1) * LANE]
            v_mine = jnp.where(vmine, v_pair, jnp.zeros_like(v_pair))
            pv = jnp.dot(p.astype(BF16), v_mine, preferred_element_type=F32)
            acc_ref[pair] = acc_ref[pair] * jnp.where(mine, alpha, 1.0) + pv

    def body(j, carry):
        step(j, False)
        return carry

    lax.fori_loop(0, i, body, 0)
    step(i, True)

    outs = []
    for pair in range(N_HEADS // 2):
        inv = jnp.where(low_half, 1.0 / l_ref[2 * pair], 1.0 / l_ref[2 * pair + 1])
        outs.append(acc_ref[pair] * inv)
    o_ref[...] = jnp.concatenate(outs, axis=1).astype(o_ref.dtype)


def _attention(q, k, v, batch, seq):
    m = q.shape[0]
    nq = seq // TQ
    return pl.pallas_call(
        _attn_kernel,
        grid=(batch, nq),
        in_specs=[pl.BlockSpec((TQ, QK_W), lambda b, i: (b * nq + i, 0)),
                  pl.BlockSpec((seq, QK_W), lambda b, i: (b, 0)),
                  pl.BlockSpec((seq, BRANCH_W), lambda b, i: (b, 0))],
        out_specs=pl.BlockSpec((TQ, BRANCH_W), lambda b, i: (b * nq + i, 0)),
        out_shape=jax.ShapeDtypeStruct((m, BRANCH_W), BF16),
        scratch_shapes=[pltpu.VMEM((N_HEADS // 2, TQ, LANE), F32),
                        pltpu.VMEM((N_HEADS, TQ, 1), F32),
                        pltpu.VMEM((N_HEADS, TQ, 1), F32)],
        compiler_params=_params(("arbitrary", "arbitrary")),
        name="attn",
    )(q, k, v)


S5_LANE_CHUNK = 512


def _s5_kernel(u_ref, bmat_ref, lam_ref, cmat_ref, d_ref, wglu_ref, bglu_ref, o_ref,
               s_ref, st_ref, *, batch):
    step_idx = pl.program_id(0)
    rows = u_ref.shape[0]
    tt = rows // batch

    @pl.when(step_idx == 0)
    def _():
        st_ref[...] = jnp.zeros_like(st_ref)

    u = u_ref[...]
    s_ref[...] = jnp.dot(u.astype(BF16), bmat_ref[...], preferred_element_type=F32)

    for c0 in range(0, S5_N, S5_LANE_CHUNK):
        re_cols = slice(c0, c0 + S5_LANE_CHUNK)
        im_cols = slice(S5_N + c0, S5_N + c0 + S5_LANE_CHUNK)
        lam_re = lam_ref[0:1, re_cols]
        lam_im = lam_ref[1:2, re_cols]

        def body(t, carry):
            x_re, x_im = carry
            r0 = pl.multiple_of(t * batch, batch)
            n_re = lam_re * x_re - lam_im * x_im + s_ref[pl.ds(r0, batch), re_cols]
            n_im = lam_re * x_im + lam_im * x_re + s_ref[pl.ds(r0, batch), im_cols]
            s_ref[pl.ds(r0, batch), re_cols] = n_re
            s_ref[pl.ds(r0, batch), im_cols] = n_im
            return n_re, n_im

        x_re, x_im = lax.fori_loop(0, tt, body, (st_ref[:, re_cols], st_ref[:, im_cols]),
                                   unroll=2)
        st_ref[:, re_cols] = x_re
        st_ref[:, im_cols] = x_im

    y = jnp.dot(s_ref[...].astype(BF16), cmat_ref[...], preferred_element_type=F32)
    y = y + d_ref[...] * u
    y = 0.5 * y * (1.0 + jnp.tanh(math.sqrt(2.0 / math.pi) * (y + 0.044715 * (y * y * y))))
    z = jnp.dot(y.astype(BF16), wglu_ref[...], preferred_element_type=F32) + bglu_ref[...]
    o_ref[...] = (y * jax.nn.sigmoid(z)).astype(o_ref.dtype)


def _s5(u_tm, w, batch, seq):
    rows = TT_S5 * batch
    consts = [w["bmat"], w["lam"], w["cmat"], w["s5_d"], w["wglu"], w["bglu"]]
    return pl.pallas_call(
        functools.partial(_s5_kernel, batch=batch),
        grid=(seq // TT_S5,),
        in_specs=[pl.BlockSpec((rows, S5_W), lambda t: (t, 0))]
                 + [_const_spec(c.shape) for c in consts],
        out_specs=pl.BlockSpec((rows, S5_W), lambda t: (t, 0)),
        out_shape=jax.ShapeDtypeStruct((seq * batch, S5_W), BF16),
        scratch_shapes=[pltpu.VMEM((rows, 2 * S5_N), F32), pltpu.VMEM((batch, 2 * S5_N), F32)],
        compiler_params=_params(("arbitrary",)),
        name="s5",
    )(u_tm, *consts)


def _merge_kernel(x_ref, om_ref, of_ref, os_ref, g_ref, wg_ref, wbr_ref, wout_ref, o_ref):
    x = x_ref[...]
    h = _rms(x, g_ref[...]).astype(BF16)
    merged = None
    for n, br_ref in enumerate((om_ref, of_ref, os_ref)):
        logits = jnp.dot(h, wg_ref[:, n * D_MODEL:(n + 1) * D_MODEL], preferred_element_type=F32)
        proj = jnp.dot(br_ref[...], wbr_ref[n], preferred_element_type=F32)
        term = jax.nn.sigmoid(logits) * proj
        merged = term if merged is None else merged + term
    o_ref[...] = x + jnp.dot(merged.astype(BF16), wout_ref[...], preferred_element_type=F32)


def _merge(x, o_mla, o_fox, o_s5_tm, w, batch, seq):
    m = x.shape[0]
    tm = TM_MERGE
    nt = seq // tm
    row = lambda b, t: (b * nt + t, 0)
    consts = [w["attn_g"], w["wg"], w["wbr"], w["wout"]]
    return pl.pallas_call(
        _merge_kernel,
        grid=(batch, nt),
        in_specs=[pl.BlockSpec((tm, D_MODEL), row), pl.BlockSpec((tm, BRANCH_W), row),
                  pl.BlockSpec((tm, BRANCH_W), row),
                  pl.BlockSpec((tm, BRANCH_W), lambda b, t: (t, b))]
                 + [_const_spec(c.shape) for c in consts],
        out_specs=pl.BlockSpec((tm, D_MODEL), row),
        out_shape=jax.ShapeDtypeStruct((m, D_MODEL), F32),
        compiler_params=_params(("arbitrary", "arbitrary")),
        name="merge",
    )(x, o_mla, o_fox, o_s5_tm, *consts)


FFN_CHUNK = 256


def _ffn_kernel(x_ref, g_ref, wup_ref, conv_ref, wdown_ref, o_ref, up_ref, act_ref):
    t = pl.program_id(1)
    tm = x_ref.shape[0]

    @pl.when(t == 0)
    def _():
        up_ref[0:CONV_HALO, :] = jnp.zeros((CONV_HALO, 2 * D_FF), F32)

    x = x_ref[...]
    h = _rms(x, g_ref[...]).astype(BF16)
    up_ref[CONV_HALO:CONV_HALO + tm, :] = jnp.dot(h, wup_ref[...], preferred_element_type=F32)

    def conv(cols):
        out = None
        for j in range(CONV_W):
            lo = CONV_HALO - (CONV_W - 1) + j
            term = conv_ref[j:j + 1, cols] * up_ref[lo:lo + tm, cols]
            out = term if out is None else out + term
        return out

    for c0 in range(0, D_FF, FFN_CHUNK):
        gate = conv(slice(c0, c0 + FFN_CHUNK))
        val = conv(slice(D_FF + c0, D_FF + c0 + FFN_CHUNK))
        act_ref[:, c0:c0 + FFN_CHUNK] = (gate * jax.nn.sigmoid(gate) * val).astype(BF16)

    up_ref[0:CONV_HALO, :] = up_ref[tm:tm + CONV_HALO, :]
    o_ref[...] = x + jnp.dot(act_ref[...], wdown_ref[...], preferred_element_type=F32)


def _ffn(x, w, batch, seq):
    m = x.shape[0]
    tm = TM_FFN
    nt = seq // tm
    row = lambda b, t: (b * nt + t, 0)
    consts = [w["ffn_g"], w["wup"], w["conv"], w["wdown"]]
    return pl.pallas_call(
        _ffn_kernel,
        grid=(batch, nt),
        in_specs=[pl.BlockSpec((tm, D_MODEL), row)] + [_const_spec(c.shape) for c in consts],
        out_specs=pl.BlockSpec((tm, D_MODEL), row),
        out_shape=jax.ShapeDtypeStruct((m, D_MODEL), F32),
        scratch_shapes=[pltpu.VMEM((tm + CONV_HALO, 2 * D_FF), F32),
                        pltpu.VMEM((tm, D_FF), BF16)],
        compiler_params=_params(("arbitrary", "arbitrary")),
        name="ffn",
    )(x, *consts)


def _block_ones(n, width):
    idx = np.arange(n) // width
    return jnp.asarray((idx[:, None] == idx[None, :]).astype(np.float32), dtype=BF16)


def _pack_weights(attn_norm_g, w_in, q_lat_norm_g, w_uq, kv_lat_norm_g, w_ukv,
                  mla_q_norm_g, mla_k_norm_g, fox_q_norm_g, fox_k_norm_g, fox_f_bias,
                  s5_lambda_re, s5_lambda_im, s5_b_re, s5_b_im, s5_c_re, s5_c_im, s5_d,
                  s5_log_step, s5_w_glu, s5_b_glu, w_branch, w_out, ffn_norm_g, w_up,
                  ffn_conv_w, w_down):
    depth = w_in.shape[0]
    f32 = lambda a: a.astype(F32)
    o = np.cumsum((0, MLA_Q_RANK, MLA_KV_RANK, MLA_ROPE, BRANCH_W, BRANCH_W, BRANCH_W,
                   N_HEADS, S5_W))
    col = lambda i: w_in[:, :, o[i]:o[i + 1]]
    misc = jnp.zeros((depth, D_MODEL, LANE), w_in.dtype)
    misc = misc.at[:, :, 0:N_HEADS].set(col(6))
    misc = misc.at[:, :, KR_LANE:KR_LANE + MLA_ROPE].set(col(2))
    wa = jnp.concatenate([col(0), col(1), col(3), col(4), col(5), col(7), misc], axis=-1)
    wg = w_in[:, :, o[8]:]

    pad_h = lambda a: jnp.pad(a, [(0, 0)] * (a.ndim - 1) + [(0, HEAD_PAD - a.shape[-1])])
    wuq = pad_h(w_uq).reshape(depth, MLA_Q_RANK, QK_W)
    wukv = jnp.concatenate(
        [pad_h(w_ukv[..., :MLA_NOPE]).reshape(depth, MLA_KV_RANK, QK_W),
         w_ukv[..., MLA_NOPE:].reshape(depth, MLA_KV_RANK, BRANCH_W)], axis=-1)
    tile_h = lambda g: jnp.tile(g, (1, N_HEADS)).reshape(depth, 1, -1)
    gq = tile_h(pad_h(f32(mla_q_norm_g))) * (MLA_QK ** -0.5 * LOG2E)
    gk = tile_h(pad_h(f32(mla_k_norm_g)))
    gfq = tile_h(f32(fox_q_norm_g)) * (FOX_HD ** -0.5 * LOG2E)
    gfk = tile_h(f32(fox_k_norm_g))
    fbias = jnp.pad(f32(fox_f_bias), ((0, 0), (0, LANE - N_HEADS))).reshape(depth, 1, LANE)

    lam_re, lam_im = f32(s5_lambda_re), f32(s5_lambda_im)
    step = jnp.exp(f32(s5_log_step))[..., None]
    mag = jnp.exp(lam_re * step)
    a_re, a_im = mag * jnp.cos(lam_im * step), mag * jnp.sin(lam_im * step)
    den = lam_re * lam_re + lam_im * lam_im
    k_re = ((a_re - 1.0) * lam_re + a_im * lam_im) / den
    k_im = (a_im * lam_re - (a_re - 1.0) * lam_im) / den
    b_re, b_im = f32(s5_b_re), f32(s5_b_im)
    bb_re = k_re[..., None] * b_re - k_im[..., None] * b_im
    bb_im = k_re[..., None] * b_im + k_im[..., None] * b_re
    eye = jnp.eye(S5_G, dtype=F32)
    bd_in = lambda a: jnp.einsum("lgph,gk->lghkp", a, eye).reshape(depth, S5_W, S5_N)
    bmat = jnp.concatenate([bd_in(bb_re), bd_in(bb_im)], axis=-1)
    bd_out = lambda a: jnp.einsum("lghp,gk->lgpkh", a, eye).reshape(depth, S5_N, S5_W)
    cmat = jnp.concatenate([bd_out(f32(s5_c_re)), -bd_out(f32(s5_c_im))], axis=1)
    lam = jnp.stack([a_re.reshape(depth, S5_N), a_im.reshape(depth, S5_N)], axis=1)

    bf = lambda a: a.astype(BF16)
    row = lambda a: f32(a).reshape(depth, 1, -1)
    tri = jnp.asarray(np.tril(np.ones((TM_PROJ, TM_PROJ), np.float32)), dtype=BF16)
    shared = {"ones_qk": _block_ones(QK_W, HEAD_PAD), "ones_fox": _block_ones(BRANCH_W, FOX_HD),
              "tri": tri}
    stacked = {
        "attn_g": row(attn_norm_g), "wa": bf(wa), "wg": bf(wg), "fbias": fbias,
        "qlat_g": row(q_lat_norm_g), "wuq": bf(wuq), "kvlat_g": row(kv_lat_norm_g),
        "wukv": bf(wukv), "gq": gq, "gk": gk, "gfq": gfq, "gfk": gfk,
        "bmat": bf(bmat), "lam": lam, "cmat": bf(cmat), "s5_d": row(s5_d),
        "wglu": bf(s5_w_glu), "bglu": row(s5_b_glu),
        "wbr": bf(w_branch), "wout": bf(w_out),
        "ffn_g": row(ffn_norm_g), "wup": bf(w_up), "conv": f32(ffn_conv_w), "wdown": bf(w_down),
    }
    return [dict(shared, **{k: v[l] for k, v in stacked.items()}) for l in range(depth)]


def kernel(x, positions, attn_norm_g, w_in, q_lat_norm_g, w_uq, kv_lat_norm_g, w_ukv, mla_q_norm_g, mla_k_norm_g, fox_q_norm_g, fox_k_norm_g, fox_f_bias, s5_lambda_re, s5_lambda_im, s5_b_re, s5_b_im, s5_c_re, s5_c_im, s5_d, s5_log_step, s5_w_glu, s5_b_glu, w_branch, w_out, ffn_norm_g, w_up, ffn_conv_w, w_down):
    batch, seq, d_model = x.shape
    assert d_model == D_MODEL and seq % TM_PROJ == 0 and seq % TQ == 0 and seq % TT_S5 == 0
    assert batch % 8 == 0
    layers = _pack_weights(attn_norm_g, w_in, q_lat_norm_g, w_uq, kv_lat_norm_g, w_ukv,
                           mla_q_norm_g, mla_k_norm_g, fox_q_norm_g, fox_k_norm_g, fox_f_bias,
                           s5_lambda_re, s5_lambda_im, s5_b_re, s5_b_im, s5_c_re, s5_c_im, s5_d,
                           s5_log_step, s5_w_glu, s5_b_glu, w_branch, w_out, ffn_norm_g, w_up,
                           ffn_conv_w, w_down)
    cos, sin = _rope_tables(positions)
    xf = x.astype(F32).reshape(batch * seq, D_MODEL)
    for w in layers:
        qm, km, vm, qf, kf, vf, u_tm = _proj(xf, cos, sin, w, batch, seq)
        o_mla = _attention(qm, km, vm, batch, seq)
        o_fox = _attention(qf, kf, vf, batch, seq)
        o_s5 = _s5(u_tm.reshape(seq * batch, S5_W), w, batch, seq)
        xf = _merge(xf, o_mla, o_fox, o_s5.reshape(seq, batch * S5_W), w, batch, seq)
        xf = _ffn(xf, w, batch, seq)
    return xf.reshape(batch, seq, D_MODEL).astype(x.dtype)
```

```python
import functools
import math

import numpy as np
import jax
import jax.numpy as jnp
from jax import lax
from jax.experimental import pallas as pl
from jax.experimental.pallas import tpu as pltpu

F32 = jnp.float32
BF16 = jnp.bfloat16

D_MODEL = 1024
N_HEADS = 4
MLA_NOPE = 64
MLA_ROPE = 32
MLA_QK = MLA_NOPE + MLA_ROPE
MLA_V = 64
MLA_Q_RANK = 384
MLA_KV_RANK = 256
FOX_HD = 64
S5_G = 16
S5_H = 16
S5_P = 64
S5_W = S5_G * S5_H
S5_N = S5_G * S5_P
BRANCH_W = 256
N_BRANCH = 3
D_FF = 2816
CONV_W = 3
ROPE_THETA = 10000.0
EPS = 1e-6
NEG = -1e30
LOG2E = math.log2(math.e)

LANE = 128
HEAD_PAD = 128
QK_W = N_HEADS * HEAD_PAD
C_CQ, C_CKV, C_FQ, C_FK, C_FV, C_U, C_MISC = 0, 384, 640, 896, 1152, 1408, 1664
WA_COLS = 1792
KR_LANE = MLA_NOPE
AUG = FOX_HD

VMEM_LIMIT = 56 * 1024 * 1024

TM_PROJ = 512
TK = 512
TT_S5 = 64
TM_MERGE = 512
TM_FFN = 512
CONV_HALO = 8


def _const_spec(shape):
    nd = len(shape)
    return pl.BlockSpec(shape, lambda *_: (0,) * nd, pipeline_mode=pl.Buffered(1))


def _params(sem):
    return pltpu.CompilerParams(dimension_semantics=sem, vmem_limit_bytes=VMEM_LIMIT)


def _rms(x, gain):
    ms = jnp.mean(x * x, axis=-1, keepdims=True)
    return x * lax.rsqrt(ms + EPS) * gain


def _split2(x):
    hi = x.astype(BF16)
    lo = (x - hi.astype(F32)).astype(BF16)
    return hi, lo


def _split3(x):
    hi = x.astype(BF16)
    r = x - hi.astype(F32)
    mid = r.astype(BF16)
    lo = (r - mid.astype(F32)).astype(BF16)
    return hi, mid, lo


def _head_norm(x, ones_ref, inv_width):
    sq = x * x
    hi, lo = _split2(sq)
    ones = ones_ref[...]
    ssum = (jnp.dot(hi, ones, preferred_element_type=F32)
            + jnp.dot(lo, ones, preferred_element_type=F32))
    return x * lax.rsqrt(ssum * inv_width + EPS)


def _rope_tiles(x, cos, sin):
    lane = lax.broadcasted_iota(jnp.int32, cos.shape, 1)
    first_half = lane < (MLA_NOPE + MLA_ROPE // 2)
    outs = []
    for h in range(N_HEADS):
        blk = x[:, h * HEAD_PAD:(h + 1) * HEAD_PAD]
        up = pltpu.roll(blk, HEAD_PAD - MLA_ROPE // 2, 1)
        dn = pltpu.roll(blk, MLA_ROPE // 2, 1)
        outs.append(blk * cos + jnp.where(first_half, up, dn) * sin)
    return jnp.concatenate(outs, axis=1)


def _pad_value_heads(v):
    lane = lax.broadcasted_iota(jnp.int32, (v.shape[0], LANE), 1)
    tiles = []
    for h in range(N_HEADS):
        base = v[:, (h // 2) * LANE:(h // 2 + 1) * LANE]
        if h % 2:
            base = pltpu.roll(base, MLA_V, 1)
        tiles.append(jnp.where(lane < MLA_V, base, jnp.where(lane == MLA_V, 1.0, 0.0)))
    return jnp.concatenate(tiles, axis=1)


def _rope_table_kernel(pos_ref, inv_ref, sign_ref, cos_ref, sin_ref):
    ang = pos_ref[...] * inv_ref[...]
    cos_ref[...] = jnp.cos(ang)
    sin_ref[...] = jnp.sin(ang) * sign_ref[...]


def _rope_tables(positions):
    m = positions.size
    tm = 1024
    pos = positions.astype(F32).reshape(m, 1)
    inv_freq = ROPE_THETA ** (-jnp.arange(0, MLA_ROPE, 2, dtype=F32) / MLA_ROPE)
    half = MLA_ROPE // 2
    inv_lane = jnp.zeros((LANE,), F32)
    inv_lane = inv_lane.at[MLA_NOPE:MLA_NOPE + half].set(inv_freq)
    inv_lane = inv_lane.at[MLA_NOPE + half:MLA_QK].set(inv_freq)
    sign = np.zeros((LANE,), np.float32)
    sign[MLA_NOPE:MLA_NOPE + half] = -1.0
    sign[MLA_NOPE + half:MLA_QK] = 1.0
    return pl.pallas_call(
        _rope_table_kernel,
        grid=(m // tm,),
        in_specs=[pl.BlockSpec((tm, 1), lambda i: (i, 0)),
                  pl.BlockSpec((1, LANE), lambda i: (0, 0)),
                  pl.BlockSpec((1, LANE), lambda i: (0, 0))],
        out_specs=[pl.BlockSpec((tm, LANE), lambda i: (i, 0))] * 2,
        out_shape=[jax.ShapeDtypeStruct((m, LANE), F32)] * 2,
        compiler_params=_params(("arbitrary",)),
        name="rope_tables",
    )(pos, inv_lane.reshape(1, LANE), jnp.asarray(sign).reshape(1, LANE))


def _proj_kernel(x_ref, cos_ref, sin_ref, g_ref, wa_ref, fbias_ref,
                 qlat_g_ref, wuq_ref, kvlat_g_ref, wukv_ref, gq_ref, gk_ref, ones_qk_ref,
                 gfq_ref, gfk_ref, ones_fox_ref, tri_ref,
                 qm_ref, km_ref, vm_ref, qf_ref, kf_ref, vf_ref, u_ref,
                 carry_ref):
    t = pl.program_id(1)
    tm = x_ref.shape[0]

    h = _rms(x_ref[...], g_ref[...]).astype(BF16)
    p = jnp.dot(h, wa_ref[...], preferred_element_type=F32)
    misc = p[:, C_MISC:C_MISC + LANE]
    lane = lax.broadcasted_iota(jnp.int32, (tm, LANE), 1)
    cos = cos_ref[...]
    sin = sin_ref[...]

    cq = _rms(p[:, C_CQ:C_CQ + MLA_Q_RANK], qlat_g_ref[...]).astype(BF16)
    q = jnp.dot(cq, wuq_ref[...], preferred_element_type=F32)
    q = _head_norm(q, ones_qk_ref, 1.0 / MLA_QK) * gq_ref[...]
    qm_ref[...] = _rope_tiles(q, cos, sin).astype(BF16)

    ckv = _rms(p[:, C_CKV:C_CKV + MLA_KV_RANK], kvlat_g_ref[...]).astype(BF16)
    kv = jnp.dot(ckv, wukv_ref[...], preferred_element_type=F32)
    k_rope = jnp.where((lane >= KR_LANE) & (lane < MLA_QK), misc, 0.0)
    k = kv[:, :QK_W] + jnp.concatenate([k_rope] * N_HEADS, axis=1)
    k = _head_norm(k, ones_qk_ref, 1.0 / MLA_QK) * gk_ref[...]
    km_ref[...] = _rope_tiles(k, cos, sin).astype(BF16)
    vm_ref[...] = _pad_value_heads(kv[:, QK_W:]).astype(BF16)

    @pl.when(t == 0)
    def _():
        carry_ref[...] = jnp.zeros_like(carry_ref)

    z = misc + fbias_ref[...]
    log_f = jnp.minimum(z, 0.0) - jnp.log1p(jnp.exp(-jnp.abs(z)))
    tri = tri_ref[...]
    c = carry_ref[0:1, :]
    for part in _split3(log_f):
        c = c + jnp.dot(tri, part, preferred_element_type=F32)
    carry_ref[0:1, :] = c[tm - 1:tm, :]
    c = c * LOG2E

    fq = _head_norm(p[:, C_FQ:C_FQ + BRANCH_W], ones_fox_ref, 1.0 / FOX_HD) * gfq_ref[...]
    fk = _head_norm(p[:, C_FK:C_FK + BRANCH_W], ones_fox_ref, 1.0 / FOX_HD) * gfk_ref[...]
    q_tiles, k_tiles = [], []
    for hd in range(N_HEADS):
        pair = slice((hd // 2) * LANE, (hd // 2 + 1) * LANE)
        qb, kb = fq[:, pair], fk[:, pair]
        if hd % 2:
            qb = pltpu.roll(qb, FOX_HD, 1)
            kb = pltpu.roll(kb, FOX_HD, 1)
        ch = c[:, hd:hd + 1]
        hi = ch.astype(BF16).astype(F32)
        r = ch - hi
        mid = r.astype(BF16).astype(F32)
        lo = r - mid
        qx = jnp.where(lane < AUG, qb,
             jnp.where(lane == AUG, hi,
             jnp.where(lane == AUG + 1, mid,
             jnp.where(lane == AUG + 2, lo,
             jnp.where(lane < AUG + 6, 1.0, 0.0)))))
        kx = jnp.where(lane < AUG, kb,
             jnp.where(lane < AUG + 3, 1.0,
             jnp.where(lane == AUG + 3, -hi,
             jnp.where(lane == AUG + 4, -mid,
             jnp.where(lane == AUG + 5, -lo, 0.0)))))
        q_tiles.append(qx)
        k_tiles.append(kx)
    qf_ref[...] = jnp.concatenate(q_tiles, axis=1).astype(BF16)
    kf_ref[...] = jnp.concatenate(k_tiles, axis=1).astype(BF16)
    vf_ref[...] = _pad_value_heads(p[:, C_FV:C_FV + BRANCH_W]).astype(BF16)

    u_ref[...] = p[:, C_U:C_U + S5_W]


def _proj(x, cos, sin, w, batch, seq):
    m = x.shape[0]
    tm = TM_PROJ
    nt = seq // tm
    row = lambda b, t: (b * nt + t, 0)
    consts = [w["attn_g"], w["wa"], w["fbias"], w["qlat_g"], w["wuq"], w["kvlat_g"], w["wukv"],
              w["gq"], w["gk"], w["ones_qk"], w["gfq"], w["gfk"], w["ones_fox"], w["tri"]]
    out_shape = [jax.ShapeDtypeStruct((m, QK_W), BF16)] * 6 + [
        jax.ShapeDtypeStruct((seq, batch * S5_W), F32)]
    out_specs = [pl.BlockSpec((tm, QK_W), row)] * 6 + [
        pl.BlockSpec((tm, S5_W), lambda b, t: (t, b))]
    return pl.pallas_call(
        _proj_kernel,
        grid=(batch, nt),
        in_specs=[pl.BlockSpec((tm, D_MODEL), row), pl.BlockSpec((tm, LANE), row),
                  pl.BlockSpec((tm, LANE), row)] + [_const_spec(c.shape) for c in consts],
        out_specs=out_specs,
        out_shape=out_shape,
        scratch_shapes=[pltpu.VMEM((8, LANE), F32)],
        compiler_params=_params(("arbitrary", "arbitrary")),
        name="proj",
    )(x, cos, sin, *consts)


def _attn_kernel(q_ref, k_ref, v_ref, o_ref, acc_ref, m_ref):
    seq = q_ref.shape[0]
    tk = TK
    causal = (lax.broadcasted_iota(jnp.int32, (tk, tk), 0)
              >= lax.broadcasted_iota(jnp.int32, (tk, tk), 1))
    low_half = lax.broadcasted_iota(jnp.int32, (tk, LANE), 1) < MLA_V

    for j in range(seq // tk):
        r0 = j * tk
        for h in range(N_HEADS):
            hs = slice(h * HEAD_PAD, (h + 1) * HEAD_PAD)
            s = lax.dot_general(q_ref[r0:, hs], k_ref[r0:r0 + tk, hs],
                                (((1,), (1,)), ((), ())), preferred_element_type=F32)
            top = jnp.where(causal, s[:tk], NEG)
            s = top if seq - r0 == tk else jnp.concatenate([top, s[tk:]], axis=0)
            m_cur = jnp.max(s, axis=1, keepdims=True)
            if j == 0:
                m_new = jnp.broadcast_to(m_cur, (seq, LANE))
            else:
                m_prev = m_ref[h, r0:, :]
                m_new = jnp.maximum(m_prev, m_cur)
                alpha = jnp.exp2(m_prev - m_new)
            p = jnp.exp2(s - jnp.concatenate([m_new] * (tk // LANE), axis=1))
            pv = jnp.dot(p.astype(BF16), v_ref[r0:r0 + tk, hs], preferred_element_type=F32)
            if j == 0:
                acc_ref[h] = pv
            else:
                acc_ref[h, r0:, :] = acc_ref[h, r0:, :] * alpha + pv
            if seq - r0 > tk:
                m_ref[h, r0:, :] = m_new
        outs = []
        for h in range(N_HEADS):
            a = acc_ref[h, r0:r0 + tk, :]
            o = a * (1.0 / a[:, MLA_V:MLA_V + 1])
            outs.append(pltpu.roll(o, MLA_V, 1) if h % 2 else o)
        o_ref[r0:r0 + tk, :] = jnp.concatenate(
            [jnp.where(low_half, outs[0], outs[1]), jnp.where(low_half, outs[2], outs[3])],
            axis=1).astype(o_ref.dtype)


def _attention(q, k, v, batch, seq):
    m = q.shape[0]
    spec = pl.BlockSpec((seq, QK_W), lambda b: (b, 0))
    return pl.pallas_call(
        _attn_kernel,
        grid=(batch,),
        in_specs=[spec, spec, spec],
        out_specs=pl.BlockSpec((seq, BRANCH_W), lambda b: (b, 0)),
        out_shape=jax.ShapeDtypeStruct((m, BRANCH_W), BF16),
        scratch_shapes=[pltpu.VMEM((N_HEADS, seq, LANE), F32),
                        pltpu.VMEM((N_HEADS, seq, LANE), F32)],
        compiler_params=_params(("arbitrary",)),
        name="attn",
    )(q, k, v)


S5_LANE_CHUNK = 512


def _s5_kernel(u_ref, bmat_ref, lam_ref, cmat_ref, d_ref, wglu_ref, bglu_ref, o_ref,
               s_ref, st_ref, *, batch):
    step_idx = pl.program_id(0)
    rows = u_ref.shape[0]
    tt = rows // batch

    @pl.when(step_idx == 0)
    def _():
        st_ref[...] = jnp.zeros_like(st_ref)

    u = u_ref[...]
    s_ref[...] = jnp.dot(u.astype(BF16), bmat_ref[...], preferred_element_type=F32)

    for c0 in range(0, S5_N, S5_LANE_CHUNK):
        re_cols = slice(c0, c0 + S5_LANE_CHUNK)
        im_cols = slice(S5_N + c0, S5_N + c0 + S5_LANE_CHUNK)
        lam_re = lam_ref[0:1, re_cols]
        lam_im = lam_ref[1:2, re_cols]

        def body(t, carry):
            x_re, x_im = carry
            r0 = pl.multiple_of(t * batch, batch)
            n_re = lam_re * x_re - lam_im * x_im + s_ref[pl.ds(r0, batch), re_cols]
            n_im = lam_re * x_im + lam_im * x_re + s_ref[pl.ds(r0, batch), im_cols]
            s_ref[pl.ds(r0, batch), re_cols] = n_re
            s_ref[pl.ds(r0, batch), im_cols] = n_im
            return n_re, n_im

        x_re, x_im = lax.fori_loop(0, tt, body, (st_ref[:, re_cols], st_ref[:, im_cols]),
                                   unroll=2)
        st_ref[:, re_cols] = x_re
        st_ref[:, im_cols] = x_im

    y = jnp.dot(s_ref[...].astype(BF16), cmat_ref[...], preferred_element_type=F32)
    y = y + d_ref[...] * u
    y = 0.5 * y * (1.0 + jnp.tanh(math.sqrt(2.0 / math.pi) * (y + 0.044715 * (y * y * y))))
    z = jnp.dot(y.astype(BF16), wglu_ref[...], preferred_element_type=F32) + bglu_ref[...]
    o_ref[...] = (y * jax.nn.sigmoid(z)).astype(o_ref.dtype)


def _s5(u_tm, w, batch, seq):
    rows = TT_S5 * batch
    consts = [w["bmat"], w["lam"], w["cmat"], w["s5_d"], w["wglu"], w["bglu"]]
    return pl.pallas_call(
        functools.partial(_s5_kernel, batch=batch),
        grid=(seq // TT_S5,),
        in_specs=[pl.BlockSpec((rows, S5_W), lambda t: (t, 0))]
                 + [_const_spec(c.shape) for c in consts],
        out_specs=pl.BlockSpec((rows, S5_W), lambda t: (t, 0)),
        out_shape=jax.ShapeDtypeStruct((seq * batch, S5_W), BF16),
        scratch_shapes=[pltpu.VMEM((rows, 2 * S5_N), F32), pltpu.VMEM((batch, 2 * S5_N), F32)],
        compiler_params=_params(("arbitrary",)),
        name="s5",
    )(u_tm, *consts)


def _merge_kernel(x_ref, om_ref, of_ref, os_ref, g_ref, wg_ref, wbr_ref, wout_ref, o_ref):
    x = x_ref[...]
    h = _rms(x, g_ref[...]).astype(BF16)
    merged = None
    for n, br_ref in enumerate((om_ref, of_ref, os_ref)):
        logits = jnp.dot(h, wg_ref[:, n * D_MODEL:(n + 1) * D_MODEL], preferred_element_type=F32)
        proj = jnp.dot(br_ref[...], wbr_ref[n], preferred_element_type=F32)
        term = jax.nn.sigmoid(logits) * proj
        merged = term if merged is None else merged + term
    o_ref[...] = x + jnp.dot(merged.astype(BF16), wout_ref[...], preferred_element_type=F32)


def _merge(x, o_mla, o_fox, o_s5_tm, w, batch, seq):
    m = x.shape[0]
    tm = TM_MERGE
    nt = seq // tm
    row = lambda b, t: (b * nt + t, 0)
    consts = [w["attn_g"], w["wg"], w["wbr"], w["wout"]]
    return pl.pallas_call(
        _merge_kernel,
        grid=(batch, nt),
        in_specs=[pl.BlockSpec((tm, D_MODEL), row), pl.BlockSpec((tm, BRANCH_W), row),
                  pl.BlockSpec((tm, BRANCH_W), row),
                  pl.BlockSpec((tm, BRANCH_W), lambda b, t: (t, b))]
                 + [_const_spec(c.shape) for c in consts],
        out_specs=pl.BlockSpec((tm, D_MODEL), row),
        out_shape=jax.ShapeDtypeStruct((m, D_MODEL), F32),
        compiler_params=_params(("arbitrary", "arbitrary")),
        name="merge",
    )(x, o_mla, o_fox, o_s5_tm, *consts)


FFN_CHUNK = 256


def _ffn_kernel(x_ref, g_ref, wup_ref, conv_ref, wdown_ref, o_ref, up_ref, act_ref):
    t = pl.program_id(1)
    tm = x_ref.shape[0]

    @pl.when(t == 0)
    def _():
        up_ref[0:CONV_HALO, :] = jnp.zeros((CONV_HALO, 2 * D_FF), F32)

    x = x_ref[...]
    h = _rms(x, g_ref[...]).astype(BF16)
    up_ref[CONV_HALO:CONV_HALO + tm, :] = jnp.dot(h, wup_ref[...], preferred_element_type=F32)

    def conv(cols):
        out = None
        for j in range(CONV_W):
            lo = CONV_HALO - (CONV_W - 1) + j
            term = conv_ref[j:j + 1, cols] * up_ref[lo:lo + tm, cols]
            out = term if out is None else out + term
        return out

    for c0 in range(0, D_FF, FFN_CHUNK):
        gate = conv(slice(c0, c0 + FFN_CHUNK))
        val = conv(slice(D_FF + c0, D_FF + c0 + FFN_CHUNK))
        act_ref[:, c0:c0 + FFN_CHUNK] = (gate * jax.nn.sigmoid(gate) * val).astype(BF16)

    up_ref[0:CONV_HALO, :] = up_ref[tm:tm + CONV_HALO, :]
    o_ref[...] = x + jnp.dot(act_ref[...], wdown_ref[...], preferred_element_type=F32)


def _ffn(x, w, batch, seq):
    m = x.shape[0]
    tm = TM_FFN
    nt = seq // tm
    row = lambda b, t: (b * nt + t, 0)
    consts = [w["ffn_g"], w["wup"], w["conv"], w["wdown"]]
    return pl.pallas_call(
        _ffn_kernel,
        grid=(batch, nt),
        in_specs=[pl.BlockSpec((tm, D_MODEL), row)] + [_const_spec(c.shape) for c in consts],
        out_specs=pl.BlockSpec((tm, D_MODEL), row),
        out_shape=jax.ShapeDtypeStruct((m, D_MODEL), F32),
        scratch_shapes=[pltpu.VMEM((tm + CONV_HALO, 2 * D_FF), F32),
                        pltpu.VMEM((tm, D_FF), BF16)],
        compiler_params=_params(("arbitrary", "arbitrary")),
        name="ffn",
    )(x, *consts)


def _block_ones(n, width):
    idx = np.arange(n) // width
    return jnp.asarray((idx[:, None] == idx[None, :]).astype(np.float32), dtype=BF16)


def _pack_weights(attn_norm_g, w_in, q_lat_norm_g, w_uq, kv_lat_norm_g, w_ukv,
                  mla_q_norm_g, mla_k_norm_g, fox_q_norm_g, fox_k_norm_g, fox_f_bias,
                  s5_lambda_re, s5_lambda_im, s5_b_re, s5_b_im, s5_c_re, s5_c_im, s5_d,
                  s5_log_step, s5_w_glu, s5_b_glu, w_branch, w_out, ffn_norm_g, w_up,
                  ffn_conv_w, w_down):
    depth = w_in.shape[0]
    f32 = lambda a: a.astype(F32)
    o = np.cumsum((0, MLA_Q_RANK, MLA_KV_RANK, MLA_ROPE, BRANCH_W, BRANCH_W, BRANCH_W,
                   N_HEADS, S5_W))
    col = lambda i: w_in[:, :, o[i]:o[i + 1]]
    misc = jnp.zeros((depth, D_MODEL, LANE), w_in.dtype)
    misc = misc.at[:, :, 0:N_HEADS].set(col(6))
    misc = misc.at[:, :, KR_LANE:KR_LANE + MLA_ROPE].set(col(2))
    wa = jnp.concatenate([col(0), col(1), col(3), col(4), col(5), col(7), misc], axis=-1)
    wg = w_in[:, :, o[8]:]

    pad_h = lambda a: jnp.pad(a, [(0, 0)] * (a.ndim - 1) + [(0, HEAD_PAD - a.shape[-1])])
    wuq = pad_h(w_uq).reshape(depth, MLA_Q_RANK, QK_W)
    wukv = jnp.concatenate(
        [pad_h(w_ukv[..., :MLA_NOPE]).reshape(depth, MLA_KV_RANK, QK_W),
         w_ukv[..., MLA_NOPE:].reshape(depth, MLA_KV_RANK, BRANCH_W)], axis=-1)
    tile_h = lambda g: jnp.tile(g, (1, N_HEADS)).reshape(depth, 1, -1)
    gq = tile_h(pad_h(f32(mla_q_norm_g))) * (MLA_QK ** -0.5 * LOG2E)
    gk = tile_h(pad_h(f32(mla_k_norm_g)))
    gfq = tile_h(f32(fox_q_norm_g)) * (FOX_HD ** -0.5 * LOG2E)
    gfk = tile_h(f32(fox_k_norm_g))
    fbias = jnp.pad(f32(fox_f_bias), ((0, 0), (0, LANE - N_HEADS))).reshape(depth, 1, LANE)

    lam_re, lam_im = f32(s5_lambda_re), f32(s5_lambda_im)
    step = jnp.exp(f32(s5_log_step))[..., None]
    mag = jnp.exp(lam_re * step)
    a_re, a_im = mag * jnp.cos(lam_im * step), mag * jnp.sin(lam_im * step)
    den = lam_re * lam_re + lam_im * lam_im
    k_re = ((a_re - 1.0) * lam_re + a_im * lam_im) / den
    k_im = (a_im * lam_re - (a_re - 1.0) * lam_im) / den
    b_re, b_im = f32(s5_b_re), f32(s5_b_im)
    bb_re = k_re[..., None] * b_re - k_im[..., None] * b_im
    bb_im = k_re[..., None] * b_im + k_im[..., None] * b_re
    eye = jnp.eye(S5_G, dtype=F32)
    bd_in = lambda a: jnp.einsum("lgph,gk->lghkp", a, eye).reshape(depth, S5_W, S5_N)
    bmat = jnp.concatenate([bd_in(bb_re), bd_in(bb_im)], axis=-1)
    bd_out = lambda a: jnp.einsum("lghp,gk->lgpkh", a, eye).reshape(depth, S5_N, S5_W)
    cmat = jnp.concatenate([bd_out(f32(s5_c_re)), -bd_out(f32(s5_c_im))], axis=1)
    lam = jnp.stack([a_re.reshape(depth, S5_N), a_im.reshape(depth, S5_N)], axis=1)

    bf = lambda a: a.astype(BF16)
    row = lambda a: f32(a).reshape(depth, 1, -1)
    tri = jnp.asarray(np.tril(np.ones((TM_PROJ, TM_PROJ), np.float32)), dtype=BF16)
    shared = {"ones_qk": _block_ones(QK_W, HEAD_PAD), "ones_fox": _block_ones(BRANCH_W, FOX_HD),
              "tri": tri}
    stacked = {
        "attn_g": row(attn_norm_g), "wa": bf(wa), "wg": bf(wg), "fbias": fbias,
        "qlat_g": row(q_lat_norm_g), "wuq": bf(wuq), "kvlat_g": row(kv_lat_norm_g),
        "wukv": bf(wukv), "gq": gq, "gk": gk, "gfq": gfq, "gfk": gfk,
        "bmat": bf(bmat), "lam": lam, "cmat": bf(cmat), "s5_d": row(s5_d),
        "wglu": bf(s5_w_glu), "bglu": row(s5_b_glu),
        "wbr": bf(w_branch), "wout": bf(w_out),
        "ffn_g": row(ffn_norm_g), "wup": bf(w_up), "conv": f32(ffn_conv_w), "wdown": bf(w_down),
    }
    return [dict(shared, **{k: v[l] for k, v in stacked.items()}) for l in range(depth)]


def kernel(x, positions, attn_norm_g, w_in, q_lat_norm_g, w_uq, kv_lat_norm_g, w_ukv, mla_q_norm_g, mla_k_norm_g, fox_q_norm_g, fox_k_norm_g, fox_f_bias, s5_lambda_re, s5_lambda_im, s5_b_re, s5_b_im, s5_c_re, s5_c_im, s5_d, s5_log_step, s5_w_glu, s5_b_glu, w_branch, w_out, ffn_norm_g, w_up, ffn_conv_w, w_down):
    batch, seq, d_model = x.shape
    assert d_model == D_MODEL and seq % TM_PROJ == 0 and seq % TK == 0 and seq % TT_S5 == 0
    assert batch % 8 == 0
    layers = _pack_weights(attn_norm_g, w_in, q_lat_norm_g, w_uq, kv_lat_norm_g, w_ukv,
                           mla_q_norm_g, mla_k_norm_g, fox_q_norm_g, fox_k_norm_g, fox_f_bias,
                           s5_lambda_re, s5_lambda_im, s5_b_re, s5_b_im, s5_c_re, s5_c_im, s5_d,
                           s5_log_step, s5_w_glu, s5_b_glu, w_branch, w_out, ffn_norm_g, w_up,
                           ffn_conv_w, w_down)
    cos, sin = _rope_tables(positions)
    xf = x.astype(F32).reshape(batch * seq, D_MODEL)
    for w in layers:
        qm, km, vm, qf, kf, vf, u_tm = _proj(xf, cos, sin, w, batch, seq)
        o_mla = _attention(qm, km, vm, batch, seq)
        o_fox = _attention(qf, kf, vf, batch, seq)
        o_s5 = _s5(u_tm.reshape(seq * batch, S5_W), w, batch, seq)
        xf = _merge(xf, o_mla, o_fox, o_s5.reshape(seq, batch * S5_W), w, batch, seq)
        xf = _ffn(xf, w, batch, seq)
    return xf.reshape(batch, seq, D_MODEL).astype(x.dtype)
```

```python
import functools
import math

import numpy as np
import jax
import jax.numpy as jnp
from jax import lax
from jax.experimental import pallas as pl
from jax.experimental.pallas import tpu as pltpu

F32 = jnp.float32
BF16 = jnp.bfloat16

D_MODEL = 1024
N_HEADS = 4
MLA_NOPE = 64
MLA_ROPE = 32
MLA_QK = MLA_NOPE + MLA_ROPE
MLA_V = 64
MLA_Q_RANK = 384
MLA_KV_RANK = 256
FOX_HD = 64
S5_G = 16
S5_H = 16
S5_P = 64
S5_W = S5_G * S5_H
S5_N = S5_G * S5_P
BRANCH_W = 256
N_BRANCH = 3
D_FF = 2816
CONV_W = 3
ROPE_THETA = 10000.0
EPS = 1e-6
NEG = -1e30
LOG2E = math.log2(math.e)

LANE = 128
HEAD_PAD = 128
QK_W = N_HEADS * HEAD_PAD
C_CQ, C_CKV, C_FQ, C_FK, C_FV, C_U, C_MISC = 0, 384, 640, 896, 1152, 1408, 1664
WA_COLS = 1792
KR_LANE = MLA_NOPE
AUG = FOX_HD

VMEM_LIMIT = 56 * 1024 * 1024

TM_PROJ = 512
CUMSUM_BLOCK = 256
TK = 512
TT_S5 = 64
TM_MERGE = 512
TM_FFN = 512
CONV_HALO = 8


def _const_spec(param):
    arr, layer = param
    if layer is None:
        return pl.BlockSpec(arr.shape, lambda *_: (0,) * arr.ndim, pipeline_mode=pl.Buffered(1))
    zeros = (0,) * (arr.ndim - 1)
    return pl.BlockSpec((None,) + arr.shape[1:], lambda *_: (layer,) + zeros,
                        pipeline_mode=pl.Buffered(1))


def _params(sem):
    return pltpu.CompilerParams(dimension_semantics=sem, vmem_limit_bytes=VMEM_LIMIT)


def _rms(x, gain):
    ms = jnp.mean(x * x, axis=-1, keepdims=True)
    return x * lax.rsqrt(ms + EPS) * gain


def _split2(x):
    hi = x.astype(BF16)
    lo = (x - hi.astype(F32)).astype(BF16)
    return hi, lo


def _split3(x):
    hi = x.astype(BF16)
    r = x - hi.astype(F32)
    mid = r.astype(BF16)
    lo = (r - mid.astype(F32)).astype(BF16)
    return hi, mid, lo


def _head_norm(x, ones_ref, inv_width):
    sq = x * x
    hi, lo = _split2(sq)
    ones = ones_ref[...]
    ssum = (jnp.dot(hi, ones, preferred_element_type=F32)
            + jnp.dot(lo, ones, preferred_element_type=F32))
    return x * lax.rsqrt(ssum * inv_width + EPS)


def _qk_norm_rope(x, gain_ref, cos, sin):
    lane = lax.broadcasted_iota(jnp.int32, cos.shape, 1)
    first_half = lane < (MLA_NOPE + MLA_ROPE // 2)
    outs = []
    for h in range(N_HEADS):
        hs = slice(h * HEAD_PAD, (h + 1) * HEAD_PAD)
        blk = x[:, hs]
        ms = jnp.sum(blk * blk, axis=1, keepdims=True) * (1.0 / MLA_QK)
        blk = blk * lax.rsqrt(ms + EPS) * gain_ref[:, hs]
        up = pltpu.roll(blk, HEAD_PAD - MLA_ROPE // 2, 1)
        dn = pltpu.roll(blk, MLA_ROPE // 2, 1)
        outs.append(blk * cos + jnp.where(first_half, up, dn) * sin)
    return jnp.concatenate(outs, axis=1)


def _pad_value_heads(v):
    lane = lax.broadcasted_iota(jnp.int32, (v.shape[0], LANE), 1)
    tiles = []
    for h in range(N_HEADS):
        base = v[:, (h // 2) * LANE:(h // 2 + 1) * LANE]
        if h % 2:
            base = pltpu.roll(base, MLA_V, 1)
        tiles.append(jnp.where(lane < MLA_V, base, jnp.where(lane == MLA_V, 1.0, 0.0)))
    return jnp.concatenate(tiles, axis=1)


def _rope_table_kernel(pos_ref, inv_ref, sign_ref, cos_ref, sin_ref):
    ang = pos_ref[...] * inv_ref[...]
    cos_ref[...] = jnp.cos(ang)
    sin_ref[...] = jnp.sin(ang) * sign_ref[...]


def _rope_tables(positions):
    m = positions.size
    tm = 1024
    pos = positions.astype(F32).reshape(m, 1)
    inv_freq = ROPE_THETA ** (-jnp.arange(0, MLA_ROPE, 2, dtype=F32) / MLA_ROPE)
    half = MLA_ROPE // 2
    inv_lane = jnp.zeros((LANE,), F32)
    inv_lane = inv_lane.at[MLA_NOPE:MLA_NOPE + half].set(inv_freq)
    inv_lane = inv_lane.at[MLA_NOPE + half:MLA_QK].set(inv_freq)
    sign = np.zeros((LANE,), np.float32)
    sign[MLA_NOPE:MLA_NOPE + half] = -1.0
    sign[MLA_NOPE + half:MLA_QK] = 1.0
    return pl.pallas_call(
        _rope_table_kernel,
        grid=(m // tm,),
        in_specs=[pl.BlockSpec((tm, 1), lambda i: (i, 0)),
                  pl.BlockSpec((1, LANE), lambda i: (0, 0)),
                  pl.BlockSpec((1, LANE), lambda i: (0, 0))],
        out_specs=[pl.BlockSpec((tm, LANE), lambda i: (i, 0))] * 2,
        out_shape=[jax.ShapeDtypeStruct((m, LANE), F32)] * 2,
        compiler_params=_params(("arbitrary",)),
        name="rope_tables",
    )(pos, inv_lane.reshape(1, LANE), jnp.asarray(sign).reshape(1, LANE))


def _proj_kernel(x_ref, cos_ref, sin_ref, g_ref, wa_ref, fbias_ref,
                 qlat_g_ref, wuq_ref, kvlat_g_ref, wukv_ref, gq_ref, gk_ref,
                 gfq_ref, gfk_ref, ones_fox_ref, tri_ref,
                 qm_ref, km_ref, vm_ref, qf_ref, kf_ref, vf_ref, u_ref,
                 carry_ref):
    t = pl.program_id(1)
    tm = x_ref.shape[0]

    h = _rms(x_ref[...], g_ref[...]).astype(BF16)
    p = jnp.dot(h, wa_ref[...], preferred_element_type=F32)
    misc = p[:, C_MISC:C_MISC + LANE]
    lane = lax.broadcasted_iota(jnp.int32, (tm, LANE), 1)
    cos = cos_ref[...]
    sin = sin_ref[...]

    cq = _rms(p[:, C_CQ:C_CQ + MLA_Q_RANK], qlat_g_ref[...]).astype(BF16)
    q = jnp.dot(cq, wuq_ref[...], preferred_element_type=F32)
    qm_ref[...] = _qk_norm_rope(q, gq_ref, cos, sin).astype(BF16)

    ckv = _rms(p[:, C_CKV:C_CKV + MLA_KV_RANK], kvlat_g_ref[...]).astype(BF16)
    kv = jnp.dot(ckv, wukv_ref[...], preferred_element_type=F32)
    k_rope = jnp.where((lane >= KR_LANE) & (lane < MLA_QK), misc, 0.0)
    k = kv[:, :QK_W] + jnp.concatenate([k_rope] * N_HEADS, axis=1)
    km_ref[...] = _qk_norm_rope(k, gk_ref, cos, sin).astype(BF16)
    vm_ref[...] = _pad_value_heads(kv[:, QK_W:]).astype(BF16)

    @pl.when(t == 0)
    def _():
        carry_ref[...] = jnp.zeros_like(carry_ref)

    z = misc + fbias_ref[...]
    log_f = jnp.minimum(z, 0.0) - jnp.log1p(jnp.exp(-jnp.abs(z)))
    tri = tri_ref[...]
    parts = _split3(log_f)
    carry = carry_ref[0:1, :]
    c_blocks = []
    for r0 in range(0, tm, CUMSUM_BLOCK):
        cb = carry
        for part in parts:
            cb = cb + jnp.dot(tri, part[r0:r0 + CUMSUM_BLOCK], preferred_element_type=F32)
        carry = cb[CUMSUM_BLOCK - 1:CUMSUM_BLOCK, :]
        c_blocks.append(cb)
    carry_ref[0:1, :] = carry
    c = jnp.concatenate(c_blocks, axis=0) * LOG2E

    fq = _head_norm(p[:, C_FQ:C_FQ + BRANCH_W], ones_fox_ref, 1.0 / FOX_HD) * gfq_ref[...]
    fk = _head_norm(p[:, C_FK:C_FK + BRANCH_W], ones_fox_ref, 1.0 / FOX_HD) * gfk_ref[...]
    q_tiles, k_tiles = [], []
    for hd in range(N_HEADS):
        pair = slice((hd // 2) * LANE, (hd // 2 + 1) * LANE)
        qb, kb = fq[:, pair], fk[:, pair]
        if hd % 2:
            qb = pltpu.roll(qb, FOX_HD, 1)
            kb = pltpu.roll(kb, FOX_HD, 1)
        ch = c[:, hd:hd + 1]
        hi = ch.astype(BF16).astype(F32)
        r = ch - hi
        mid = r.astype(BF16).astype(F32)
        lo = r - mid
        qx = jnp.where(lane < AUG, qb,
             jnp.where(lane == AUG, hi,
             jnp.where(lane == AUG + 1, mid,
             jnp.where(lane == AUG + 2, lo,
             jnp.where(lane < AUG + 6, 1.0, 0.0)))))
        kx = jnp.where(lane < AUG, kb,
             jnp.where(lane < AUG + 3, 1.0,
             jnp.where(lane == AUG + 3, -hi,
             jnp.where(lane == AUG + 4, -mid,
             jnp.where(lane == AUG + 5, -lo, 0.0)))))
        q_tiles.append(qx)
        k_tiles.append(kx)
    qf_ref[...] = jnp.concatenate(q_tiles, axis=1).astype(BF16)
    kf_ref[...] = jnp.concatenate(k_tiles, axis=1).astype(BF16)
    vf_ref[...] = _pad_value_heads(p[:, C_FV:C_FV + BRANCH_W]).astype(BF16)

    u_ref[...] = p[:, C_U:C_U + S5_W]


def _proj(x, cos, sin, w, batch, seq):
    m = x.shape[0]
    tm = TM_PROJ
    nt = seq // tm
    row = lambda b, t: (b * nt + t, 0)
    consts = [w["attn_g"], w["wa"], w["fbias"], w["qlat_g"], w["wuq"], w["kvlat_g"], w["wukv"],
              w["gq"], w["gk"], w["gfq"], w["gfk"], w["ones_fox"], w["tri"]]
    out_shape = [jax.ShapeDtypeStruct((m, QK_W), BF16)] * 6 + [
        jax.ShapeDtypeStruct((m, S5_W), F32)]
    out_specs = [pl.BlockSpec((tm, QK_W), row)] * 6 + [pl.BlockSpec((tm, S5_W), row)]
    return pl.pallas_call(
        _proj_kernel,
        grid=(batch, nt),
        in_specs=[pl.BlockSpec((tm, D_MODEL), row), pl.BlockSpec((tm, LANE), row),
                  pl.BlockSpec((tm, LANE), row)] + [_const_spec(c) for c in consts],
        out_specs=out_specs,
        out_shape=out_shape,
        scratch_shapes=[pltpu.VMEM((8, LANE), F32)],
        compiler_params=_params(("arbitrary", "arbitrary")),
        name="proj",
    )(x, cos, sin, *[c[0] for c in consts])


def _attn_kernel(q_ref, k_ref, v_ref, o_ref, acc_ref, m_ref):
    seq = q_ref.shape[0]
    tk = TK
    causal = (lax.broadcasted_iota(jnp.int32, (tk, tk), 0)
              >= lax.broadcasted_iota(jnp.int32, (tk, tk), 1))
    low_half = lax.broadcasted_iota(jnp.int32, (tk, LANE), 1) < MLA_V

    for j in range(seq // tk):
        r0 = j * tk
        for h in range(N_HEADS):
            hs = slice(h * HEAD_PAD, (h + 1) * HEAD_PAD)
            s = lax.dot_general(q_ref[r0:, hs], k_ref[r0:r0 + tk, hs],
                                (((1,), (1,)), ((), ())), preferred_element_type=F32)
            top = jnp.where(causal, s[:tk], NEG)
            s = top if seq - r0 == tk else jnp.concatenate([top, s[tk:]], axis=0)
            m_cur = jnp.max(s, axis=1, keepdims=True)
            if j == 0:
                m_new = jnp.broadcast_to(m_cur, (seq, LANE))
            else:
                m_prev = m_ref[h, r0:, :]
                m_new = jnp.maximum(m_prev, m_cur)
                alpha = jnp.exp2(m_prev - m_new)
            p = jnp.exp2(s - jnp.concatenate([m_new] * (tk // LANE), axis=1))
            pv = jnp.dot(p.astype(BF16), v_ref[r0:r0 + tk, hs], preferred_element_type=F32)
            if j == 0:
                acc_ref[h] = pv
            else:
                acc_ref[h, r0:, :] = acc_ref[h, r0:, :] * alpha + pv
            if seq - r0 > tk:
                m_ref[h, r0:, :] = m_new
        outs = []
        for h in range(N_HEADS):
            a = acc_ref[h, r0:r0 + tk, :]
            o = a * (1.0 / a[:, MLA_V:MLA_V + 1])
            outs.append(pltpu.roll(o, MLA_V, 1) if h % 2 else o)
        o_ref[r0:r0 + tk, :] = jnp.concatenate(
            [jnp.where(low_half, outs[0], outs[1]), jnp.where(low_half, outs[2], outs[3])],
            axis=1).astype(o_ref.dtype)


def _attention(q, k, v, batch, seq):
    m = q.shape[0]
    spec = pl.BlockSpec((seq, QK_W), lambda b: (b, 0))
    return pl.pallas_call(
        _attn_kernel,
        grid=(batch,),
        in_specs=[spec, spec, spec],
        out_specs=pl.BlockSpec((seq, BRANCH_W), lambda b: (b, 0)),
        out_shape=jax.ShapeDtypeStruct((m, BRANCH_W), BF16),
        scratch_shapes=[pltpu.VMEM((N_HEADS, seq, LANE), F32),
                        pltpu.VMEM((N_HEADS, seq, LANE), F32)],
        compiler_params=_params(("arbitrary",)),
        name="attn",
    )(q, k, v)


def _s5_kernel(u_ref, bmat_ref, lam_ref, cmat_ref, d_ref, wglu_ref, bglu_ref, o_ref, st_ref):
    batch, tt, _ = u_ref.shape

    @pl.when(pl.program_id(0) == 0)
    def _():
        st_ref[...] = jnp.zeros_like(st_ref)

    u = jnp.swapaxes(u_ref[...], 0, 1).reshape(tt * batch, S5_W)
    bu = jnp.dot(u.astype(BF16), bmat_ref[...], preferred_element_type=F32)
    lam_re = lam_ref[0:1, :]
    lam_im = lam_ref[1:2, :]
    x_re = st_ref[:, :S5_N]
    x_im = st_ref[:, S5_N:]
    states = []
    for t in range(tt):
        r = slice(t * batch, (t + 1) * batch)
        n_re = lam_re * x_re - lam_im * x_im + bu[r, :S5_N]
        n_im = lam_re * x_im + lam_im * x_re + bu[r, S5_N:]
        x_re, x_im = n_re, n_im
        states.append(jnp.concatenate([n_re, n_im], axis=1).astype(BF16))
    st_ref[:, :S5_N] = x_re
    st_ref[:, S5_N:] = x_im

    y = jnp.dot(jnp.concatenate(states, axis=0), cmat_ref[...], preferred_element_type=F32)
    y = y + d_ref[...] * u
    y = 0.5 * y * (1.0 + jnp.tanh(math.sqrt(2.0 / math.pi) * (y + 0.044715 * (y * y * y))))
    z = jnp.dot(y.astype(BF16), wglu_ref[...], preferred_element_type=F32) + bglu_ref[...]
    o = (y * jax.nn.sigmoid(z)).reshape(tt, batch, S5_W)
    o_ref[...] = jnp.swapaxes(o, 0, 1).astype(o_ref.dtype)


def _s5(u, w, batch, seq):
    consts = [w["bmat"], w["lam"], w["cmat"], w["s5_d"], w["wglu"], w["bglu"]]
    blk = pl.BlockSpec((batch, TT_S5, S5_W), lambda t: (0, t, 0))
    return pl.pallas_call(
        _s5_kernel,
        grid=(seq // TT_S5,),
        in_specs=[blk] + [_const_spec(c) for c in consts],
        out_specs=blk,
        out_shape=jax.ShapeDtypeStruct((batch, seq, S5_W), BF16),
        scratch_shapes=[pltpu.VMEM((batch, 2 * S5_N), F32)],
        compiler_params=_params(("arbitrary",)),
        name="s5",
    )(u.reshape(batch, seq, S5_W), *[c[0] for c in consts]).reshape(batch * seq, S5_W)


def _merge_kernel(x_ref, om_ref, of_ref, os_ref, g_ref, wg_ref, wbr_ref, wout_ref, o_ref):
    x = x_ref[...]
    h = _rms(x, g_ref[...]).astype(BF16)
    merged = None
    for n, br_ref in enumerate((om_ref, of_ref, os_ref)):
        logits = jnp.dot(h, wg_ref[:, n * D_MODEL:(n + 1) * D_MODEL], preferred_element_type=F32)
        proj = jnp.dot(br_ref[...], wbr_ref[n], preferred_element_type=F32)
        term = jax.nn.sigmoid(logits) * proj
        merged = term if merged is None else merged + term
    o_ref[...] = x + jnp.dot(merged.astype(BF16), wout_ref[...], preferred_element_type=F32)


def _merge(x, o_mla, o_fox, o_s5, w, batch, seq):
    m = x.shape[0]
    tm = TM_MERGE
    nt = seq // tm
    row = lambda b, t: (b * nt + t, 0)
    consts = [w["attn_g"], w["wg"], w["wbr"], w["wout"]]
    return pl.pallas_call(
        _merge_kernel,
        grid=(batch, nt),
        in_specs=[pl.BlockSpec((tm, D_MODEL), row)] + [pl.BlockSpec((tm, BRANCH_W), row)] * 3
                 + [_const_spec(c) for c in consts],
        out_specs=pl.BlockSpec((tm, D_MODEL), row),
        out_shape=jax.ShapeDtypeStruct((m, D_MODEL), F32),
        compiler_params=_params(("arbitrary", "arbitrary")),
        name="merge",
    )(x, o_mla, o_fox, o_s5, *[c[0] for c in consts])


FFN_CHUNK = 256


def _ffn_kernel(x_ref, g_ref, wup_ref, conv_ref, wdown_ref, o_ref, up_ref, act_ref):
    t = pl.program_id(1)
    tm = x_ref.shape[0]

    @pl.when(t == 0)
    def _():
        up_ref[0:CONV_HALO, :] = jnp.zeros((CONV_HALO, 2 * D_FF), F32)

    x = x_ref[...]
    h = _rms(x, g_ref[...]).astype(BF16)
    up_ref[CONV_HALO:CONV_HALO + tm, :] = jnp.dot(h, wup_ref[...], preferred_element_type=F32)

    def conv(cols):
        out = None
        for j in range(CONV_W):
            lo = CONV_HALO - (CONV_W - 1) + j
            term = conv_ref[j:j + 1, cols] * up_ref[lo:lo + tm, cols]
            out = term if out is None else out + term
        return out

    for c0 in range(0, D_FF, FFN_CHUNK):
        gate = conv(slice(c0, c0 + FFN_CHUNK))
        val = conv(slice(D_FF + c0, D_FF + c0 + FFN_CHUNK))
        act_ref[:, c0:c0 + FFN_CHUNK] = (gate * jax.nn.sigmoid(gate) * val).astype(BF16)

    up_ref[0:CONV_HALO, :] = up_ref[tm:tm + CONV_HALO, :]
    o_ref[...] = x + jnp.dot(act_ref[...], wdown_ref[...], preferred_element_type=F32)


def _ffn(x, w, batch, seq):
    m = x.shape[0]
    tm = TM_FFN
    nt = seq // tm
    row = lambda b, t: (b * nt + t, 0)
    consts = [w["ffn_g"], w["wup"], w["conv"], w["wdown"]]
    return pl.pallas_call(
        _ffn_kernel,
        grid=(batch, nt),
        in_specs=[pl.BlockSpec((tm, D_MODEL), row)] + [_const_spec(c) for c in consts],
        out_specs=pl.BlockSpec((tm, D_MODEL), row),
        out_shape=jax.ShapeDtypeStruct((m, D_MODEL), F32),
        scratch_shapes=[pltpu.VMEM((tm + CONV_HALO, 2 * D_FF), F32),
                        pltpu.VMEM((tm, D_FF), BF16)],
        compiler_params=_params(("arbitrary", "arbitrary")),
        name="ffn",
    )(x, *[c[0] for c in consts])


def _block_ones(n, width):
    idx = np.arange(n) // width
    return jnp.asarray((idx[:, None] == idx[None, :]).astype(np.float32), dtype=BF16)


def _pack_weights(attn_norm_g, w_in, q_lat_norm_g, w_uq, kv_lat_norm_g, w_ukv,
                  mla_q_norm_g, mla_k_norm_g, fox_q_norm_g, fox_k_norm_g, fox_f_bias,
                  s5_lambda_re, s5_lambda_im, s5_b_re, s5_b_im, s5_c_re, s5_c_im, s5_d,
                  s5_log_step, s5_w_glu, s5_b_glu, w_branch, w_out, ffn_norm_g, w_up,
                  ffn_conv_w, w_down):
    depth = w_in.shape[0]
    f32 = lambda a: a.astype(F32)
    o = np.cumsum((0, MLA_Q_RANK, MLA_KV_RANK, MLA_ROPE, BRANCH_W, BRANCH_W, BRANCH_W,
                   N_HEADS, S5_W))
    col = lambda i: w_in[:, :, o[i]:o[i + 1]]
    misc = jnp.zeros((depth, D_MODEL, LANE), w_in.dtype)
    misc = misc.at[:, :, 0:N_HEADS].set(col(6))
    misc = misc.at[:, :, KR_LANE:KR_LANE + MLA_ROPE].set(col(2))
    wa = jnp.concatenate([col(0), col(1), col(3), col(4), col(5), col(7), misc], axis=-1)
    wg = w_in[:, :, o[8]:]

    pad_h = lambda a: jnp.pad(a, [(0, 0)] * (a.ndim - 1) + [(0, HEAD_PAD - a.shape[-1])])
    wuq = pad_h(w_uq).reshape(depth, MLA_Q_RANK, QK_W)
    wukv = jnp.concatenate(
        [pad_h(w_ukv[..., :MLA_NOPE]).reshape(depth, MLA_KV_RANK, QK_W),
         w_ukv[..., MLA_NOPE:].reshape(depth, MLA_KV_RANK, BRANCH_W)], axis=-1)
    tile_h = lambda g: jnp.tile(g, (1, N_HEADS)).reshape(depth, 1, -1)
    gq = tile_h(pad_h(f32(mla_q_norm_g))) * (MLA_QK ** -0.5 * LOG2E)
    gk = tile_h(pad_h(f32(mla_k_norm_g)))
    gfq = tile_h(f32(fox_q_norm_g)) * (FOX_HD ** -0.5 * LOG2E)
    gfk = tile_h(f32(fox_k_norm_g))
    fbias = jnp.pad(f32(fox_f_bias), ((0, 0), (0, LANE - N_HEADS))).reshape(depth, 1, LANE)

    lam_re, lam_im = f32(s5_lambda_re), f32(s5_lambda_im)
    step = jnp.exp(f32(s5_log_step))[..., None]
    mag = jnp.exp(lam_re * step)
    a_re, a_im = mag * jnp.cos(lam_im * step), mag * jnp.sin(lam_im * step)
    den = lam_re * lam_re + lam_im * lam_im
    k_re = ((a_re - 1.0) * lam_re + a_im * lam_im) / den
    k_im = (a_im * lam_re - (a_re - 1.0) * lam_im) / den
    b_re, b_im = f32(s5_b_re), f32(s5_b_im)
    bb_re = k_re[..., None] * b_re - k_im[..., None] * b_im
    bb_im = k_re[..., None] * b_im + k_im[..., None] * b_re
    eye = jnp.eye(S5_G, dtype=F32)
    bd_in = lambda a: jnp.einsum("lgph,gk->lghkp", a, eye).reshape(depth, S5_W, S5_N)
    bmat = jnp.concatenate([bd_in(bb_re), bd_in(bb_im)], axis=-1)
    bd_out = lambda a: jnp.einsum("lghp,gk->lgpkh", a, eye).reshape(depth, S5_N, S5_W)
    cmat = jnp.concatenate([bd_out(f32(s5_c_re)), -bd_out(f32(s5_c_im))], axis=1)
    lam = jnp.stack([a_re.reshape(depth, S5_N), a_im.reshape(depth, S5_N)], axis=1)

    bf = lambda a: a.astype(BF16)
    row = lambda a: f32(a).reshape(depth, 1, -1)
    tri = jnp.asarray(np.tril(np.ones((CUMSUM_BLOCK, CUMSUM_BLOCK), np.float32)), dtype=BF16)
    shared = {"ones_fox": _block_ones(BRANCH_W, FOX_HD), "tri": tri}
    stacked = {
        "attn_g": row(attn_norm_g), "wa": bf(wa), "wg": bf(wg), "fbias": fbias,
        "qlat_g": row(q_lat_norm_g), "wuq": bf(wuq), "kvlat_g": row(kv_lat_norm_g),
        "wukv": bf(wukv), "gq": gq, "gk": gk, "gfq": gfq, "gfk": gfk,
        "bmat": bf(bmat), "lam": lam, "cmat": bf(cmat), "s5_d": row(s5_d),
        "wglu": bf(s5_w_glu), "bglu": row(s5_b_glu),
        "wbr": bf(w_branch), "wout": bf(w_out),
        "ffn_g": row(ffn_norm_g), "wup": bf(w_up), "conv": f32(ffn_conv_w), "wdown": bf(w_down),
    }
    return [dict({k: (v, None) for k, v in shared.items()},
                 **{k: (v, l) for k, v in stacked.items()}) for l in range(depth)]


def kernel(x, positions, attn_norm_g, w_in, q_lat_norm_g, w_uq, kv_lat_norm_g, w_ukv, mla_q_norm_g, mla_k_norm_g, fox_q_norm_g, fox_k_norm_g, fox_f_bias, s5_lambda_re, s5_lambda_im, s5_b_re, s5_b_im, s5_c_re, s5_c_im, s5_d, s5_log_step, s5_w_glu, s5_b_glu, w_branch, w_out, ffn_norm_g, w_up, ffn_conv_w, w_down):
    batch, seq, d_model = x.shape
    assert d_model == D_MODEL and seq % TM_PROJ == 0 and seq % TK == 0 and seq % TT_S5 == 0
    assert batch % 8 == 0
    layers = _pack_weights(attn_norm_g, w_in, q_lat_norm_g, w_uq, kv_lat_norm_g, w_ukv,
                           mla_q_norm_g, mla_k_norm_g, fox_q_norm_g, fox_k_norm_g, fox_f_bias,
                           s5_lambda_re, s5_lambda_im, s5_b_re, s5_b_im, s5_c_re, s5_c_im, s5_d,
                           s5_log_step, s5_w_glu, s5_b_glu, w_branch, w_out, ffn_norm_g, w_up,
                           ffn_conv_w, w_down)
    cos, sin = _rope_tables(positions)
    xf = x.astype(F32).reshape(batch * seq, D_MODEL)
    for w in layers:
        qm, km, vm, qf, kf, vf, u = _proj(xf, cos, sin, w, batch, seq)
        o_mla = _attention(qm, km, vm, batch, seq)
        o_fox = _attention(qf, kf, vf, batch, seq)
        o_s5 = _s5(u, w, batch, seq)
        xf = _merge(xf, o_mla, o_fox, o_s5, w, batch, seq)
        xf = _ffn(xf, w, batch, seq)
    return xf.reshape(batch, seq, D_MODEL).astype(x.dtype)
```

```python
import functools
import math

import numpy as np
import jax
import jax.numpy as jnp
from jax import lax
from jax.experimental import pallas as pl
from jax.experimental.pallas import tpu as pltpu

F32 = jnp.float32
BF16 = jnp.bfloat16

D_MODEL = 1024
N_HEADS = 4
MLA_NOPE = 64
MLA_ROPE = 32
MLA_QK = MLA_NOPE + MLA_ROPE
MLA_V = 64
MLA_Q_RANK = 384
MLA_KV_RANK = 256
FOX_HD = 64
S5_G = 16
S5_H = 16
S5_P = 64
S5_W = S5_G * S5_H
S5_N = S5_G * S5_P
BRANCH_W = 256
N_BRANCH = 3
D_FF = 2816
CONV_W = 3
ROPE_THETA = 10000.0
EPS = 1e-6
NEG = -1e30
LOG2E = math.log2(math.e)

LANE = 128
HEAD_PAD = 128
QK_W = N_HEADS * HEAD_PAD
C_CQ, C_CKV, C_FQ, C_FK, C_FV, C_U, C_MISC = 0, 384, 640, 896, 1152, 1408, 1664
WA_COLS = 1792
KR_LANE = MLA_NOPE
AUG = FOX_HD

VMEM_LIMIT = 56 * 1024 * 1024

TM_PROJ = 512
CUMSUM_BLOCK = 256
TK = 512
TT_S5 = 64
TM_MERGE = 512
TM_FFN = 512
CONV_HALO = 8


def _const_spec(param):
    arr, layer = param
    if layer is None:
        return pl.BlockSpec(arr.shape, lambda *_: (0,) * arr.ndim, pipeline_mode=pl.Buffered(1))
    zeros = (0,) * (arr.ndim - 1)
    return pl.BlockSpec((None,) + arr.shape[1:], lambda *_: (layer,) + zeros,
                        pipeline_mode=pl.Buffered(1))


def _params(sem):
    return pltpu.CompilerParams(dimension_semantics=sem, vmem_limit_bytes=VMEM_LIMIT)


def _rms(x, gain):
    ms = jnp.mean(x * x, axis=-1, keepdims=True)
    return x * lax.rsqrt(ms + EPS) * gain


def _split2(x):
    hi = x.astype(BF16)
    lo = (x - hi.astype(F32)).astype(BF16)
    return hi, lo


def _split3(x):
    hi = x.astype(BF16)
    r = x - hi.astype(F32)
    mid = r.astype(BF16)
    lo = (r - mid.astype(F32)).astype(BF16)
    return hi, mid, lo


def _head_sums(sq, ones_ref):
    sq = sq.astype(BF16)
    ones = ones_ref[...]
    outs = [jnp.dot(sq[:, c0:c0 + 2 * LANE], ones, preferred_element_type=F32)
            for c0 in range(0, sq.shape[1], 2 * LANE)]
    return outs[0] if len(outs) == 1 else jnp.concatenate(outs, axis=1)


def _pad_value_heads(v):
    lane = lax.broadcasted_iota(jnp.int32, (v.shape[0], LANE), 1)
    tiles = []
    for h in range(N_HEADS):
        base = v[:, (h // 2) * LANE:(h // 2 + 1) * LANE]
        if h % 2:
            base = pltpu.roll(base, MLA_V, 1)
        tiles.append(jnp.where(lane < MLA_V, base, jnp.where(lane == MLA_V, 1.0, 0.0)))
    return jnp.concatenate(tiles, axis=1)


def _rope_table_kernel(pos_ref, inv_ref, sign_ref, cos_ref, sin_ref):
    ang = pos_ref[...] * inv_ref[...]
    cos_ref[...] = jnp.cos(ang)
    sin_ref[...] = jnp.sin(ang) * sign_ref[...]


def _rope_tables(positions):
    m = positions.size
    tm = 1024
    pos = positions.astype(F32).reshape(m, 1)
    inv_freq = ROPE_THETA ** (-jnp.arange(0, MLA_ROPE, 2, dtype=F32) / MLA_ROPE)
    half = MLA_ROPE // 2
    inv_lane = jnp.zeros((LANE,), F32)
    inv_lane = inv_lane.at[MLA_NOPE:MLA_NOPE + half].set(inv_freq)
    inv_lane = inv_lane.at[MLA_NOPE + half:MLA_QK].set(inv_freq)
    sign = np.zeros((LANE,), np.float32)
    sign[MLA_NOPE:MLA_NOPE + half] = -1.0
    sign[MLA_NOPE + half:MLA_QK] = 1.0
    return pl.pallas_call(
        _rope_table_kernel,
        grid=(m // tm,),
        in_specs=[pl.BlockSpec((tm, 1), lambda i: (i, 0)),
                  pl.BlockSpec((1, LANE), lambda i: (0, 0)),
                  pl.BlockSpec((1, LANE), lambda i: (0, 0))],
        out_specs=[pl.BlockSpec((tm, LANE), lambda i: (i, 0))] * 2,
        out_shape=[jax.ShapeDtypeStruct((m, LANE), F32)] * 2,
        compiler_params=_params(("arbitrary",)),
        name="rope_tables",
    )(pos, inv_lane.reshape(1, LANE), jnp.asarray(sign).reshape(1, LANE))


def _proj_kernel(x_ref, cos_ref, sin_ref, g_ref, wa_ref, fbias_ref,
                 qlat_g_ref, wuq_ref, kvlat_g_ref, wukv_ref, gq_ref, gk_ref, ones_qk_ref,
                 gfq_ref, gfk_ref, ones_fox_ref, tri_ref,
                 qm_ref, km_ref, vm_ref, qf_ref, kf_ref, vf_ref, u_ref,
                 carry_ref):
    t = pl.program_id(1)
    tm = x_ref.shape[0]

    h = _rms(x_ref[...], g_ref[...]).astype(BF16)
    p = jnp.dot(h, wa_ref[...], preferred_element_type=F32)
    misc = p[:, C_MISC:C_MISC + LANE]
    lane = lax.broadcasted_iota(jnp.int32, (tm, LANE), 1)
    cos = cos_ref[...]
    sin = sin_ref[...]

    cq = _rms(p[:, C_CQ:C_CQ + MLA_Q_RANK], qlat_g_ref[...]).astype(BF16)
    qq = jnp.dot(cq, wuq_ref[...], preferred_element_type=F32)
    q = qq[:, :QK_W]
    cos4 = jnp.concatenate([cos] * N_HEADS, axis=1)
    sin4 = jnp.concatenate([sin] * N_HEADS, axis=1)
    r = lax.rsqrt(_head_sums(q * q, ones_qk_ref) * (1.0 / MLA_QK) + EPS)
    qm_ref[...] = (r * (q * gq_ref[...] * cos4 + qq[:, QK_W:] * sin4)).astype(BF16)

    ckv = _rms(p[:, C_CKV:C_CKV + MLA_KV_RANK], kvlat_g_ref[...]).astype(BF16)
    kv = jnp.dot(ckv, wukv_ref[...], preferred_element_type=F32)
    k_rope = jnp.where((lane >= KR_LANE) & (lane < MLA_QK), misc, 0.0)
    k = kv[:, :QK_W] + jnp.concatenate([k_rope] * N_HEADS, axis=1)
    r = lax.rsqrt(_head_sums(k * k, ones_qk_ref) * (1.0 / MLA_QK) + EPS)
    krg = k_rope * gk_ref[:, :LANE]
    first_half = lane < (MLA_NOPE + MLA_ROPE // 2)
    partner = jnp.where(first_half, pltpu.roll(krg, HEAD_PAD - MLA_ROPE // 2, 1),
                        pltpu.roll(krg, MLA_ROPE // 2, 1)) * sin
    km_ref[...] = (r * (k * gk_ref[...] * cos4
                        + jnp.concatenate([partner] * N_HEADS, axis=1))).astype(BF16)
    vm_ref[...] = _pad_value_heads(kv[:, QK_W:]).astype(BF16)

    @pl.when(t == 0)
    def _():
        carry_ref[...] = jnp.zeros_like(carry_ref)

    z = misc + fbias_ref[...]
    log_f = jnp.minimum(z, 0.0) - jnp.log1p(jnp.exp(-jnp.abs(z)))
    tri = tri_ref[...]
    parts = _split3(log_f)
    carry = carry_ref[0:1, :]
    c_blocks = []
    for r0 in range(0, tm, CUMSUM_BLOCK):
        cb = carry
        for part in parts:
            cb = cb + jnp.dot(tri, part[r0:r0 + CUMSUM_BLOCK], preferred_element_type=F32)
        carry = cb[CUMSUM_BLOCK - 1:CUMSUM_BLOCK, :]
        c_blocks.append(cb)
    carry_ref[0:1, :] = carry
    c = jnp.concatenate(c_blocks, axis=0) * LOG2E

    fq = p[:, C_FQ:C_FQ + BRANCH_W]
    fq = fq * lax.rsqrt(_head_sums(fq * fq, ones_fox_ref) * (1.0 / FOX_HD) + EPS) * gfq_ref[...]
    fk = p[:, C_FK:C_FK + BRANCH_W]
    fk = fk * lax.rsqrt(_head_sums(fk * fk, ones_fox_ref) * (1.0 / FOX_HD) + EPS) * gfk_ref[...]
    q_tiles, k_tiles = [], []
    for hd in range(N_HEADS):
        pair = slice((hd // 2) * LANE, (hd // 2 + 1) * LANE)
        qb, kb = fq[:, pair], fk[:, pair]
        if hd % 2:
            qb = pltpu.roll(qb, FOX_HD, 1)
            kb = pltpu.roll(kb, FOX_HD, 1)
        ch = c[:, hd:hd + 1]
        hi = ch.astype(BF16).astype(F32)
        r = ch - hi
        mid = r.astype(BF16).astype(F32)
        lo = r - mid
        qx = jnp.where(lane < AUG, qb,
             jnp.where(lane == AUG, hi,
             jnp.where(lane == AUG + 1, mid,
             jnp.where(lane == AUG + 2, lo,
             jnp.where(lane < AUG + 6, 1.0, 0.0)))))
        kx = jnp.where(lane < AUG, kb,
             jnp.where(lane < AUG + 3, 1.0,
             jnp.where(lane == AUG + 3, -hi,
             jnp.where(lane == AUG + 4, -mid,
             jnp.where(lane == AUG + 5, -lo, 0.0)))))
        q_tiles.append(qx)
        k_tiles.append(kx)
    qf_ref[...] = jnp.concatenate(q_tiles, axis=1).astype(BF16)
    kf_ref[...] = jnp.concatenate(k_tiles, axis=1).astype(BF16)
    vf_ref[...] = _pad_value_heads(p[:, C_FV:C_FV + BRANCH_W]).astype(BF16)

    u_ref[...] = p[:, C_U:C_U + S5_W]


def _proj(x, cos, sin, w, batch, seq):
    m = x.shape[0]
    tm = TM_PROJ
    nt = seq // tm
    row = lambda b, t: (b * nt + t, 0)
    consts = [w["attn_g"], w["wa"], w["fbias"], w["qlat_g"], w["wuq"], w["kvlat_g"], w["wukv"],
              w["gq"], w["gk"], w["ones_qk"], w["gfq"], w["gfk"], w["ones_fox"], w["tri"]]
    out_shape = [jax.ShapeDtypeStruct((m, QK_W), BF16)] * 6 + [
        jax.ShapeDtypeStruct((m, S5_W), F32)]
    out_specs = [pl.BlockSpec((tm, QK_W), row)] * 6 + [pl.BlockSpec((tm, S5_W), row)]
    return pl.pallas_call(
        _proj_kernel,
        grid=(batch, nt),
        in_specs=[pl.BlockSpec((tm, D_MODEL), row), pl.BlockSpec((tm, LANE), row),
                  pl.BlockSpec((tm, LANE), row)] + [_const_spec(c) for c in consts],
        out_specs=out_specs,
        out_shape=out_shape,
        scratch_shapes=[pltpu.VMEM((8, LANE), F32)],
        compiler_params=_params(("arbitrary", "arbitrary")),
        name="proj",
    )(x, cos, sin, *[c[0] for c in consts])


def _attn_kernel(q_ref, k_ref, v_ref, o_ref, acc_ref, m_ref):
    seq = q_ref.shape[0]
    tk = TK
    causal = (lax.broadcasted_iota(jnp.int32, (tk, tk), 0)
              >= lax.broadcasted_iota(jnp.int32, (tk, tk), 1))
    low_half = lax.broadcasted_iota(jnp.int32, (tk, LANE), 1) < MLA_V

    for j in range(seq // tk):
        r0 = j * tk
        for h in range(N_HEADS):
            hs = slice(h * HEAD_PAD, (h + 1) * HEAD_PAD)
            s = lax.dot_general(q_ref[r0:, hs], k_ref[r0:r0 + tk, hs],
                                (((1,), (1,)), ((), ())), preferred_element_type=F32)
            top = jnp.where(causal, s[:tk], NEG)
            s = top if seq - r0 == tk else jnp.concatenate([top, s[tk:]], axis=0)
            m_cur = jnp.max(s, axis=1, keepdims=True)
            if j == 0:
                m_new = jnp.broadcast_to(m_cur, (seq, LANE))
            else:
                m_prev = m_ref[h, r0:, :]
                m_new = jnp.maximum(m_prev, m_cur)
                alpha = jnp.exp2(m_prev - m_new)
            p = jnp.exp2(s - jnp.concatenate([m_new] * (tk // LANE), axis=1))
            pv = jnp.dot(p.astype(BF16), v_ref[r0:r0 + tk, hs], preferred_element_type=F32)
            if j == 0:
                acc_ref[h] = pv
            else:
                acc_ref[h, r0:, :] = acc_ref[h, r0:, :] * alpha + pv
            if seq - r0 > tk:
                m_ref[h, r0:, :] = m_new
        outs = []
        for h in range(N_HEADS):
            a = acc_ref[h, r0:r0 + tk, :]
            o = a * (1.0 / a[:, MLA_V:MLA_V + 1])
            outs.append(pltpu.roll(o, MLA_V, 1) if h % 2 else o)
        o_ref[r0:r0 + tk, :] = jnp.concatenate(
            [jnp.where(low_half, outs[0], outs[1]), jnp.where(low_half, outs[2], outs[3])],
            axis=1).astype(o_ref.dtype)


def _attention(q, k, v, batch, seq):
    m = q.shape[0]
    spec = pl.BlockSpec((seq, QK_W), lambda b: (b, 0))
    return pl.pallas_call(
        _attn_kernel,
        grid=(batch,),
        in_specs=[spec, spec, spec],
        out_specs=pl.BlockSpec((seq, BRANCH_W), lambda b: (b, 0)),
        out_shape=jax.ShapeDtypeStruct((m, BRANCH_W), BF16),
        scratch_shapes=[pltpu.VMEM((N_HEADS, seq, LANE), F32),
                        pltpu.VMEM((N_HEADS, seq, LANE), F32)],
        compiler_params=_params(("arbitrary",)),
        name="attn",
    )(q, k, v)


def _s5_kernel(u_ref, bmat_ref, lam_ref, cmat_ref, d_ref, wglu_ref, bglu_ref, o_ref, st_ref):
    batch, tt, _ = u_ref.shape

    @pl.when(pl.program_id(0) == 0)
    def _():
        st_ref[...] = jnp.zeros_like(st_ref)

    u = jnp.swapaxes(u_ref[...], 0, 1).reshape(tt * batch, S5_W)
    bu = jnp.dot(u.astype(BF16), bmat_ref[...], preferred_element_type=F32)
    lam_re = lam_ref[0:1, :]
    lam_im = lam_ref[1:2, :]
    x_re = st_ref[:, :S5_N]
    x_im = st_ref[:, S5_N:]
    states = []
    for t in range(tt):
        r = slice(t * batch, (t + 1) * batch)
        n_re = lam_re * x_re - lam_im * x_im + bu[r, :S5_N]
        n_im = lam_re * x_im + lam_im * x_re + bu[r, S5_N:]
        x_re, x_im = n_re, n_im
        states.append(jnp.concatenate([n_re, n_im], axis=1).astype(BF16))
    st_ref[:, :S5_N] = x_re
    st_ref[:, S5_N:] = x_im

    y = jnp.dot(jnp.concatenate(states, axis=0), cmat_ref[...], preferred_element_type=F32)
    y = y + d_ref[...] * u
    y = 0.5 * y * (1.0 + jnp.tanh(math.sqrt(2.0 / math.pi) * (y + 0.044715 * (y * y * y))))
    z = jnp.dot(y.astype(BF16), wglu_ref[...], preferred_element_type=F32) + bglu_ref[...]
    o = (y * jax.nn.sigmoid(z)).reshape(tt, batch, S5_W)
    o_ref[...] = jnp.swapaxes(o, 0, 1).astype(o_ref.dtype)


def _s5(u, w, batch, seq):
    consts = [w["bmat"], w["lam"], w["cmat"], w["s5_d"], w["wglu"], w["bglu"]]
    blk = pl.BlockSpec((batch, TT_S5, S5_W), lambda t: (0, t, 0))
    return pl.pallas_call(
        _s5_kernel,
        grid=(seq // TT_S5,),
        in_specs=[blk] + [_const_spec(c) for c in consts],
        out_specs=blk,
        out_shape=jax.ShapeDtypeStruct((batch, seq, S5_W), BF16),
        scratch_shapes=[pltpu.VMEM((batch, 2 * S5_N), F32)],
        compiler_params=_params(("arbitrary",)),
        name="s5",
    )(u.reshape(batch, seq, S5_W), *[c[0] for c in consts]).reshape(batch * seq, S5_W)


def _merge_kernel(x_ref, om_ref, of_ref, os_ref, g_ref, wg_ref, wbr_ref, wout_ref, o_ref):
    x = x_ref[...]
    h = _rms(x, g_ref[...]).astype(BF16)
    merged = None
    for n, br_ref in enumerate((om_ref, of_ref, os_ref)):
        logits = jnp.dot(h, wg_ref[:, n * D_MODEL:(n + 1) * D_MODEL], preferred_element_type=F32)
        proj = jnp.dot(br_ref[...], wbr_ref[n], preferred_element_type=F32)
        term = jax.nn.sigmoid(logits) * proj
        merged = term if merged is None else merged + term
    o_ref[...] = x + jnp.dot(merged.astype(BF16), wout_ref[...], preferred_element_type=F32)


def _merge(x, o_mla, o_fox, o_s5, w, batch, seq):
    m = x.shape[0]
    tm = TM_MERGE
    nt = seq // tm
    row = lambda b, t: (b * nt + t, 0)
    consts = [w["attn_g"], w["wg"], w["wbr"], w["wout"]]
    return pl.pallas_call(
        _merge_kernel,
        grid=(batch, nt),
        in_specs=[pl.BlockSpec((tm, D_MODEL), row)] + [pl.BlockSpec((tm, BRANCH_W), row)] * 3
                 + [_const_spec(c) for c in consts],
        out_specs=pl.BlockSpec((tm, D_MODEL), row),
        out_shape=jax.ShapeDtypeStruct((m, D_MODEL), F32),
        compiler_params=_params(("arbitrary", "arbitrary")),
        name="merge",
    )(x, o_mla, o_fox, o_s5, *[c[0] for c in consts])


FFN_CHUNK = 256


def _ffn_kernel(x_ref, g_ref, wup_ref, conv_ref, wdown_ref, o_ref, up_ref, act_ref):
    t = pl.program_id(1)
    tm = x_ref.shape[0]

    @pl.when(t == 0)
    def _():
        up_ref[0:CONV_HALO, :] = jnp.zeros((CONV_HALO, 2 * D_FF), F32)

    x = x_ref[...]
    h = _rms(x, g_ref[...]).astype(BF16)
    up_ref[CONV_HALO:CONV_HALO + tm, :] = jnp.dot(h, wup_ref[...], preferred_element_type=F32)

    def conv(cols):
        out = None
        for j in range(CONV_W):
            lo = CONV_HALO - (CONV_W - 1) + j
            term = conv_ref[j:j + 1, cols] * up_ref[lo:lo + tm, cols]
            out = term if out is None else out + term
        return out

    for c0 in range(0, D_FF, FFN_CHUNK):
        gate = conv(slice(c0, c0 + FFN_CHUNK))
        val = conv(slice(D_FF + c0, D_FF + c0 + FFN_CHUNK))
        act_ref[:, c0:c0 + FFN_CHUNK] = (gate * jax.nn.sigmoid(gate) * val).astype(BF16)

    up_ref[0:CONV_HALO, :] = up_ref[tm:tm + CONV_HALO, :]
    o_ref[...] = x + jnp.dot(act_ref[...], wdown_ref[...], preferred_element_type=F32)


def _ffn(x, w, batch, seq):
    m = x.shape[0]
    tm = TM_FFN
    nt = seq // tm
    row = lambda b, t: (b * nt + t, 0)
    consts = [w["ffn_g"], w["wup"], w["conv"], w["wdown"]]
    return pl.pallas_call(
        _ffn_kernel,
        grid=(batch, nt),
        in_specs=[pl.BlockSpec((tm, D_MODEL), row)] + [_const_spec(c) for c in consts],
        out_specs=pl.BlockSpec((tm, D_MODEL), row),
        out_shape=jax.ShapeDtypeStruct((m, D_MODEL), F32),
        scratch_shapes=[pltpu.VMEM((tm + CONV_HALO, 2 * D_FF), F32),
                        pltpu.VMEM((tm, D_FF), BF16)],
        compiler_params=_params(("arbitrary", "arbitrary")),
        name="ffn",
    )(x, *[c[0] for c in consts])


def _block_ones(n, width):
    idx = np.arange(n) // width
    return jnp.asarray((idx[:, None] == idx[None, :]).astype(np.float32), dtype=BF16)


def _pack_weights(attn_norm_g, w_in, q_lat_norm_g, w_uq, kv_lat_norm_g, w_ukv,
                  mla_q_norm_g, mla_k_norm_g, fox_q_norm_g, fox_k_norm_g, fox_f_bias,
                  s5_lambda_re, s5_lambda_im, s5_b_re, s5_b_im, s5_c_re, s5_c_im, s5_d,
                  s5_log_step, s5_w_glu, s5_b_glu, w_branch, w_out, ffn_norm_g, w_up,
                  ffn_conv_w, w_down):
    depth = w_in.shape[0]
    f32 = lambda a: a.astype(F32)
    o = np.cumsum((0, MLA_Q_RANK, MLA_KV_RANK, MLA_ROPE, BRANCH_W, BRANCH_W, BRANCH_W,
                   N_HEADS, S5_W))
    col = lambda i: w_in[:, :, o[i]:o[i + 1]]
    misc = jnp.zeros((depth, D_MODEL, LANE), w_in.dtype)
    misc = misc.at[:, :, 0:N_HEADS].set(col(6))
    misc = misc.at[:, :, KR_LANE:KR_LANE + MLA_ROPE].set(col(2))
    wa = jnp.concatenate([col(0), col(1), col(3), col(4), col(5), col(7), misc], axis=-1)
    wg = w_in[:, :, o[8]:]

    pad_h = lambda a: jnp.pad(a, [(0, 0)] * (a.ndim - 1) + [(0, HEAD_PAD - a.shape[-1])])
    half = MLA_ROPE // 2
    wq_g = f32(w_uq) * f32(mla_q_norm_g)[:, None, None, :] * (MLA_QK ** -0.5 * LOG2E)
    wq_partner = jnp.concatenate(
        [jnp.zeros_like(wq_g[..., :MLA_NOPE]), wq_g[..., MLA_NOPE + half:],
         wq_g[..., MLA_NOPE:MLA_NOPE + half]], axis=-1)
    wuq = jnp.concatenate([pad_h(f32(w_uq)).reshape(depth, MLA_Q_RANK, QK_W),
                           pad_h(wq_partner).reshape(depth, MLA_Q_RANK, QK_W)], axis=-1)
    wukv = jnp.concatenate(
        [pad_h(w_ukv[..., :MLA_NOPE]).reshape(depth, MLA_KV_RANK, QK_W),
         w_ukv[..., MLA_NOPE:].reshape(depth, MLA_KV_RANK, BRANCH_W)], axis=-1)
    tile_h = lambda g: jnp.tile(g, (1, N_HEADS)).reshape(depth, 1, -1)
    gq = tile_h(pad_h(f32(mla_q_norm_g))) * (MLA_QK ** -0.5 * LOG2E)
    gk = tile_h(pad_h(f32(mla_k_norm_g)))
    gfq = tile_h(f32(fox_q_norm_g)) * (FOX_HD ** -0.5 * LOG2E)
    gfk = tile_h(f32(fox_k_norm_g))
    fbias = jnp.pad(f32(fox_f_bias), ((0, 0), (0, LANE - N_HEADS))).reshape(depth, 1, LANE)

    lam_re, lam_im = f32(s5_lambda_re), f32(s5_lambda_im)
    step = jnp.exp(f32(s5_log_step))[..., None]
    mag = jnp.exp(lam_re * step)
    a_re, a_im = mag * jnp.cos(lam_im * step), mag * jnp.sin(lam_im * step)
    den = lam_re * lam_re + lam_im * lam_im
    k_re = ((a_re - 1.0) * lam_re + a_im * lam_im) / den
    k_im = (a_im * lam_re - (a_re - 1.0) * lam_im) / den
    b_re, b_im = f32(s5_b_re), f32(s5_b_im)
    bb_re = k_re[..., None] * b_re - k_im[..., None] * b_im
    bb_im = k_re[..., None] * b_im + k_im[..., None] * b_re
    eye = jnp.eye(S5_G, dtype=F32)
    bd_in = lambda a: jnp.einsum("lgph,gk->lghkp", a, eye).reshape(depth, S5_W, S5_N)
    bmat = jnp.concatenate([bd_in(bb_re), bd_in(bb_im)], axis=-1)
    bd_out = lambda a: jnp.einsum("lghp,gk->lgpkh", a, eye).reshape(depth, S5_N, S5_W)
    cmat = jnp.concatenate([bd_out(f32(s5_c_re)), -bd_out(f32(s5_c_im))], axis=1)
    lam = jnp.stack([a_re.reshape(depth, S5_N), a_im.reshape(depth, S5_N)], axis=1)

    bf = lambda a: a.astype(BF16)
    row = lambda a: f32(a).reshape(depth, 1, -1)
    tri = jnp.asarray(np.tril(np.ones((CUMSUM_BLOCK, CUMSUM_BLOCK), np.float32)), dtype=BF16)
    shared = {"ones_qk": _block_ones(2 * LANE, HEAD_PAD), "ones_fox": _block_ones(BRANCH_W, FOX_HD),
              "tri": tri}
    stacked = {
        "attn_g": row(attn_norm_g), "wa": bf(wa), "wg": bf(wg), "fbias": fbias,
        "qlat_g": row(q_lat_norm_g), "wuq": bf(wuq), "kvlat_g": row(kv_lat_norm_g),
        "wukv": bf(wukv), "gq": gq, "gk": gk, "gfq": gfq, "gfk": gfk,
        "bmat": bf(bmat), "lam": lam, "cmat": bf(cmat), "s5_d": row(s5_d),
        "wglu": bf(s5_w_glu), "bglu": row(s5_b_glu),
        "wbr": bf(w_branch), "wout": bf(w_out),
        "ffn_g": row(ffn_norm_g), "wup": bf(w_up), "conv": f32(ffn_conv_w), "wdown": bf(w_down),
    }
    return [dict({k: (v, None) for k, v in shared.items()},
                 **{k: (v, l) for k, v in stacked.items()}) for l in range(depth)]


def kernel(x, positions, attn_norm_g, w_in, q_lat_norm_g, w_uq, kv_lat_norm_g, w_ukv, mla_q_norm_g, mla_k_norm_g, fox_q_norm_g, fox_k_norm_g, fox_f_bias, s5_lambda_re, s5_lambda_im, s5_b_re, s5_b_im, s5_c_re, s5_c_im, s5_d, s5_log_step, s5_w_glu, s5_b_glu, w_branch, w_out, ffn_norm_g, w_up, ffn_conv_w, w_down):
    batch, seq, d_model = x.shape
    assert d_model == D_MODEL and seq % TM_PROJ == 0 and seq % TK == 0 and seq % TT_S5 == 0
    assert batch % 8 == 0
    layers = _pack_weights(attn_norm_g, w_in, q_lat_norm_g, w_uq, kv_lat_norm_g, w_ukv,
                           mla_q_norm_g, mla_k_norm_g, fox_q_norm_g, fox_k_norm_g, fox_f_bias,
                           s5_lambda_re, s5_lambda_im, s5_b_re, s5_b_im, s5_c_re, s5_c_im, s5_d,
                           s5_log_step, s5_w_glu, s5_b_glu, w_branch, w_out, ffn_norm_g, w_up,
                           ffn_conv_w, w_down)
    cos, sin = _rope_tables(positions)
    xf = x.astype(F32).reshape(batch * seq, D_MODEL)
    for w in layers:
        qm, km, vm, qf, kf, vf, u = _proj(xf, cos, sin, w, batch, seq)
        o_mla = _attention(qm, km, vm, batch, seq)
        o_fox = _attention(qf, kf, vf, batch, seq)
        o_s5 = _s5(u, w, batch, seq)
        xf = _merge(xf, o_mla, o_fox, o_s5, w, batch, seq)
        xf = _ffn(xf, w, batch, seq)
    return xf.reshape(batch, seq, D_MODEL).astype(x.dtype)
```

```python
import functools
import math

import numpy as np
import jax
import jax.numpy as jnp
from jax import lax
from jax.experimental import pallas as pl
from jax.experimental.pallas import tpu as pltpu

F32 = jnp.float32
BF16 = jnp.bfloat16

D_MODEL = 1024
N_HEADS = 4
MLA_NOPE = 64
MLA_ROPE = 32
MLA_QK = MLA_NOPE + MLA_ROPE
MLA_V = 64
MLA_Q_RANK = 384
MLA_KV_RANK = 256
FOX_HD = 64
S5_G = 16
S5_H = 16
S5_P = 64
S5_W = S5_G * S5_H
S5_N = S5_G * S5_P
BRANCH_W = 256
N_BRANCH = 3
D_FF = 2816
CONV_W = 3
ROPE_THETA = 10000.0
EPS = 1e-6
NEG = -1e30
LOG2E = math.log2(math.e)

LANE = 128
HEAD_PAD = 128
QK_W = N_HEADS * HEAD_PAD
C_CQ, C_CKV, C_FQ, C_FK, C_FV, C_U, C_MISC = 0, 384, 640, 896, 1152, 1408, 1664
WA_COLS = 1792
KR_LANE = MLA_NOPE
AUG_W = 6

VMEM_LIMIT = 56 * 1024 * 1024

TM_PROJ = 512
CUMSUM_BLOCK = 256
TK = 512
TT_S5 = 64
TM_MERGE = 512
TM_FFN = 512
CONV_HALO = 8


def _const_spec(param):
    arr, layer = param
    if layer is None:
        return pl.BlockSpec(arr.shape, lambda *_: (0,) * arr.ndim, pipeline_mode=pl.Buffered(1))
    zeros = (0,) * (arr.ndim - 1)
    return pl.BlockSpec((None,) + arr.shape[1:], lambda *_: (layer,) + zeros,
                        pipeline_mode=pl.Buffered(1))


def _params(sem):
    return pltpu.CompilerParams(dimension_semantics=sem, vmem_limit_bytes=VMEM_LIMIT)


def _rms(x, gain):
    ms = jnp.mean(x * x, axis=-1, keepdims=True)
    return x * lax.rsqrt(ms + EPS) * gain


def _split3(x):
    hi = x.astype(BF16)
    r = x - hi.astype(F32)
    mid = r.astype(BF16)
    lo = (r - mid.astype(F32)).astype(BF16)
    return hi, mid, lo


def _head_sums(sq, ones_ref):
    sq = sq.astype(BF16)
    ones = ones_ref[...]
    outs = [jnp.dot(sq[:, c0:c0 + 2 * LANE], ones, preferred_element_type=F32)
            for c0 in range(0, sq.shape[1], 2 * LANE)]
    return outs[0] if len(outs) == 1 else jnp.concatenate(outs, axis=1)


def _aug_lane(h):
    return MLA_QK + AUG_W * h if h % 2 == 0 else AUG_W * (h + 1)


def _den_lane(h):
    return MLA_V if h % 2 == 0 else 0


def _pad_value_heads(v):
    lane = lax.broadcasted_iota(jnp.int32, (v.shape[0], LANE), 1)
    tiles = []
    for h in range(N_HEADS):
        pair = v[:, (h // 2) * LANE:(h // 2 + 1) * LANE]
        mine = lane < MLA_V if h % 2 == 0 else lane >= MLA_V
        tiles.append(jnp.where(mine, pair, jnp.where(lane == _den_lane(h), 1.0, 0.0)))
    return jnp.concatenate(tiles, axis=1)


def _rope_table_kernel(pos_ref, inv_ref, sign_ref, cos_ref, sin_ref):
    ang = pos_ref[...] * inv_ref[...]
    cos_ref[...] = jnp.cos(ang)
    sin_ref[...] = jnp.sin(ang) * sign_ref[...]


def _rope_tables(positions):
    m = positions.size
    tm = 1024
    pos = positions.astype(F32).reshape(m, 1)
    inv_freq = ROPE_THETA ** (-jnp.arange(0, MLA_ROPE, 2, dtype=F32) / MLA_ROPE)
    half = MLA_ROPE // 2
    inv_lane = jnp.zeros((LANE,), F32)
    inv_lane = inv_lane.at[MLA_NOPE:MLA_NOPE + half].set(inv_freq)
    inv_lane = inv_lane.at[MLA_NOPE + half:MLA_QK].set(inv_freq)
    sign = np.zeros((LANE,), np.float32)
    sign[MLA_NOPE:MLA_NOPE + half] = -1.0
    sign[MLA_NOPE + half:MLA_QK] = 1.0
    return pl.pallas_call(
        _rope_table_kernel,
        grid=(m // tm,),
        in_specs=[pl.BlockSpec((tm, 1), lambda i: (i, 0)),
                  pl.BlockSpec((1, LANE), lambda i: (0, 0)),
                  pl.BlockSpec((1, LANE), lambda i: (0, 0))],
        out_specs=[pl.BlockSpec((tm, LANE), lambda i: (i, 0))] * 2,
        out_shape=[jax.ShapeDtypeStruct((m, LANE), F32)] * 2,
        compiler_params=_params(("arbitrary",)),
        name="rope_tables",
    )(pos, inv_lane.reshape(1, LANE), jnp.asarray(sign).reshape(1, LANE))


def _proj_kernel(x_ref, cos_ref, sin_ref, g_ref, wa_ref, fbias_ref, augm_ref,
                 qlat_g_ref, wuq_ref, kvlat_g_ref, wukv_ref, gq_ref, gk_ref, ones_qk_ref,
                 gfq_ref, gfk_ref, ones_fox_ref, tri_ref,
                 qm_ref, km_ref, vm_ref, qf_ref, kf_ref, vf_ref, u_ref,
                 carry_ref):
    t = pl.program_id(1)
    tm = x_ref.shape[0]

    h = _rms(x_ref[...], g_ref[...]).astype(BF16)
    p = jnp.dot(h, wa_ref[...], preferred_element_type=F32)
    misc = p[:, C_MISC:C_MISC + LANE]
    lane = lax.broadcasted_iota(jnp.int32, (tm, LANE), 1)
    cos = cos_ref[...]
    sin = sin_ref[...]

    cq = _rms(p[:, C_CQ:C_CQ + MLA_Q_RANK], qlat_g_ref[...]).astype(BF16)
    qq = jnp.dot(cq, wuq_ref[...], preferred_element_type=F32)
    q = qq[:, :QK_W]
    cos4 = jnp.concatenate([cos] * N_HEADS, axis=1)
    sin4 = jnp.concatenate([sin] * N_HEADS, axis=1)
    r = lax.rsqrt(_head_sums(q * q, ones_qk_ref) * (1.0 / MLA_QK) + EPS)
    qm_ref[...] = (r * (q * gq_ref[...] * cos4 + qq[:, QK_W:] * sin4)).astype(BF16)

    ckv = _rms(p[:, C_CKV:C_CKV + MLA_KV_RANK], kvlat_g_ref[...]).astype(BF16)
    kv = jnp.dot(ckv, wukv_ref[...], preferred_element_type=F32)
    k_rope = jnp.where((lane >= KR_LANE) & (lane < MLA_QK), misc, 0.0)
    k = kv[:, :QK_W] + jnp.concatenate([k_rope] * N_HEADS, axis=1)
    r = lax.rsqrt(_head_sums(k * k, ones_qk_ref) * (1.0 / MLA_QK) + EPS)
    krg = k_rope * gk_ref[:, :LANE]
    first_half = lane < (MLA_NOPE + MLA_ROPE // 2)
    partner = jnp.where(first_half, pltpu.roll(krg, HEAD_PAD - MLA_ROPE // 2, 1),
                        pltpu.roll(krg, MLA_ROPE // 2, 1)) * sin
    km_ref[...] = (r * (k * gk_ref[...] * cos4
                        + jnp.concatenate([partner] * N_HEADS, axis=1))).astype(BF16)
    vm_ref[...] = _pad_value_heads(kv[:, QK_W:]).astype(BF16)

    @pl.when(t == 0)
    def _():
        carry_ref[...] = jnp.zeros_like(carry_ref)

    z = misc + fbias_ref[...]
    log_f = jnp.minimum(z, 0.0) - jnp.log1p(jnp.exp(-jnp.abs(z)))
    tri = tri_ref[...]
    parts = _split3(log_f)
    carry = carry_ref[0:1, :]
    c_blocks = []
    for r0 in range(0, tm, CUMSUM_BLOCK):
        cb = carry
        for part in parts:
            cb = cb + jnp.dot(tri, part[r0:r0 + CUMSUM_BLOCK], preferred_element_type=F32)
        carry = cb[CUMSUM_BLOCK - 1:CUMSUM_BLOCK, :]
        c_blocks.append(cb)
    carry_ref[0:1, :] = carry
    c = jnp.concatenate(c_blocks, axis=0) * LOG2E

    fq = p[:, C_FQ:C_FQ + BRANCH_W]
    fq = fq * lax.rsqrt(_head_sums(fq * fq, ones_fox_ref) * (1.0 / FOX_HD) + EPS) * gfq_ref[...]
    fk = p[:, C_FK:C_FK + BRANCH_W]
    fk = fk * lax.rsqrt(_head_sums(fk * fk, ones_fox_ref) * (1.0 / FOX_HD) + EPS) * gfk_ref[...]
    hi = c.astype(BF16).astype(F32)
    rem = c - hi
    mid = rem.astype(BF16).astype(F32)
    lo = rem - mid
    aug_q = hi * augm_ref[0:1, :] + mid * augm_ref[1:2, :] + lo * augm_ref[2:3, :] + augm_ref[3:4, :]
    aug_k = augm_ref[4:5, :] - hi * augm_ref[5:6, :] - mid * augm_ref[6:7, :] - lo * augm_ref[7:8, :]
    q_tiles, k_tiles = [], []
    for hd in range(N_HEADS):
        pair = slice((hd // 2) * LANE, (hd // 2 + 1) * LANE)
        data = lane < FOX_HD if hd % 2 == 0 else lane >= FOX_HD
        in_aug = (lane >= _aug_lane(hd)) & (lane < _aug_lane(hd) + AUG_W)
        q_tiles.append(jnp.where(data, fq[:, pair], jnp.where(in_aug, aug_q, 0.0)))
        k_tiles.append(jnp.where(data, fk[:, pair], jnp.where(in_aug, aug_k, 0.0)))
    qf_ref[...] = jnp.concatenate(q_tiles, axis=1).astype(BF16)
    kf_ref[...] = jnp.concatenate(k_tiles, axis=1).astype(BF16)
    vf_ref[...] = _pad_value_heads(p[:, C_FV:C_FV + BRANCH_W]).astype(BF16)

    u_ref[...] = p[:, C_U:C_U + S5_W]


def _proj(x, cos, sin, w, batch, seq):
    m = x.shape[0]
    tm = TM_PROJ
    nt = seq // tm
    row = lambda b, t: (b * nt + t, 0)
    consts = [w["attn_g"], w["wa"], w["fbias"], w["augm"], w["qlat_g"], w["wuq"], w["kvlat_g"], w["wukv"],
              w["gq"], w["gk"], w["ones_qk"], w["gfq"], w["gfk"], w["ones_fox"], w["tri"]]
    out_shape = [jax.ShapeDtypeStruct((m, QK_W), BF16)] * 6 + [
        jax.ShapeDtypeStruct((m, S5_W), F32)]
    out_specs = [pl.BlockSpec((tm, QK_W), row)] * 6 + [pl.BlockSpec((tm, S5_W), row)]
    return pl.pallas_call(
        _proj_kernel,
        grid=(batch, nt),
        in_specs=[pl.BlockSpec((tm, D_MODEL), row), pl.BlockSpec((tm, LANE), row),
                  pl.BlockSpec((tm, LANE), row)] + [_const_spec(c) for c in consts],
        out_specs=out_specs,
        out_shape=out_shape,
        scratch_shapes=[pltpu.VMEM((8, LANE), F32)],
        compiler_params=_params(("arbitrary", "arbitrary")),
        name="proj",
    )(x, cos, sin, *[c[0] for c in consts])


def _attn_kernel(q_ref, k_ref, v_ref, o_ref, acc_ref, m_ref):
    seq = q_ref.shape[0]
    tk = TK
    causal = (lax.broadcasted_iota(jnp.int32, (tk, tk), 0)
              >= lax.broadcasted_iota(jnp.int32, (tk, tk), 1))
    low_half = lax.broadcasted_iota(jnp.int32, (tk, LANE), 1) < MLA_V

    for j in range(seq // tk):
        r0 = j * tk
        for h in range(N_HEADS):
            hs = slice(h * HEAD_PAD, (h + 1) * HEAD_PAD)
            s = lax.dot_general(q_ref[r0:, hs], k_ref[r0:r0 + tk, hs],
                                (((1,), (1,)), ((), ())), preferred_element_type=F32)
            top = jnp.where(causal, s[:tk], NEG)
            s = top if seq - r0 == tk else jnp.concatenate([top, s[tk:]], axis=0)
            m_cur = jnp.max(s, axis=1, keepdims=True)
            if j == 0:
                m_new = jnp.broadcast_to(m_cur, (seq, LANE))
            else:
                m_prev = m_ref[h, r0:, :]
                m_new = jnp.maximum(m_prev, m_cur)
                alpha = jnp.exp2(m_prev - m_new)
            p = jnp.exp2(s - jnp.concatenate([m_new] * (tk // LANE), axis=1))
            pv = jnp.dot(p.astype(BF16), v_ref[r0:r0 + tk, hs], preferred_element_type=F32)
            if j == 0:
                acc_ref[h] = pv
            else:
                acc_ref[h, r0:, :] = acc_ref[h, r0:, :] * alpha + pv
            if seq - r0 > tk:
                m_ref[h, r0:, :] = m_new
        outs = []
        for h in range(N_HEADS):
            a = acc_ref[h, r0:r0 + tk, :]
            outs.append(a * (1.0 / a[:, _den_lane(h):_den_lane(h) + 1]))
        o_ref[r0:r0 + tk, :] = jnp.concatenate(
            [jnp.where(low_half, outs[0], outs[1]), jnp.where(low_half, outs[2], outs[3])],
            axis=1).astype(o_ref.dtype)


def _attention(q, k, v, batch, seq):
    m = q.shape[0]
    spec = pl.BlockSpec((seq, QK_W), lambda b: (b, 0))
    return pl.pallas_call(
        _attn_kernel,
        grid=(batch,),
        in_specs=[spec, spec, spec],
        out_specs=pl.BlockSpec((seq, BRANCH_W), lambda b: (b, 0)),
        out_shape=jax.ShapeDtypeStruct((m, BRANCH_W), BF16),
        scratch_shapes=[pltpu.VMEM((N_HEADS, seq, LANE), F32),
                        pltpu.VMEM((N_HEADS, seq, LANE), F32)],
        compiler_params=_params(("arbitrary",)),
        name="attn",
    )(q, k, v)


def _s5_kernel(u_ref, bmat_ref, lam_ref, cmat_ref, d_ref, wglu_ref, bglu_ref, o_ref, st_ref):
    batch, tt, _ = u_ref.shape

    @pl.when(pl.program_id(0) == 0)
    def _():
        st_ref[...] = jnp.zeros_like(st_ref)

    u = jnp.swapaxes(u_ref[...], 0, 1).reshape(tt * batch, S5_W)
    bu = jnp.dot(u.astype(BF16), bmat_ref[...], preferred_element_type=F32)
    lam_re = lam_ref[0:1, :]
    lam_im = lam_ref[1:2, :]
    x_re = st_ref[:, :S5_N]
    x_im = st_ref[:, S5_N:]
    states = []
    for t in range(tt):
        r = slice(t * batch, (t + 1) * batch)
        n_re = lam_re * x_re - lam_im * x_im + bu[r, :S5_N]
        n_im = lam_re * x_im + lam_im * x_re + bu[r, S5_N:]
        x_re, x_im = n_re, n_im
        states.append(jnp.concatenate([n_re, n_im], axis=1).astype(BF16))
    st_ref[:, :S5_N] = x_re
    st_ref[:, S5_N:] = x_im

    y = jnp.dot(jnp.concatenate(states, axis=0), cmat_ref[...], preferred_element_type=F32)
    y = y + d_ref[...] * u
    y = 0.5 * y * (1.0 + jnp.tanh(math.sqrt(2.0 / math.pi) * (y + 0.044715 * (y * y * y))))
    z = jnp.dot(y.astype(BF16), wglu_ref[...], preferred_element_type=F32) + bglu_ref[...]
    o = (y * jax.nn.sigmoid(z)).reshape(tt, batch, S5_W)
    o_ref[...] = jnp.swapaxes(o, 0, 1).astype(o_ref.dtype)


def _s5(u, w, batch, seq):
    consts = [w["bmat"], w["lam"], w["cmat"], w["s5_d"], w["wglu"], w["bglu"]]
    blk = pl.BlockSpec((batch, TT_S5, S5_W), lambda t: (0, t, 0))
    return pl.pallas_call(
        _s5_kernel,
        grid=(seq // TT_S5,),
        in_specs=[blk] + [_const_spec(c) for c in consts],
        out_specs=blk,
        out_shape=jax.ShapeDtypeStruct((batch, seq, S5_W), BF16),
        scratch_shapes=[pltpu.VMEM((batch, 2 * S5_N), F32)],
        compiler_params=_params(("arbitrary",)),
        name="s5",
    )(u.reshape(batch, seq, S5_W), *[c[0] for c in consts]).reshape(batch * seq, S5_W)


def _merge_kernel(x_ref, om_ref, of_ref, os_ref, g_ref, wg_ref, wbr_ref, wout_ref, o_ref):
    x = x_ref[...]
    h = _rms(x, g_ref[...]).astype(BF16)
    merged = None
    for n, br_ref in enumerate((om_ref, of_ref, os_ref)):
        logits = jnp.dot(h, wg_ref[:, n * D_MODEL:(n + 1) * D_MODEL], preferred_element_type=F32)
        proj = jnp.dot(br_ref[...], wbr_ref[n], preferred_element_type=F32)
        term = jax.nn.sigmoid(logits) * proj
        merged = term if merged is None else merged + term
    o_ref[...] = x + jnp.dot(merged.astype(BF16), wout_ref[...], preferred_element_type=F32)


def _merge(x, o_mla, o_fox, o_s5, w, batch, seq):
    m = x.shape[0]
    tm = TM_MERGE
    nt = seq // tm
    row = lambda b, t: (b * nt + t, 0)
    consts = [w["attn_g"], w["wg"], w["wbr"], w["wout"]]
    return pl.pallas_call(
        _merge_kernel,
        grid=(batch, nt),
        in_specs=[pl.BlockSpec((tm, D_MODEL), row)] + [pl.BlockSpec((tm, BRANCH_W), row)] * 3
                 + [_const_spec(c) for c in consts],
        out_specs=pl.BlockSpec((tm, D_MODEL), row),
        out_shape=jax.ShapeDtypeStruct((m, D_MODEL), F32),
        compiler_params=_params(("arbitrary", "arbitrary")),
        name="merge",
    )(x, o_mla, o_fox, o_s5, *[c[0] for c in consts])


FFN_CHUNK = 256


def _ffn_kernel(x_ref, g_ref, wup_ref, conv_ref, wdown_ref, o_ref, up_ref, act_ref):
    t = pl.program_id(1)
    tm = x_ref.shape[0]

    @pl.when(t == 0)
    def _():
        up_ref[0:CONV_HALO, :] = jnp.zeros((CONV_HALO, 2 * D_FF), F32)

    x = x_ref[...]
    h = _rms(x, g_ref[...]).astype(BF16)
    up_ref[CONV_HALO:CONV_HALO + tm, :] = jnp.dot(h, wup_ref[...], preferred_element_type=F32)

    def conv(cols):
        out = None
        for j in range(CONV_W):
            lo = CONV_HALO - (CONV_W - 1) + j
            term = conv_ref[j:j + 1, cols] * up_ref[lo:lo + tm, cols]
            out = term if out is None else out + term
        return out

    for c0 in range(0, D_FF, FFN_CHUNK):
        gate = conv(slice(c0, c0 + FFN_CHUNK))
        val = conv(slice(D_FF + c0, D_FF + c0 + FFN_CHUNK))
        act_ref[:, c0:c0 + FFN_CHUNK] = (gate * jax.nn.sigmoid(gate) * val).astype(BF16)

    up_ref[0:CONV_HALO, :] = up_ref[tm:tm + CONV_HALO, :]
    o_ref[...] = x + jnp.dot(act_ref[...], wdown_ref[...], preferred_element_type=F32)


def _ffn(x, w, batch, seq):
    m = x.shape[0]
    tm = TM_FFN
    nt = seq // tm
    row = lambda b, t: (b * nt + t, 0)
    consts = [w["ffn_g"], w["wup"], w["conv"], w["wdown"]]
    return pl.pallas_call(
        _ffn_kernel,
        grid=(batch, nt),
        in_specs=[pl.BlockSpec((tm, D_MODEL), row)] + [_const_spec(c) for c in consts],
        out_specs=pl.BlockSpec((tm, D_MODEL), row),
        out_shape=jax.ShapeDtypeStruct((m, D_MODEL), F32),
        scratch_shapes=[pltpu.VMEM((tm + CONV_HALO, 2 * D_FF), F32),
                        pltpu.VMEM((tm, D_FF), BF16)],
        compiler_params=_params(("arbitrary", "arbitrary")),
        name="ffn",
    )(x, *[c[0] for c in consts])


def _block_ones(n, width):
    idx = np.arange(n) // width
    return jnp.asarray((idx[:, None] == idx[None, :]).astype(np.float32), dtype=BF16)


def _pack_weights(attn_norm_g, w_in, q_lat_norm_g, w_uq, kv_lat_norm_g, w_ukv,
                  mla_q_norm_g, mla_k_norm_g, fox_q_norm_g, fox_k_norm_g, fox_f_bias,
                  s5_lambda_re, s5_lambda_im, s5_b_re, s5_b_im, s5_c_re, s5_c_im, s5_d,
                  s5_log_step, s5_w_glu, s5_b_glu, w_branch, w_out, ffn_norm_g, w_up,
                  ffn_conv_w, w_down):
    depth = w_in.shape[0]
    f32 = lambda a: a.astype(F32)
    o = np.cumsum((0, MLA_Q_RANK, MLA_KV_RANK, MLA_ROPE, BRANCH_W, BRANCH_W, BRANCH_W,
                   N_HEADS, S5_W))
    col = lambda i: w_in[:, :, o[i]:o[i + 1]]
    misc = jnp.zeros((depth, D_MODEL, LANE), w_in.dtype)
    for h in range(N_HEADS):
        misc = misc.at[:, :, _aug_lane(h):_aug_lane(h) + AUG_W].set(
            jnp.repeat(col(6)[:, :, h:h + 1], AUG_W, axis=2))
    misc = misc.at[:, :, KR_LANE:KR_LANE + MLA_ROPE].set(col(2))
    wa = jnp.concatenate([col(0), col(1), col(3), col(4), col(5), col(7), misc], axis=-1)
    wg = w_in[:, :, o[8]:]

    pad_h = lambda a: jnp.pad(a, [(0, 0)] * (a.ndim - 1) + [(0, HEAD_PAD - a.shape[-1])])
    half = MLA_ROPE // 2
    wq_g = f32(w_uq) * f32(mla_q_norm_g)[:, None, None, :] * (MLA_QK ** -0.5 * LOG2E)
    wq_partner = jnp.concatenate(
        [jnp.zeros_like(wq_g[..., :MLA_NOPE]), wq_g[..., MLA_NOPE + half:],
         wq_g[..., MLA_NOPE:MLA_NOPE + half]], axis=-1)
    wuq = jnp.concatenate([pad_h(f32(w_uq)).reshape(depth, MLA_Q_RANK, QK_W),
                           pad_h(wq_partner).reshape(depth, MLA_Q_RANK, QK_W)], axis=-1)
    wukv = jnp.concatenate(
        [pad_h(w_ukv[..., :MLA_NOPE]).reshape(depth, MLA_KV_RANK, QK_W),
         w_ukv[..., MLA_NOPE:].reshape(depth, MLA_KV_RANK, BRANCH_W)], axis=-1)
    tile_h = lambda g: jnp.tile(g, (1, N_HEADS)).reshape(depth, 1, -1)
    gq = tile_h(pad_h(f32(mla_q_norm_g))) * (MLA_QK ** -0.5 * LOG2E)
    gk = tile_h(pad_h(f32(mla_k_norm_g)))
    gfq = tile_h(f32(fox_q_norm_g)) * (FOX_HD ** -0.5 * LOG2E)
    gfk = tile_h(f32(fox_k_norm_g))
    fbias = jnp.zeros((depth, LANE), F32)
    augm = np.zeros((8, LANE), np.float32)
    for h in range(N_HEADS):
        a = _aug_lane(h)
        fbias = fbias.at[:, a:a + AUG_W].set(f32(fox_f_bias)[:, h:h + 1])
        for j in range(3):
            augm[j, a + j] = 1.0
            augm[3, a + 3 + j] = 1.0
            augm[4, a + j] = 1.0
            augm[5 + j, a + 3 + j] = 1.0
    fbias = fbias.reshape(depth, 1, LANE)

    lam_re, lam_im = f32(s5_lambda_re), f32(s5_lambda_im)
    step = jnp.exp(f32(s5_log_step))[..., None]
    mag = jnp.exp(lam_re * step)
    a_re, a_im = mag * jnp.cos(lam_im * step), mag * jnp.sin(lam_im * step)
    den = lam_re * lam_re + lam_im * lam_im
    k_re = ((a_re - 1.0) * lam_re + a_im * lam_im) / den
    k_im = (a_im * lam_re - (a_re - 1.0) * lam_im) / den
    b_re, b_im = f32(s5_b_re), f32(s5_b_im)
    bb_re = k_re[..., None] * b_re - k_im[..., None] * b_im
    bb_im = k_re[..., None] * b_im + k_im[..., None] * b_re
    eye = jnp.eye(S5_G, dtype=F32)
    bd_in = lambda a: jnp.einsum("lgph,gk->lghkp", a, eye).reshape(depth, S5_W, S5_N)
    bmat = jnp.concatenate([bd_in(bb_re), bd_in(bb_im)], axis=-1)
    bd_out = lambda a: jnp.einsum("lghp,gk->lgpkh", a, eye).reshape(depth, S5_N, S5_W)
    cmat = jnp.concatenate([bd_out(f32(s5_c_re)), -bd_out(f32(s5_c_im))], axis=1)
    lam = jnp.stack([a_re.reshape(depth, S5_N), a_im.reshape(depth, S5_N)], axis=1)

    bf = lambda a: a.astype(BF16)
    row = lambda a: f32(a).reshape(depth, 1, -1)
    tri = jnp.asarray(np.tril(np.ones((CUMSUM_BLOCK, CUMSUM_BLOCK), np.float32)), dtype=BF16)
    shared = {"ones_qk": _block_ones(2 * LANE, HEAD_PAD), "ones_fox": _block_ones(BRANCH_W, FOX_HD),
              "tri": tri, "augm": jnp.asarray(augm)}
    stacked = {
        "attn_g": row(attn_norm_g), "wa": bf(wa), "wg": bf(wg), "fbias": fbias,
        "qlat_g": row(q_lat_norm_g), "wuq": bf(wuq), "kvlat_g": row(kv_lat_norm_g),
        "wukv": bf(wukv), "gq": gq, "gk": gk, "gfq": gfq, "gfk": gfk,
        "bmat": bf(bmat), "lam": lam, "cmat": bf(cmat), "s5_d": row(s5_d),
        "wglu": bf(s5_w_glu), "bglu": row(s5_b_glu),
        "wbr": bf(w_branch), "wout": bf(w_out),
        "ffn_g": row(ffn_norm_g), "wup": bf(w_up), "conv": f32(ffn_conv_w), "wdown": bf(w_down),
    }
    return [dict({k: (v, None) for k, v in shared.items()},
                 **{k: (v, l) for k, v in stacked.items()}) for l in range(depth)]


def kernel(x, positions, attn_norm_g, w_in, q_lat_norm_g, w_uq, kv_lat_norm_g, w_ukv, mla_q_norm_g, mla_k_norm_g, fox_q_norm_g, fox_k_norm_g, fox_f_bias, s5_lambda_re, s5_lambda_im, s5_b_re, s5_b_im, s5_c_re, s5_c_im, s5_d, s5_log_step, s5_w_glu, s5_b_glu, w_branch, w_out, ffn_norm_g, w_up, ffn_conv_w, w_down):
    batch, seq, d_model = x.shape
    assert d_model == D_MODEL and seq % TM_PROJ == 0 and seq % TK == 0 and seq % TT_S5 == 0
    assert batch % 8 == 0
    layers = _pack_weights(attn_norm_g, w_in, q_lat_norm_g, w_uq, kv_lat_norm_g, w_ukv,
                           mla_q_norm_g, mla_k_norm_g, fox_q_norm_g, fox_k_norm_g, fox_f_bias,
                           s5_lambda_re, s5_lambda_im, s5_b_re, s5_b_im, s5_c_re, s5_c_im, s5_d,
                           s5_log_step, s5_w_glu, s5_b_glu, w_branch, w_out, ffn_norm_g, w_up,
                           ffn_conv_w, w_down)
    cos, sin = _rope_tables(positions)
    xf = x.astype(F32).reshape(batch * seq, D_MODEL)
    for w in layers:
        qm, km, vm, qf, kf, vf, u = _proj(xf, cos, sin, w, batch, seq)
        o_mla = _attention(qm, km, vm, batch, seq)
        o_fox = _attention(qf, kf, vf, batch, seq)
        o_s5 = _s5(u, w, batch, seq)
        xf = _merge(xf, o_mla, o_fox, o_s5, w, batch, seq)
        xf = _ffn(xf, w, batch, seq)
    return xf.reshape(batch, seq, D_MODEL).astype(x.dtype)
```

```python
import functools
import math

import numpy as np
import jax
import jax.numpy as jnp
from jax import lax
from jax.experimental import pallas as pl
from jax.experimental.pallas import tpu as pltpu

F32 = jnp.float32
BF16 = jnp.bfloat16

D_MODEL = 1024
N_HEADS = 4
MLA_NOPE = 64
MLA_ROPE = 32
MLA_QK = MLA_NOPE + MLA_ROPE
MLA_V = 64
MLA_Q_RANK = 384
MLA_KV_RANK = 256
FOX_HD = 64
S5_G = 16
S5_H = 16
S5_P = 64
S5_W = S5_G * S5_H
S5_N = S5_G * S5_P
BRANCH_W = 256
N_BRANCH = 3
D_FF = 2816
CONV_W = 3
ROPE_THETA = 10000.0
EPS = 1e-6
NEG = -1e30
LOG2E = math.log2(math.e)

LANE = 128
HEAD_PAD = 128
QK_W = N_HEADS * HEAD_PAD
C_CQ, C_CKV, C_FQ, C_FK, C_FV, C_U, C_MISC = 0, 384, 640, 896, 1152, 1408, 1664
WA_COLS = 1792
KR_LANE = MLA_NOPE
AUG_W = 6

VMEM_LIMIT = 56 * 1024 * 1024

TM_PROJ = 1024
PROJ_SUB = 512
CUMSUM_BLOCK = 256
TK = 512
TT_S5 = 64
TM_MERGE = 1024
MERGE_SUB = 512
TM_FFN = 512
CONV_HALO = 8


def _const_spec(param):
    arr, layer = param
    if layer is None:
        return pl.BlockSpec(arr.shape, lambda *_: (0,) * arr.ndim, pipeline_mode=pl.Buffered(1))
    zeros = (0,) * (arr.ndim - 1)
    return pl.BlockSpec((None,) + arr.shape[1:], lambda *_: (layer,) + zeros,
                        pipeline_mode=pl.Buffered(1))


def _params(sem):
    return pltpu.CompilerParams(dimension_semantics=sem, vmem_limit_bytes=VMEM_LIMIT)


def _rms(x, gain):
    ms = jnp.mean(x * x, axis=-1, keepdims=True)
    return x * lax.rsqrt(ms + EPS) * gain


def _split3(x):
    hi = x.astype(BF16)
    r = x - hi.astype(F32)
    mid = r.astype(BF16)
    lo = (r - mid.astype(F32)).astype(BF16)
    return hi, mid, lo


def _head_sums(sq, ones_ref):
    sq = sq.astype(BF16)
    ones = ones_ref[...]
    outs = [jnp.dot(sq[:, c0:c0 + 2 * LANE], ones, preferred_element_type=F32)
            for c0 in range(0, sq.shape[1], 2 * LANE)]
    return outs[0] if len(outs) == 1 else jnp.concatenate(outs, axis=1)


def _aug_lane(h):
    return MLA_QK + AUG_W * h if h % 2 == 0 else AUG_W * (h + 1)


def _den_lane(h):
    return MLA_V if h % 2 == 0 else 0


def _pad_value_heads(v):
    lane = lax.broadcasted_iota(jnp.int32, (v.shape[0], LANE), 1)
    tiles = []
    for h in range(N_HEADS):
        pair = v[:, (h // 2) * LANE:(h // 2 + 1) * LANE]
        mine = lane < MLA_V if h % 2 == 0 else lane >= MLA_V
        tiles.append(jnp.where(mine, pair, jnp.where(lane == _den_lane(h), 1.0, 0.0)))
    return jnp.concatenate(tiles, axis=1)


def _rope_table_kernel(pos_ref, inv_ref, sign_ref, cos_ref, sin_ref):
    ang = pos_ref[...] * inv_ref[...]
    cos_ref[...] = jnp.cos(ang)
    sin_ref[...] = jnp.sin(ang) * sign_ref[...]


def _rope_tables(positions):
    m = positions.size
    tm = 1024
    pos = positions.astype(F32).reshape(m, 1)
    inv_freq = ROPE_THETA ** (-jnp.arange(0, MLA_ROPE, 2, dtype=F32) / MLA_ROPE)
    half = MLA_ROPE // 2
    inv_lane = jnp.zeros((LANE,), F32)
    inv_lane = inv_lane.at[MLA_NOPE:MLA_NOPE + half].set(inv_freq)
    inv_lane = inv_lane.at[MLA_NOPE + half:MLA_QK].set(inv_freq)
    sign = np.zeros((LANE,), np.float32)
    sign[MLA_NOPE:MLA_NOPE + half] = -1.0
    sign[MLA_NOPE + half:MLA_QK] = 1.0
    return pl.pallas_call(
        _rope_table_kernel,
        grid=(m // tm,),
        in_specs=[pl.BlockSpec((tm, 1), lambda i: (i, 0)),
                  pl.BlockSpec((1, LANE), lambda i: (0, 0)),
                  pl.BlockSpec((1, LANE), lambda i: (0, 0))],
        out_specs=[pl.BlockSpec((tm, LANE), lambda i: (i, 0))] * 2,
        out_shape=[jax.ShapeDtypeStruct((m, LANE), F32)] * 2,
        compiler_params=_params(("arbitrary",)),
        name="rope_tables",
    )(pos, inv_lane.reshape(1, LANE), jnp.asarray(sign).reshape(1, LANE))


def _proj_kernel(x_ref, *refs):
    carry_ref = refs[-1]

    @pl.when(pl.program_id(1) == 0)
    def _():
        carry_ref[...] = jnp.zeros_like(carry_ref)

    for r0 in range(0, x_ref.shape[0], PROJ_SUB):
        _proj_rows(slice(r0, r0 + PROJ_SUB), x_ref, *refs)


def _proj_rows(rows, x_ref, cos_ref, sin_ref, g_ref, wa_ref, fbias_ref, augm_ref,
               qlat_g_ref, wuq_ref, kvlat_g_ref, wukv_ref, gq_ref, gk_ref, ones_qk_ref,
               gfq_ref, gfk_ref, ones_fox_ref, tri_ref,
               qm_ref, km_ref, vm_ref, qf_ref, kf_ref, vf_ref, u_ref,
               carry_ref):
    tm = PROJ_SUB
    h = _rms(x_ref[rows, :], g_ref[...]).astype(BF16)
    p = jnp.dot(h, wa_ref[...], preferred_element_type=F32)
    misc = p[:, C_MISC:C_MISC + LANE]
    lane = lax.broadcasted_iota(jnp.int32, (tm, LANE), 1)
    cos = cos_ref[rows, :]
    sin = sin_ref[rows, :]

    cq = _rms(p[:, C_CQ:C_CQ + MLA_Q_RANK], qlat_g_ref[...]).astype(BF16)
    qq = jnp.dot(cq, wuq_ref[...], preferred_element_type=F32)
    q = qq[:, :QK_W]
    cos4 = jnp.concatenate([cos] * N_HEADS, axis=1)
    sin4 = jnp.concatenate([sin] * N_HEADS, axis=1)
    r = lax.rsqrt(_head_sums(q * q, ones_qk_ref) * (1.0 / MLA_QK) + EPS)
    qm_ref[rows, :] = (r * (q * gq_ref[...] * cos4 + qq[:, QK_W:] * sin4)).astype(BF16)

    ckv = _rms(p[:, C_CKV:C_CKV + MLA_KV_RANK], kvlat_g_ref[...]).astype(BF16)
    kv = jnp.dot(ckv, wukv_ref[...], preferred_element_type=F32)
    k_rope = jnp.where((lane >= KR_LANE) & (lane < MLA_QK), misc, 0.0)
    k = kv[:, :QK_W] + jnp.concatenate([k_rope] * N_HEADS, axis=1)
    r = lax.rsqrt(_head_sums(k * k, ones_qk_ref) * (1.0 / MLA_QK) + EPS)
    krg = k_rope * gk_ref[:, :LANE]
    first_half = lane < (MLA_NOPE + MLA_ROPE // 2)
    partner = jnp.where(first_half, pltpu.roll(krg, HEAD_PAD - MLA_ROPE // 2, 1),
                        pltpu.roll(krg, MLA_ROPE // 2, 1)) * sin
    km_ref[rows, :] = (r * (k * gk_ref[...] * cos4
                            + jnp.concatenate([partner] * N_HEADS, axis=1))).astype(BF16)
    vm_ref[rows, :] = _pad_value_heads(kv[:, QK_W:]).astype(BF16)

    z = misc + fbias_ref[...]
    log_f = jnp.minimum(z, 0.0) - jnp.log1p(jnp.exp(-jnp.abs(z)))
    tri = tri_ref[...]
    parts = _split3(log_f)
    carry = carry_ref[0:1, :]
    c_blocks = []
    for r0 in range(0, tm, CUMSUM_BLOCK):
        cb = carry
        for part in parts:
            cb = cb + jnp.dot(tri, part[r0:r0 + CUMSUM_BLOCK], preferred_element_type=F32)
        carry = cb[CUMSUM_BLOCK - 1:CUMSUM_BLOCK, :]
        c_blocks.append(cb)
    carry_ref[0:1, :] = carry
    c = jnp.concatenate(c_blocks, axis=0) * LOG2E

    fq = p[:, C_FQ:C_FQ + BRANCH_W]
    fq = fq * lax.rsqrt(_head_sums(fq * fq, ones_fox_ref) * (1.0 / FOX_HD) + EPS) * gfq_ref[...]
    fk = p[:, C_FK:C_FK + BRANCH_W]
    fk = fk * lax.rsqrt(_head_sums(fk * fk, ones_fox_ref) * (1.0 / FOX_HD) + EPS) * gfk_ref[...]
    hi = c.astype(BF16).astype(F32)
    rem = c - hi
    mid = rem.astype(BF16).astype(F32)
    lo = rem - mid
    aug_q = hi * augm_ref[0:1, :] + mid * augm_ref[1:2, :] + lo * augm_ref[2:3, :] + augm_ref[3:4, :]
    aug_k = augm_ref[4:5, :] - hi * augm_ref[5:6, :] - mid * augm_ref[6:7, :] - lo * augm_ref[7:8, :]
    q_tiles, k_tiles = [], []
    for hd in range(N_HEADS):
        pair = slice((hd // 2) * LANE, (hd // 2 + 1) * LANE)
        data = lane < FOX_HD if hd % 2 == 0 else lane >= FOX_HD
        in_aug = (lane >= _aug_lane(hd)) & (lane < _aug_lane(hd) + AUG_W)
        q_tiles.append(jnp.where(data, fq[:, pair], jnp.where(in_aug, aug_q, 0.0)))
        k_tiles.append(jnp.where(data, fk[:, pair], jnp.where(in_aug, aug_k, 0.0)))
    qf_ref[rows, :] = jnp.concatenate(q_tiles, axis=1).astype(BF16)
    kf_ref[rows, :] = jnp.concatenate(k_tiles, axis=1).astype(BF16)
    vf_ref[rows, :] = _pad_value_heads(p[:, C_FV:C_FV + BRANCH_W]).astype(BF16)

    u_ref[rows, :] = p[:, C_U:C_U + S5_W]


def _proj(x, cos, sin, w, batch, seq):
    m = x.shape[0]
    tm = TM_PROJ
    nt = seq // tm
    row = lambda b, t: (b * nt + t, 0)
    consts = [w["attn_g"], w["wa"], w["fbias"], w["augm"], w["qlat_g"], w["wuq"], w["kvlat_g"], w["wukv"],
              w["gq"], w["gk"], w["ones_qk"], w["gfq"], w["gfk"], w["ones_fox"], w["tri"]]
    out_shape = [jax.ShapeDtypeStruct((m, QK_W), BF16)] * 6 + [
        jax.ShapeDtypeStruct((m, S5_W), F32)]
    out_specs = [pl.BlockSpec((tm, QK_W), row)] * 6 + [pl.BlockSpec((tm, S5_W), row)]
    return pl.pallas_call(
        _proj_kernel,
        grid=(batch, nt),
        in_specs=[pl.BlockSpec((tm, D_MODEL), row), pl.BlockSpec((tm, LANE), row),
                  pl.BlockSpec((tm, LANE), row)] + [_const_spec(c) for c in consts],
        out_specs=out_specs,
        out_shape=out_shape,
        scratch_shapes=[pltpu.VMEM((8, LANE), F32)],
        compiler_params=_params(("arbitrary", "arbitrary")),
        name="proj",
    )(x, cos, sin, *[c[0] for c in consts])


def _attn_kernel(q_ref, k_ref, v_ref, o_ref, acc_ref, m_ref):
    seq = q_ref.shape[0]
    tk = TK
    causal = (lax.broadcasted_iota(jnp.int32, (tk, tk), 0)
              >= lax.broadcasted_iota(jnp.int32, (tk, tk), 1))
    low_half = lax.broadcasted_iota(jnp.int32, (tk, LANE), 1) < MLA_V

    for j in range(seq // tk):
        r0 = j * tk
        for h in range(N_HEADS):
            hs = slice(h * HEAD_PAD, (h + 1) * HEAD_PAD)
            s = lax.dot_general(q_ref[r0:, hs], k_ref[r0:r0 + tk, hs],
                                (((1,), (1,)), ((), ())), preferred_element_type=F32)
            top = jnp.where(causal, s[:tk], NEG)
            s = top if seq - r0 == tk else jnp.concatenate([top, s[tk:]], axis=0)
            m_cur = jnp.max(s, axis=1, keepdims=True)
            if j == 0:
                m_new = jnp.broadcast_to(m_cur, (seq, LANE))
            else:
                m_prev = m_ref[h, r0:, :]
                m_new = jnp.maximum(m_prev, m_cur)
                alpha = jnp.exp2(m_prev - m_new)
            p = jnp.exp2(s - jnp.concatenate([m_new] * (tk // LANE), axis=1))
            pv = jnp.dot(p.astype(BF16), v_ref[r0:r0 + tk, hs], preferred_element_type=F32)
            if j == 0:
                acc_ref[h] = pv
            else:
                acc_ref[h, r0:, :] = acc_ref[h, r0:, :] * alpha + pv
            if seq - r0 > tk:
                m_ref[h, r0:, :] = m_new
        outs = []
        for h in range(N_HEADS):
            a = acc_ref[h, r0:r0 + tk, :]
            outs.append(a * (1.0 / a[:, _den_lane(h):_den_lane(h) + 1]))
        o_ref[r0:r0 + tk, :] = jnp.concatenate(
            [jnp.where(low_half, outs[0], outs[1]), jnp.where(low_half, outs[2], outs[3])],
            axis=1).astype(o_ref.dtype)


def _attention(q, k, v, batch, seq):
    m = q.shape[0]
    spec = pl.BlockSpec((seq, QK_W), lambda b: (b, 0))
    return pl.pallas_call(
        _attn_kernel,
        grid=(batch,),
        in_specs=[spec, spec, spec],
        out_specs=pl.BlockSpec((seq, BRANCH_W), lambda b: (b, 0)),
        out_shape=jax.ShapeDtypeStruct((m, BRANCH_W), BF16),
        scratch_shapes=[pltpu.VMEM((N_HEADS, seq, LANE), F32),
                        pltpu.VMEM((N_HEADS, seq, LANE), F32)],
        compiler_params=_params(("arbitrary",)),
        name="attn",
    )(q, k, v)


def _s5_kernel(u_ref, bmat_ref, lam_ref, cmat_ref, d_ref, wglu_ref, bglu_ref, o_ref, st_ref):
    batch, tt, _ = u_ref.shape

    @pl.when(pl.program_id(0) == 0)
    def _():
        st_ref[...] = jnp.zeros_like(st_ref)

    u = jnp.swapaxes(u_ref[...], 0, 1).reshape(tt * batch, S5_W)
    bu = jnp.dot(u.astype(BF16), bmat_ref[...], preferred_element_type=F32)
    lam_re = lam_ref[0:1, :]
    lam_im = lam_ref[1:2, :]
    x_re = st_ref[:, :S5_N]
    x_im = st_ref[:, S5_N:]
    states = []
    for t in range(tt):
        r = slice(t * batch, (t + 1) * batch)
        n_re = lam_re * x_re - lam_im * x_im + bu[r, :S5_N]
        n_im = lam_re * x_im + lam_im * x_re + bu[r, S5_N:]
        x_re, x_im = n_re, n_im
        states.append(jnp.concatenate([n_re, n_im], axis=1).astype(BF16))
    st_ref[:, :S5_N] = x_re
    st_ref[:, S5_N:] = x_im

    y = jnp.dot(jnp.concatenate(states, axis=0), cmat_ref[...], preferred_element_type=F32)
    y = y + d_ref[...] * u
    y = 0.5 * y * (1.0 + jnp.tanh(math.sqrt(2.0 / math.pi) * (y + 0.044715 * (y * y * y))))
    z = jnp.dot(y.astype(BF16), wglu_ref[...], preferred_element_type=F32) + bglu_ref[...]
    o = (y * jax.nn.sigmoid(z)).reshape(tt, batch, S5_W)
    o_ref[...] = jnp.swapaxes(o, 0, 1).astype(o_ref.dtype)


def _s5(u, w, batch, seq):
    consts = [w["bmat"], w["lam"], w["cmat"], w["s5_d"], w["wglu"], w["bglu"]]
    blk = pl.BlockSpec((batch, TT_S5, S5_W), lambda t: (0, t, 0))
    return pl.pallas_call(
        _s5_kernel,
        grid=(seq // TT_S5,),
        in_specs=[blk] + [_const_spec(c) for c in consts],
        out_specs=blk,
        out_shape=jax.ShapeDtypeStruct((batch, seq, S5_W), BF16),
        scratch_shapes=[pltpu.VMEM((batch, 2 * S5_N), F32)],
        compiler_params=_params(("arbitrary",)),
        name="s5",
    )(u.reshape(batch, seq, S5_W), *[c[0] for c in consts]).reshape(batch * seq, S5_W)


def _merge_kernel(x_ref, om_ref, of_ref, os_ref, g_ref, wg_ref, wbr_ref, wout_ref, o_ref):
    for r0 in range(0, x_ref.shape[0], MERGE_SUB):
        rows = slice(r0, r0 + MERGE_SUB)
        x = x_ref[rows, :]
        h = _rms(x, g_ref[...]).astype(BF16)
        merged = None
        for n, br_ref in enumerate((om_ref, of_ref, os_ref)):
            logits = jnp.dot(h, wg_ref[:, n * D_MODEL:(n + 1) * D_MODEL],
                             preferred_element_type=F32)
            proj = jnp.dot(br_ref[rows, :], wbr_ref[n], preferred_element_type=F32)
            term = jax.nn.sigmoid(logits) * proj
            merged = term if merged is None else merged + term
        o_ref[rows, :] = x + jnp.dot(merged.astype(BF16), wout_ref[...],
                                     preferred_element_type=F32)


def _merge(x, o_mla, o_fox, o_s5, w, batch, seq):
    m = x.shape[0]
    tm = TM_MERGE
    nt = seq // tm
    row = lambda b, t: (b * nt + t, 0)
    consts = [w["attn_g"], w["wg"], w["wbr"], w["wout"]]
    return pl.pallas_call(
        _merge_kernel,
        grid=(batch, nt),
        in_specs=[pl.BlockSpec((tm, D_MODEL), row)] + [pl.BlockSpec((tm, BRANCH_W), row)] * 3
                 + [_const_spec(c) for c in consts],
        out_specs=pl.BlockSpec((tm, D_MODEL), row),
        out_shape=jax.ShapeDtypeStruct((m, D_MODEL), F32),
        compiler_params=_params(("arbitrary", "arbitrary")),
        name="merge",
    )(x, o_mla, o_fox, o_s5, *[c[0] for c in consts])


FFN_CHUNK = 256


def _ffn_kernel(x_ref, g_ref, wup_ref, conv_ref, wdown_ref, o_ref, up_ref, act_ref):
    t = pl.program_id(1)
    tm = x_ref.shape[0]

    @pl.when(t == 0)
    def _():
        up_ref[0:CONV_HALO, :] = jnp.zeros((CONV_HALO, 2 * D_FF), F32)

    x = x_ref[...]
    h = _rms(x, g_ref[...]).astype(BF16)
    up_ref[CONV_HALO:CONV_HALO + tm, :] = jnp.dot(h, wup_ref[...], preferred_element_type=F32)

    def conv(cols):
        out = None
        for j in range(CONV_W):
            lo = CONV_HALO - (CONV_W - 1) + j
            term = conv_ref[j:j + 1, cols] * up_ref[lo:lo + tm, cols]
            out = term if out is None else out + term
        return out

    for c0 in range(0, D_FF, FFN_CHUNK):
        gate = conv(slice(c0, c0 + FFN_CHUNK))
        val = conv(slice(D_FF + c0, D_FF + c0 + FFN_CHUNK))
        act_ref[:, c0:c0 + FFN_CHUNK] = (gate * jax.nn.sigmoid(gate) * val).astype(BF16)

    up_ref[0:CONV_HALO, :] = up_ref[tm:tm + CONV_HALO, :]
    o_ref[...] = x + jnp.dot(act_ref[...], wdown_ref[...], preferred_element_type=F32)


def _ffn(x, w, batch, seq):
    m = x.shape[0]
    tm = TM_FFN
    nt = seq // tm
    row = lambda b, t: (b * nt + t, 0)
    consts = [w["ffn_g"], w["wup"], w["conv"], w["wdown"]]
    return pl.pallas_call(
        _ffn_kernel,
        grid=(batch, nt),
        in_specs=[pl.BlockSpec((tm, D_MODEL), row)] + [_const_spec(c) for c in consts],
        out_specs=pl.BlockSpec((tm, D_MODEL), row),
        out_shape=jax.ShapeDtypeStruct((m, D_MODEL), F32),
        scratch_shapes=[pltpu.VMEM((tm + CONV_HALO, 2 * D_FF), F32),
                        pltpu.VMEM((tm, D_FF), BF16)],
        compiler_params=_params(("arbitrary", "arbitrary")),
        name="ffn",
    )(x, *[c[0] for c in consts])


def _block_ones(n, width):
    idx = np.arange(n) // width
    return jnp.asarray((idx[:, None] == idx[None, :]).astype(np.float32), dtype=BF16)


def _pack_weights(attn_norm_g, w_in, q_lat_norm_g, w_uq, kv_lat_norm_g, w_ukv,
                  mla_q_norm_g, mla_k_norm_g, fox_q_norm_g, fox_k_norm_g, fox_f_bias,
                  s5_lambda_re, s5_lambda_im, s5_b_re, s5_b_im, s5_c_re, s5_c_im, s5_d,
                  s5_log_step, s5_w_glu, s5_b_glu, w_branch, w_out, ffn_norm_g, w_up,
                  ffn_conv_w, w_down):
    depth = w_in.shape[0]
    f32 = lambda a: a.astype(F32)
    o = np.cumsum((0, MLA_Q_RANK, MLA_KV_RANK, MLA_ROPE, BRANCH_W, BRANCH_W, BRANCH_W,
                   N_HEADS, S5_W))
    col = lambda i: w_in[:, :, o[i]:o[i + 1]]
    spans = sorted([(KR_LANE, col(2))] + [
        (_aug_lane(h), jnp.repeat(col(6)[:, :, h:h + 1], AUG_W, axis=2)) for h in range(N_HEADS)],
        key=lambda s: s[0])
    misc, lane_pos = [], 0
    for start, piece in spans + [(LANE, None)]:
        if start > lane_pos:
            misc.append(jnp.zeros((depth, D_MODEL, start - lane_pos), w_in.dtype))
        if piece is not None:
            misc.append(piece)
            lane_pos = start + piece.shape[-1]
    wa = jnp.concatenate([col(0), col(1), col(3), col(4), col(5), col(7)] + misc, axis=-1)
    wg = w_in[:, :, o[8]:]

    pad_h = lambda a: jnp.pad(a, [(0, 0)] * (a.ndim - 1) + [(0, HEAD_PAD - a.shape[-1])])
    half = MLA_ROPE // 2
    wq_g = f32(w_uq) * f32(mla_q_norm_g)[:, None, None, :] * (MLA_QK ** -0.5 * LOG2E)
    wq_partner = jnp.concatenate(
        [jnp.zeros_like(wq_g[..., :MLA_NOPE]), wq_g[..., MLA_NOPE + half:],
         wq_g[..., MLA_NOPE:MLA_NOPE + half]], axis=-1)
    wuq = jnp.concatenate([pad_h(f32(w_uq)).reshape(depth, MLA_Q_RANK, QK_W),
                           pad_h(wq_partner).reshape(depth, MLA_Q_RANK, QK_W)], axis=-1)
    wukv = jnp.concatenate(
        [pad_h(w_ukv[..., :MLA_NOPE]).reshape(depth, MLA_KV_RANK, QK_W),
         w_ukv[..., MLA_NOPE:].reshape(depth, MLA_KV_RANK, BRANCH_W)], axis=-1)
    tile_h = lambda g: jnp.tile(g, (1, N_HEADS)).reshape(depth, 1, -1)
    gq = tile_h(pad_h(f32(mla_q_norm_g))) * (MLA_QK ** -0.5 * LOG2E)
    gk = tile_h(pad_h(f32(mla_k_norm_g)))
    gfq = tile_h(f32(fox_q_norm_g)) * (FOX_HD ** -0.5 * LOG2E)
    gfk = tile_h(f32(fox_k_norm_g))
    fbias = jnp.zeros((depth, LANE), F32)
    augm = np.zeros((8, LANE), np.float32)
    for h in range(N_HEADS):
        a = _aug_lane(h)
        fbias = fbias.at[:, a:a + AUG_W].set(f32(fox_f_bias)[:, h:h + 1])
        for j in range(3):
            augm[j, a + j] = 1.0
            augm[3, a + 3 + j] = 1.0
            augm[4, a + j] = 1.0
            augm[5 + j, a + 3 + j] = 1.0
    fbias = fbias.reshape(depth, 1, LANE)

    lam_re, lam_im = f32(s5_lambda_re), f32(s5_lambda_im)
    step = jnp.exp(f32(s5_log_step))[..., None]
    mag = jnp.exp(lam_re * step)
    a_re, a_im = mag * jnp.cos(lam_im * step), mag * jnp.sin(lam_im * step)
    den = lam_re * lam_re + lam_im * lam_im
    k_re = ((a_re - 1.0) * lam_re + a_im * lam_im) / den
    k_im = (a_im * lam_re - (a_re - 1.0) * lam_im) / den
    b_re, b_im = f32(s5_b_re), f32(s5_b_im)
    bb_re = k_re[..., None] * b_re - k_im[..., None] * b_im
    bb_im = k_re[..., None] * b_im + k_im[..., None] * b_re
    eye = jnp.eye(S5_G, dtype=F32)
    bd_in = lambda a: jnp.einsum("lgph,gk->lghkp", a, eye).reshape(depth, S5_W, S5_N)
    bmat = jnp.concatenate([bd_in(bb_re), bd_in(bb_im)], axis=-1)
    bd_out = lambda a: jnp.einsum("lghp,gk->lgpkh", a, eye).reshape(depth, S5_N, S5_W)
    cmat = jnp.concatenate([bd_out(f32(s5_c_re)), -bd_out(f32(s5_c_im))], axis=1)
    lam = jnp.stack([a_re.reshape(depth, S5_N), a_im.reshape(depth, S5_N)], axis=1)

    bf = lambda a: a.astype(BF16)
    row = lambda a: f32(a).reshape(depth, 1, -1)
    tri = jnp.asarray(np.tril(np.ones((CUMSUM_BLOCK, CUMSUM_BLOCK), np.float32)), dtype=BF16)
    shared = {"ones_qk": _block_ones(2 * LANE, HEAD_PAD), "ones_fox": _block_ones(BRANCH_W, FOX_HD),
              "tri": tri, "augm": jnp.asarray(augm)}
    stacked = {
        "attn_g": row(attn_norm_g), "wa": bf(wa), "wg": bf(wg), "fbias": fbias,
        "qlat_g": row(q_lat_norm_g), "wuq": bf(wuq), "kvlat_g": row(kv_lat_norm_g),
        "wukv": bf(wukv), "gq": gq, "gk": gk, "gfq": gfq, "gfk": gfk,
        "bmat": bf(bmat), "lam": lam, "cmat": bf(cmat), "s5_d": row(s5_d),
        "wglu": bf(s5_w_glu), "bglu": row(s5_b_glu),
        "wbr": bf(w_branch), "wout": bf(w_out),
        "ffn_g": row(ffn_norm_g), "wup": bf(w_up), "conv": f32(ffn_conv_w), "wdown": bf(w_down),
    }
    return [dict({k: (v, None) for k, v in shared.items()},
                 **{k: (v, l) for k, v in stacked.items()}) for l in range(depth)]


def kernel(x, positions, attn_norm_g, w_in, q_lat_norm_g, w_uq, kv_lat_norm_g, w_ukv, mla_q_norm_g, mla_k_norm_g, fox_q_norm_g, fox_k_norm_g, fox_f_bias, s5_lambda_re, s5_lambda_im, s5_b_re, s5_b_im, s5_c_re, s5_c_im, s5_d, s5_log_step, s5_w_glu, s5_b_glu, w_branch, w_out, ffn_norm_g, w_up, ffn_conv_w, w_down):
    batch, seq, d_model = x.shape
    assert d_model == D_MODEL and seq % TM_PROJ == 0 and seq % TK == 0 and seq % TT_S5 == 0
    assert batch % 8 == 0
    layers = _pack_weights(attn_norm_g, w_in, q_lat_norm_g, w_uq, kv_lat_norm_g, w_ukv,
                           mla_q_norm_g, mla_k_norm_g, fox_q_norm_g, fox_k_norm_g, fox_f_bias,
                           s5_lambda_re, s5_lambda_im, s5_b_re, s5_b_im, s5_c_re, s5_c_im, s5_d,
                           s5_log_step, s5_w_glu, s5_b_glu, w_branch, w_out, ffn_norm_g, w_up,
                           ffn_conv_w, w_down)
    cos, sin = _rope_tables(positions)
    xf = x.astype(F32).reshape(batch * seq, D_MODEL)
    for w in layers:
        qm, km, vm, qf, kf, vf, u = _proj(xf, cos, sin, w, batch, seq)
        o_mla = _attention(qm, km, vm, batch, seq)
        o_fox = _attention(qf, kf, vf, batch, seq)
        o_s5 = _s5(u, w, batch, seq)
        xf = _merge(xf, o_mla, o_fox, o_s5, w, batch, seq)
        xf = _ffn(xf, w, batch, seq)
    return xf.reshape(batch, seq, D_MODEL).astype(x.dtype)
```

```python
import functools
import math

import numpy as np
import jax
import jax.numpy as jnp
from jax import lax
from jax.experimental import pallas as pl
from jax.experimental.pallas import tpu as pltpu

F32 = jnp.float32
BF16 = jnp.bfloat16

D_MODEL = 1024
N_HEADS = 4
MLA_NOPE = 64
MLA_ROPE = 32
MLA_QK = MLA_NOPE + MLA_ROPE
MLA_V = 64
MLA_Q_RANK = 384
MLA_KV_RANK = 256
FOX_HD = 64
S5_G = 16
S5_H = 16
S5_P = 64
S5_W = S5_G * S5_H
S5_N = S5_G * S5_P
BRANCH_W = 256
N_BRANCH = 3
D_FF = 2816
CONV_W = 3
ROPE_THETA = 10000.0
EPS = 1e-6
NEG = -1e30
LOG2E = math.log2(math.e)

LANE = 128
HEAD_PAD = 128
QK_W = N_HEADS * HEAD_PAD
C_CQ, C_CKV, C_FQ, C_FK, C_FV, C_U, C_MISC = 0, 384, 640, 896, 1152, 1408, 1664
WA_COLS = 1792
KR_LANE = MLA_NOPE
AUG_W = 6

VMEM_LIMIT = 56 * 1024 * 1024

TM_PROJ = 1024
PROJ_SUB = 512
CUMSUM_BLOCK = 256
TK = 512
TT_S5 = 64
TM_MERGE = 1024
MERGE_SUB = 512
TM_FFN = 512
CONV_HALO = 8


def _const_spec(param):
    arr, layer = param
    if layer is None:
        return pl.BlockSpec(arr.shape, lambda *_: (0,) * arr.ndim, pipeline_mode=pl.Buffered(1))
    zeros = (0,) * (arr.ndim - 1)
    return pl.BlockSpec((None,) + arr.shape[1:], lambda *_: (layer,) + zeros,
                        pipeline_mode=pl.Buffered(1))


def _params(sem):
    return pltpu.CompilerParams(dimension_semantics=sem, vmem_limit_bytes=VMEM_LIMIT)


def _rms(x, gain):
    ms = jnp.mean(x * x, axis=-1, keepdims=True)
    return x * lax.rsqrt(ms + EPS) * gain


def _split3(x):
    hi = x.astype(BF16)
    r = x - hi.astype(F32)
    mid = r.astype(BF16)
    lo = (r - mid.astype(F32)).astype(BF16)
    return hi, mid, lo


def _head_sums(sq, ones_ref):
    sq = sq.astype(BF16)
    ones = ones_ref[...]
    outs = [jnp.dot(sq[:, c0:c0 + 2 * LANE], ones, preferred_element_type=F32)
            for c0 in range(0, sq.shape[1], 2 * LANE)]
    return outs[0] if len(outs) == 1 else jnp.concatenate(outs, axis=1)


def _aug_lane(h):
    return MLA_QK + AUG_W * h if h % 2 == 0 else AUG_W * (h + 1)


def _den_lane(h):
    return MLA_V if h % 2 == 0 else 0


def _pad_value_heads(v):
    lane = lax.broadcasted_iota(jnp.int32, (v.shape[0], LANE), 1)
    tiles = []
    for h in range(N_HEADS):
        pair = v[:, (h // 2) * LANE:(h // 2 + 1) * LANE]
        mine = lane < MLA_V if h % 2 == 0 else lane >= MLA_V
        tiles.append(jnp.where(mine, pair, jnp.where(lane == _den_lane(h), 1.0, 0.0)))
    return jnp.concatenate(tiles, axis=1)


def _rope_table_kernel(pos_ref, inv_ref, sign_ref, cos_ref, sin_ref):
    ang = pos_ref[...] * inv_ref[...]
    cos_ref[...] = jnp.cos(ang)
    sin_ref[...] = jnp.sin(ang) * sign_ref[...]


def _rope_tables(positions):
    m = positions.size
    tm = 1024
    pos = positions.astype(F32).reshape(m, 1)
    inv_freq = ROPE_THETA ** (-jnp.arange(0, MLA_ROPE, 2, dtype=F32) / MLA_ROPE)
    half = MLA_ROPE // 2
    inv_lane = jnp.zeros((LANE,), F32)
    inv_lane = inv_lane.at[MLA_NOPE:MLA_NOPE + half].set(inv_freq)
    inv_lane = inv_lane.at[MLA_NOPE + half:MLA_QK].set(inv_freq)
    sign = np.zeros((LANE,), np.float32)
    sign[MLA_NOPE:MLA_NOPE + half] = -1.0
    sign[MLA_NOPE + half:MLA_QK] = 1.0
    return pl.pallas_call(
        _rope_table_kernel,
        grid=(m // tm,),
        in_specs=[pl.BlockSpec((tm, 1), lambda i: (i, 0)),
                  pl.BlockSpec((1, LANE), lambda i: (0, 0)),
                  pl.BlockSpec((1, LANE), lambda i: (0, 0))],
        out_specs=[pl.BlockSpec((tm, LANE), lambda i: (i, 0))] * 2,
        out_shape=[jax.ShapeDtypeStruct((m, LANE), F32)] * 2,
        compiler_params=_params(("arbitrary",)),
        name="rope_tables",
    )(pos, inv_lane.reshape(1, LANE), jnp.asarray(sign).reshape(1, LANE))


def _proj_kernel(x_ref, *refs):
    carry_ref = refs[-1]

    @pl.when(pl.program_id(1) == 0)
    def _():
        carry_ref[...] = jnp.zeros_like(carry_ref)

    for r0 in range(0, x_ref.shape[0], PROJ_SUB):
        _proj_rows(slice(r0, r0 + PROJ_SUB), x_ref, *refs)


def _proj_rows(rows, x_ref, cos_ref, sin_ref, g_ref, wa_ref, fbias_ref, augm_ref,
               qlat_g_ref, wuq_ref, kvlat_g_ref, wukv_ref, gq_ref, gk_ref, ones_qk_ref,
               gfq_ref, gfk_ref, ones_fox_ref, tri_ref,
               qm_ref, km_ref, vm_ref, qf_ref, kf_ref, vf_ref, u_ref,
               carry_ref):
    tm = PROJ_SUB
    h = _rms(x_ref[rows, :], g_ref[...]).astype(BF16)
    p = jnp.dot(h, wa_ref[...], preferred_element_type=F32)
    misc = p[:, C_MISC:C_MISC + LANE]
    lane = lax.broadcasted_iota(jnp.int32, (tm, LANE), 1)
    cos = cos_ref[rows, :]
    sin = sin_ref[rows, :]

    cq = _rms(p[:, C_CQ:C_CQ + MLA_Q_RANK], qlat_g_ref[...]).astype(BF16)
    qq = jnp.dot(cq, wuq_ref[...], preferred_element_type=F32)
    q = qq[:, :QK_W]
    cos4 = jnp.concatenate([cos] * N_HEADS, axis=1)
    sin4 = jnp.concatenate([sin] * N_HEADS, axis=1)
    r = lax.rsqrt(_head_sums(q * q, ones_qk_ref) * (1.0 / MLA_QK) + EPS)
    qm_ref[rows, :] = (r * (q * gq_ref[...] * cos4 + qq[:, QK_W:] * sin4)).astype(BF16)

    ckv = _rms(p[:, C_CKV:C_CKV + MLA_KV_RANK], kvlat_g_ref[...]).astype(BF16)
    kv = jnp.dot(ckv, wukv_ref[...], preferred_element_type=F32)
    k_rope = jnp.where((lane >= KR_LANE) & (lane < MLA_QK), misc, 0.0)
    k = kv[:, :QK_W] + jnp.concatenate([k_rope] * N_HEADS, axis=1)
    r = lax.rsqrt(_head_sums(k * k, ones_qk_ref) * (1.0 / MLA_QK) + EPS)
    krg = k_rope * gk_ref[:, :LANE]
    first_half = lane < (MLA_NOPE + MLA_ROPE // 2)
    partner = jnp.where(first_half, pltpu.roll(krg, HEAD_PAD - MLA_ROPE // 2, 1),
                        pltpu.roll(krg, MLA_ROPE // 2, 1)) * sin
    km_ref[rows, :] = (r * (k * gk_ref[...] * cos4
                            + jnp.concatenate([partner] * N_HEADS, axis=1))).astype(BF16)
    vm_ref[rows, :] = _pad_value_heads(kv[:, QK_W:]).astype(BF16)

    z = misc + fbias_ref[...]
    log_f = jnp.minimum(z, 0.0) - jnp.log1p(jnp.exp(-jnp.abs(z)))
    tri = tri_ref[...]
    parts = _split3(log_f)
    carry = carry_ref[0:1, :]
    c_blocks = []
    for r0 in range(0, tm, CUMSUM_BLOCK):
        cb = carry
        for part in parts:
            cb = cb + jnp.dot(tri, part[r0:r0 + CUMSUM_BLOCK], preferred_element_type=F32)
        carry = cb[CUMSUM_BLOCK - 1:CUMSUM_BLOCK, :]
        c_blocks.append(cb)
    carry_ref[0:1, :] = carry
    c = jnp.concatenate(c_blocks, axis=0) * LOG2E

    fq = p[:, C_FQ:C_FQ + BRANCH_W]
    fq = fq * lax.rsqrt(_head_sums(fq * fq, ones_fox_ref) * (1.0 / FOX_HD) + EPS) * gfq_ref[...]
    fk = p[:, C_FK:C_FK + BRANCH_W]
    fk = fk * lax.rsqrt(_head_sums(fk * fk, ones_fox_ref) * (1.0 / FOX_HD) + EPS) * gfk_ref[...]
    hi = c.astype(BF16).astype(F32)
    rem = c - hi
    mid = rem.astype(BF16).astype(F32)
    lo = rem - mid
    aug_q = hi * augm_ref[0:1, :] + mid * augm_ref[1:2, :] + lo * augm_ref[2:3, :] + augm_ref[3:4, :]
    aug_k = augm_ref[4:5, :] - hi * augm_ref[5:6, :] - mid * augm_ref[6:7, :] - lo * augm_ref[7:8, :]
    q_tiles, k_tiles = [], []
    for hd in range(N_HEADS):
        pair = slice((hd // 2) * LANE, (hd // 2 + 1) * LANE)
        data = lane < FOX_HD if hd % 2 == 0 else lane >= FOX_HD
        in_aug = (lane >= _aug_lane(hd)) & (lane < _aug_lane(hd) + AUG_W)
        q_tiles.append(jnp.where(data, fq[:, pair], jnp.where(in_aug, aug_q, 0.0)))
        k_tiles.append(jnp.where(data, fk[:, pair], jnp.where(in_aug, aug_k, 0.0)))
    qf_ref[rows, :] = jnp.concatenate(q_tiles, axis=1).astype(BF16)
    kf_ref[rows, :] = jnp.concatenate(k_tiles, axis=1).astype(BF16)
    vf_ref[rows, :] = _pad_value_heads(p[:, C_FV:C_FV + BRANCH_W]).astype(BF16)

    u_ref[rows, :] = p[:, C_U:C_U + S5_W]


def _proj(x, cos, sin, w, batch, seq):
    m = x.shape[0]
    tm = TM_PROJ
    nt = seq // tm
    row = lambda b, t: (b * nt + t, 0)
    consts = [w["attn_g"], w["wa"], w["fbias"], w["augm"], w["qlat_g"], w["wuq"], w["kvlat_g"], w["wukv"],
              w["gq"], w["gk"], w["ones_qk"], w["gfq"], w["gfk"], w["ones_fox"], w["tri"]]
    out_shape = [jax.ShapeDtypeStruct((m, QK_W), BF16)] * 6 + [
        jax.ShapeDtypeStruct((m, S5_W), F32)]
    out_specs = [pl.BlockSpec((tm, QK_W), row)] * 6 + [pl.BlockSpec((tm, S5_W), row)]
    return pl.pallas_call(
        _proj_kernel,
        grid=(batch, nt),
        in_specs=[pl.BlockSpec((tm, D_MODEL), row), pl.BlockSpec((tm, LANE), row),
                  pl.BlockSpec((tm, LANE), row)] + [_const_spec(c) for c in consts],
        out_specs=out_specs,
        out_shape=out_shape,
        scratch_shapes=[pltpu.VMEM((8, LANE), F32)],
        compiler_params=_params(("arbitrary", "arbitrary")),
        name="proj",
    )(x, cos, sin, *[c[0] for c in consts])


def _attn_kernel(q_ref, k_ref, v_ref, o_ref, acc_ref, m_ref):
    seq = q_ref.shape[0]
    tk = TK
    causal = (lax.broadcasted_iota(jnp.int32, (tk, tk), 0)
              >= lax.broadcasted_iota(jnp.int32, (tk, tk), 1))
    low_half = lax.broadcasted_iota(jnp.int32, (tk, LANE), 1) < MLA_V

    for j in range(seq // tk):
        r0 = j * tk
        for h in range(N_HEADS):
            hs = slice(h * HEAD_PAD, (h + 1) * HEAD_PAD)
            s = lax.dot_general(q_ref[r0:, hs], k_ref[r0:r0 + tk, hs],
                                (((1,), (1,)), ((), ())), preferred_element_type=F32)
            top = jnp.where(causal, s[:tk], NEG)
            s = top if seq - r0 == tk else jnp.concatenate([top, s[tk:]], axis=0)
            m_cur = jnp.max(s, axis=1, keepdims=True)
            if j == 0:
                m_new = jnp.broadcast_to(m_cur, (seq, LANE))
            else:
                m_prev = m_ref[h, r0:, :]
                m_new = jnp.maximum(m_prev, m_cur)
                alpha = jnp.exp2(m_prev - m_new)
            p = jnp.exp2(s - jnp.concatenate([m_new] * (tk // LANE), axis=1))
            pv = jnp.dot(p.astype(BF16), v_ref[r0:r0 + tk, hs], preferred_element_type=F32)
            if j == 0:
                acc_ref[h] = pv
            else:
                acc_ref[h, r0:, :] = acc_ref[h, r0:, :] * alpha + pv
            if seq - r0 > tk:
                m_ref[h, r0:, :] = m_new
        outs = []
        for h in range(N_HEADS):
            a = acc_ref[h, r0:r0 + tk, :]
            outs.append(a * (1.0 / a[:, _den_lane(h):_den_lane(h) + 1]))
        o_ref[r0:r0 + tk, :] = jnp.concatenate(
            [jnp.where(low_half, outs[0], outs[1]), jnp.where(low_half, outs[2], outs[3])],
            axis=1).astype(o_ref.dtype)


def _attention(q, k, v, batch, seq):
    m = q.shape[0]
    spec = pl.BlockSpec((seq, QK_W), lambda b: (b, 0))
    return pl.pallas_call(
        _attn_kernel,
        grid=(batch,),
        in_specs=[spec, spec, spec],
        out_specs=pl.BlockSpec((seq, BRANCH_W), lambda b: (b, 0)),
        out_shape=jax.ShapeDtypeStruct((m, BRANCH_W), BF16),
        scratch_shapes=[pltpu.VMEM((N_HEADS, seq, LANE), F32),
                        pltpu.VMEM((N_HEADS, seq, LANE), F32)],
        compiler_params=_params(("arbitrary",)),
        name="attn",
    )(q, k, v)


def _s5_kernel(u_ref, bmat_ref, lam_ref, cmat_ref, d_ref, wglu_ref, bglu_ref, o_ref, st_ref):
    batch, tt, _ = u_ref.shape

    @pl.when(pl.program_id(0) == 0)
    def _():
        st_ref[...] = jnp.zeros_like(st_ref)

    u = jnp.swapaxes(u_ref[...], 0, 1).reshape(tt * batch, S5_W)
    half = tt * batch // 2
    u16 = u.astype(BF16)
    bu = jnp.concatenate([jnp.dot(u16[r0:r0 + half], bmat_ref[...], preferred_element_type=F32)
                          for r0 in (0, half)], axis=0)
    lam_re = lam_ref[0:1, :]
    lam_im = lam_ref[1:2, :]
    x_re = st_ref[:, :S5_N]
    x_im = st_ref[:, S5_N:]
    states = []
    for t in range(tt):
        r = slice(t * batch, (t + 1) * batch)
        n_re = lam_re * x_re - lam_im * x_im + bu[r, :S5_N]
        n_im = lam_re * x_im + lam_im * x_re + bu[r, S5_N:]
        x_re, x_im = n_re, n_im
        states.append(jnp.concatenate([n_re, n_im], axis=1).astype(BF16))
    st_ref[:, :S5_N] = x_re
    st_ref[:, S5_N:] = x_im

    st = jnp.concatenate(states, axis=0)
    y = jnp.concatenate([jnp.dot(st[r0:r0 + half], cmat_ref[...], preferred_element_type=F32)
                         for r0 in (0, half)], axis=0)
    y = y + d_ref[...] * u
    y = 0.5 * y * (1.0 + jnp.tanh(math.sqrt(2.0 / math.pi) * (y + 0.044715 * (y * y * y))))
    z = jnp.dot(y.astype(BF16), wglu_ref[...], preferred_element_type=F32) + bglu_ref[...]
    o = (y * jax.nn.sigmoid(z)).reshape(tt, batch, S5_W)
    o_ref[...] = jnp.swapaxes(o, 0, 1).astype(o_ref.dtype)


def _s5(u, w, batch, seq):
    consts = [w["bmat"], w["lam"], w["cmat"], w["s5_d"], w["wglu"], w["bglu"]]
    blk = pl.BlockSpec((batch, TT_S5, S5_W), lambda t: (0, t, 0))
    return pl.pallas_call(
        _s5_kernel,
        grid=(seq // TT_S5,),
        in_specs=[blk] + [_const_spec(c) for c in consts],
        out_specs=blk,
        out_shape=jax.ShapeDtypeStruct((batch, seq, S5_W), BF16),
        scratch_shapes=[pltpu.VMEM((batch, 2 * S5_N), F32)],
        compiler_params=_params(("arbitrary",)),
        name="s5",
    )(u.reshape(batch, seq, S5_W), *[c[0] for c in consts]).reshape(batch * seq, S5_W)


def _merge_kernel(x_ref, om_ref, of_ref, os_ref, g_ref, wg_ref, wbr_ref, wout_ref, o_ref):
    for r0 in range(0, x_ref.shape[0], MERGE_SUB):
        rows = slice(r0, r0 + MERGE_SUB)
        x = x_ref[rows, :]
        h = _rms(x, g_ref[...]).astype(BF16)
        merged = None
        for n, br_ref in enumerate((om_ref, of_ref, os_ref)):
            logits = jnp.dot(h, wg_ref[:, n * D_MODEL:(n + 1) * D_MODEL],
                             preferred_element_type=F32)
            proj = jnp.dot(br_ref[rows, :], wbr_ref[n], preferred_element_type=F32)
            term = jax.nn.sigmoid(logits) * proj
            merged = term if merged is None else merged + term
        o_ref[rows, :] = x + jnp.dot(merged.astype(BF16), wout_ref[...],
                                     preferred_element_type=F32)


def _merge(x, o_mla, o_fox, o_s5, w, batch, seq):
    m = x.shape[0]
    tm = TM_MERGE
    nt = seq // tm
    row = lambda b, t: (b * nt + t, 0)
    consts = [w["attn_g"], w["wg"], w["wbr"], w["wout"]]
    return pl.pallas_call(
        _merge_kernel,
        grid=(batch, nt),
        in_specs=[pl.BlockSpec((tm, D_MODEL), row)] + [pl.BlockSpec((tm, BRANCH_W), row)] * 3
                 + [_const_spec(c) for c in consts],
        out_specs=pl.BlockSpec((tm, D_MODEL), row),
        out_shape=jax.ShapeDtypeStruct((m, D_MODEL), F32),
        compiler_params=_params(("arbitrary", "arbitrary")),
        name="merge",
    )(x, o_mla, o_fox, o_s5, *[c[0] for c in consts])


FFN_CHUNK = 256


def _ffn_kernel(x_ref, g_ref, wup_ref, conv_ref, wdown_ref, o_ref, up_ref, act_ref):
    t = pl.program_id(1)
    tm = x_ref.shape[0]

    @pl.when(t == 0)
    def _():
        up_ref[0:CONV_HALO, :] = jnp.zeros((CONV_HALO, 2 * D_FF), F32)

    x = x_ref[...]
    h = _rms(x, g_ref[...]).astype(BF16)
    up_ref[CONV_HALO:CONV_HALO + tm, :] = jnp.dot(h, wup_ref[...], preferred_element_type=F32)

    def conv(cols):
        out = None
        for j in range(CONV_W):
            lo = CONV_HALO - (CONV_W - 1) + j
            term = conv_ref[j:j + 1, cols] * up_ref[lo:lo + tm, cols]
            out = term if out is None else out + term
        return out

    for c0 in range(0, D_FF, FFN_CHUNK):
        gate = conv(slice(c0, c0 + FFN_CHUNK))
        val = conv(slice(D_FF + c0, D_FF + c0 + FFN_CHUNK))
        act_ref[:, c0:c0 + FFN_CHUNK] = (gate * jax.nn.sigmoid(gate) * val).astype(BF16)

    up_ref[0:CONV_HALO, :] = up_ref[tm:tm + CONV_HALO, :]
    o_ref[...] = x + jnp.dot(act_ref[...], wdown_ref[...], preferred_element_type=F32)


def _ffn(x, w, batch, seq):
    m = x.shape[0]
    tm = TM_FFN
    nt = seq // tm
    row = lambda b, t: (b * nt + t, 0)
    consts = [w["ffn_g"], w["wup"], w["conv"], w["wdown"]]
    return pl.pallas_call(
        _ffn_kernel,
        grid=(batch, nt),
        in_specs=[pl.BlockSpec((tm, D_MODEL), row)] + [_const_spec(c) for c in consts],
        out_specs=pl.BlockSpec((tm, D_MODEL), row),
        out_shape=jax.ShapeDtypeStruct((m, D_MODEL), F32),
        scratch_shapes=[pltpu.VMEM((tm + CONV_HALO, 2 * D_FF), F32),
                        pltpu.VMEM((tm, D_FF), BF16)],
        compiler_params=_params(("arbitrary", "arbitrary")),
        name="ffn",
    )(x, *[c[0] for c in consts])


def _block_ones(n, width):
    idx = np.arange(n) // width
    return jnp.asarray((idx[:, None] == idx[None, :]).astype(np.float32), dtype=BF16)


def _pack_weights(attn_norm_g, w_in, q_lat_norm_g, w_uq, kv_lat_norm_g, w_ukv,
                  mla_q_norm_g, mla_k_norm_g, fox_q_norm_g, fox_k_norm_g, fox_f_bias,
                  s5_lambda_re, s5_lambda_im, s5_b_re, s5_b_im, s5_c_re, s5_c_im, s5_d,
                  s5_log_step, s5_w_glu, s5_b_glu, w_branch, w_out, ffn_norm_g, w_up,
                  ffn_conv_w, w_down):
    depth = w_in.shape[0]
    f32 = lambda a: a.astype(F32)
    o = np.cumsum((0, MLA_Q_RANK, MLA_KV_RANK, MLA_ROPE, BRANCH_W, BRANCH_W, BRANCH_W,
                   N_HEADS, S5_W))
    col = lambda i: w_in[:, :, o[i]:o[i + 1]]
    spans = sorted([(KR_LANE, col(2))] + [
        (_aug_lane(h), jnp.repeat(col(6)[:, :, h:h + 1], AUG_W, axis=2)) for h in range(N_HEADS)],
        key=lambda s: s[0])
    misc, lane_pos = [], 0
    for start, piece in spans + [(LANE, None)]:
        if start > lane_pos:
            misc.append(jnp.zeros((depth, D_MODEL, start - lane_pos), w_in.dtype))
        if piece is not None:
            misc.append(piece)
            lane_pos = start + piece.shape[-1]
    wa = jnp.concatenate([col(0), col(1), col(3), col(4), col(5), col(7)] + misc, axis=-1)
    wg = w_in[:, :, o[8]:]

    pad_h = lambda a: jnp.pad(a, [(0, 0)] * (a.ndim - 1) + [(0, HEAD_PAD - a.shape[-1])])
    half = MLA_ROPE // 2
    wq_g = f32(w_uq) * f32(mla_q_norm_g)[:, None, None, :] * (MLA_QK ** -0.5 * LOG2E)
    wq_partner = jnp.concatenate(
        [jnp.zeros_like(wq_g[..., :MLA_NOPE]), wq_g[..., MLA_NOPE + half:],
         wq_g[..., MLA_NOPE:MLA_NOPE + half]], axis=-1)
    wuq = jnp.concatenate([pad_h(f32(w_uq)).reshape(depth, MLA_Q_RANK, QK_W),
                           pad_h(wq_partner).reshape(depth, MLA_Q_RANK, QK_W)], axis=-1)
    wukv = jnp.concatenate(
        [pad_h(w_ukv[..., :MLA_NOPE]).reshape(depth, MLA_KV_RANK, QK_W),
         w_ukv[..., MLA_NOPE:].reshape(depth, MLA_KV_RANK, BRANCH_W)], axis=-1)
    tile_h = lambda g: jnp.tile(g, (1, N_HEADS)).reshape(depth, 1, -1)
    gq = tile_h(pad_h(f32(mla_q_norm_g))) * (MLA_QK ** -0.5 * LOG2E)
    gk = tile_h(pad_h(f32(mla_k_norm_g)))
    gfq = tile_h(f32(fox_q_norm_g)) * (FOX_HD ** -0.5 * LOG2E)
    gfk = tile_h(f32(fox_k_norm_g))
    fbias = jnp.zeros((depth, LANE), F32)
    augm = np.zeros((8, LANE), np.float32)
    for h in range(N_HEADS):
        a = _aug_lane(h)
        fbias = fbias.at[:, a:a + AUG_W].set(f32(fox_f_bias)[:, h:h + 1])
        for j in range(3):
            augm[j, a + j] = 1.0
            augm[3, a + 3 + j] = 1.0
            augm[4, a + j] = 1.0
            augm[5 + j, a + 3 + j] = 1.0
    fbias = fbias.reshape(depth, 1, LANE)

    lam_re, lam_im = f32(s5_lambda_re), f32(s5_lambda_im)
    step = jnp.exp(f32(s5_log_step))[..., None]
    mag = jnp.exp(lam_re * step)
    a_re, a_im = mag * jnp.cos(lam_im * step), mag * jnp.sin(lam_im * step)
    den = lam_re * lam_re + lam_im * lam_im
    k_re = ((a_re - 1.0) * lam_re + a_im * lam_im) / den
    k_im = (a_im * lam_re - (a_re - 1.0) * lam_im) / den
    b_re, b_im = f32(s5_b_re), f32(s5_b_im)
    bb_re = k_re[..., None] * b_re - k_im[..., None] * b_im
    bb_im = k_re[..., None] * b_im + k_im[..., None] * b_re
    eye = jnp.eye(S5_G, dtype=F32)
    bd_in = lambda a: jnp.einsum("lgph,gk->lghkp", a, eye).reshape(depth, S5_W, S5_N)
    bmat = jnp.concatenate([bd_in(bb_re), bd_in(bb_im)], axis=-1)
    bd_out = lambda a: jnp.einsum("lghp,gk->lgpkh", a, eye).reshape(depth, S5_N, S5_W)
    cmat = jnp.concatenate([bd_out(f32(s5_c_re)), -bd_out(f32(s5_c_im))], axis=1)
    lam = jnp.stack([a_re.reshape(depth, S5_N), a_im.reshape(depth, S5_N)], axis=1)

    bf = lambda a: a.astype(BF16)
    row = lambda a: f32(a).reshape(depth, 1, -1)
    tri = jnp.asarray(np.tril(np.ones((CUMSUM_BLOCK, CUMSUM_BLOCK), np.float32)), dtype=BF16)
    shared = {"ones_qk": _block_ones(2 * LANE, HEAD_PAD), "ones_fox": _block_ones(BRANCH_W, FOX_HD),
              "tri": tri, "augm": jnp.asarray(augm)}
    stacked = {
        "attn_g": row(attn_norm_g), "wa": bf(wa), "wg": bf(wg), "fbias": fbias,
        "qlat_g": row(q_lat_norm_g), "wuq": bf(wuq), "kvlat_g": row(kv_lat_norm_g),
        "wukv": bf(wukv), "gq": gq, "gk": gk, "gfq": gfq, "gfk": gfk,
        "bmat": bf(bmat), "lam": lam, "cmat": bf(cmat), "s5_d": row(s5_d),
        "wglu": bf(s5_w_glu), "bglu": row(s5_b_glu),
        "wbr": bf(w_branch), "wout": bf(w_out),
        "ffn_g": row(ffn_norm_g), "wup": bf(w_up), "conv": f32(ffn_conv_w), "wdown": bf(w_down),
    }
    return [dict({k: (v, None) for k, v in shared.items()},
                 **{k: (v, l) for k, v in stacked.items()}) for l in range(depth)]


def kernel(x, positions, attn_norm_g, w_in, q_lat_norm_g, w_uq, kv_lat_norm_g, w_ukv, mla_q_norm_g, mla_k_norm_g, fox_q_norm_g, fox_k_norm_g, fox_f_bias, s5_lambda_re, s5_lambda_im, s5_b_re, s5_b_im, s5_c_re, s5_c_im, s5_d, s5_log_step, s5_w_glu, s5_b_glu, w_branch, w_out, ffn_norm_g, w_up, ffn_conv_w, w_down):
    batch, seq, d_model = x.shape
    assert d_model == D_MODEL and seq % TM_PROJ == 0 and seq % TK == 0 and seq % TT_S5 == 0
    assert batch % 8 == 0
    layers = _pack_weights(attn_norm_g, w_in, q_lat_norm_g, w_uq, kv_lat_norm_g, w_ukv,
                           mla_q_norm_g, mla_k_norm_g, fox_q_norm_g, fox_k_norm_g, fox_f_bias,
                           s5_lambda_re, s5_lambda_im, s5_b_re, s5_b_im, s5_c_re, s5_c_im, s5_d,
                           s5_log_step, s5_w_glu, s5_b_glu, w_branch, w_out, ffn_norm_g, w_up,
                           ffn_conv_w, w_down)
    cos, sin = _rope_tables(positions)
    xf = x.astype(F32).reshape(batch * seq, D_MODEL)
    for w in layers:
        qm, km, vm, qf, kf, vf, u = _proj(xf, cos, sin, w, batch, seq)
        o_mla = _attention(qm, km, vm, batch, seq)
        o_fox = _attention(qf, kf, vf, batch, seq)
        o_s5 = _s5(u, w, batch, seq)
        xf = _merge(xf, o_mla, o_fox, o_s5, w, batch, seq)
        xf = _ffn(xf, w, batch, seq)
    return xf.reshape(batch, seq, D_MODEL).astype(x.dtype)
```

```python
import functools
import math

import numpy as np
import jax
import jax.numpy as jnp
from jax import lax
from jax.experimental import pallas as pl
from jax.experimental.pallas import tpu as pltpu

F32 = jnp.float32
BF16 = jnp.bfloat16

D_MODEL = 1024
N_HEADS = 4
MLA_NOPE = 64
MLA_ROPE = 32
MLA_QK = MLA_NOPE + MLA_ROPE
MLA_V = 64
MLA_Q_RANK = 384
MLA_KV_RANK = 256
FOX_HD = 64
S5_G = 16
S5_H = 16
S5_P = 64
S5_W = S5_G * S5_H
S5_N = S5_G * S5_P
BRANCH_W = 256
N_BRANCH = 3
D_FF = 2816
CONV_W = 3
ROPE_THETA = 10000.0
EPS = 1e-6
NEG = -1e30
LOG2E = math.log2(math.e)

LANE = 128
HEAD_PAD = 128
QK_W = N_HEADS * HEAD_PAD
C_CQ, C_CKV, C_FQ, C_FK, C_FV, C_U, C_MISC = 0, 384, 640, 896, 1152, 1408, 1664
WA_COLS = 1792
KR_LANE = MLA_NOPE
AUG_W = 6

VMEM_LIMIT = 56 * 1024 * 1024

TM_PROJ = 1024
PROJ_SUB = 512
CUMSUM_BLOCK = 256
TK = 512
TT_S5 = 64
TM_MERGE = 1024
MERGE_SUB = 512
TM_FFN = 512
CONV_HALO = 8


def _const_spec(param):
    arr, layer = param
    if layer is None:
        return pl.BlockSpec(arr.shape, lambda *_: (0,) * arr.ndim, pipeline_mode=pl.Buffered(1))
    zeros = (0,) * (arr.ndim - 1)
    return pl.BlockSpec((None,) + arr.shape[1:], lambda *_: (layer,) + zeros,
                        pipeline_mode=pl.Buffered(1))


def _params(sem):
    return pltpu.CompilerParams(dimension_semantics=sem, vmem_limit_bytes=VMEM_LIMIT)


def _rms(x, gain):
    ms = jnp.mean(x * x, axis=-1, keepdims=True)
    return x * lax.rsqrt(ms + EPS) * gain


def _split3(x):
    hi = x.astype(BF16)
    r = x - hi.astype(F32)
    mid = r.astype(BF16)
    lo = (r - mid.astype(F32)).astype(BF16)
    return hi, mid, lo


def _head_sums(sq, ones_ref):
    sq = sq.astype(BF16)
    ones = ones_ref[...]
    outs = [jnp.dot(sq[:, c0:c0 + 2 * LANE], ones, preferred_element_type=F32)
            for c0 in range(0, sq.shape[1], 2 * LANE)]
    return outs[0] if len(outs) == 1 else jnp.concatenate(outs, axis=1)


def _aug_lane(h):
    return MLA_QK + AUG_W * h if h % 2 == 0 else AUG_W * (h + 1)


def _den_lane(h):
    return MLA_V if h % 2 == 0 else 0


def _pad_value_heads(v):
    lane = lax.broadcasted_iota(jnp.int32, (v.shape[0], LANE), 1)
    tiles = []
    for h in range(N_HEADS):
        pair = v[:, (h // 2) * LANE:(h // 2 + 1) * LANE]
        mine = lane < MLA_V if h % 2 == 0 else lane >= MLA_V
        tiles.append(jnp.where(mine, pair, jnp.where(lane == _den_lane(h), 1.0, 0.0)))
    return jnp.concatenate(tiles, axis=1)


def _rope_table_kernel(pos_ref, inv_ref, sign_ref, cos_ref, sin_ref):
    ang = pos_ref[...] * inv_ref[...]
    cos_ref[...] = jnp.cos(ang)
    sin_ref[...] = jnp.sin(ang) * sign_ref[...]


def _rope_tables(positions):
    m = positions.size
    tm = 1024
    pos = positions.astype(F32).reshape(m, 1)
    inv_freq = ROPE_THETA ** (-jnp.arange(0, MLA_ROPE, 2, dtype=F32) / MLA_ROPE)
    half = MLA_ROPE // 2
    inv_lane = jnp.zeros((LANE,), F32)
    inv_lane = inv_lane.at[MLA_NOPE:MLA_NOPE + half].set(inv_freq)
    inv_lane = inv_lane.at[MLA_NOPE + half:MLA_QK].set(inv_freq)
    sign = np.zeros((LANE,), np.float32)
    sign[MLA_NOPE:MLA_NOPE + half] = -1.0
    sign[MLA_NOPE + half:MLA_QK] = 1.0
    return pl.pallas_call(
        _rope_table_kernel,
        grid=(m // tm,),
        in_specs=[pl.BlockSpec((tm, 1), lambda i: (i, 0)),
                  pl.BlockSpec((1, LANE), lambda i: (0, 0)),
                  pl.BlockSpec((1, LANE), lambda i: (0, 0))],
        out_specs=[pl.BlockSpec((tm, LANE), lambda i: (i, 0))] * 2,
        out_shape=[jax.ShapeDtypeStruct((m, LANE), F32)] * 2,
        compiler_params=_params(("arbitrary",)),
        name="rope_tables",
    )(pos, inv_lane.reshape(1, LANE), jnp.asarray(sign).reshape(1, LANE))


def _proj_kernel(x_ref, *refs):
    carry_ref = refs[-1]

    @pl.when(pl.program_id(1) == 0)
    def _():
        carry_ref[...] = jnp.zeros_like(carry_ref)

    for r0 in range(0, x_ref.shape[0], PROJ_SUB):
        _proj_rows(slice(r0, r0 + PROJ_SUB), x_ref, *refs)


def _proj_rows(rows, x_ref, cos_ref, sin_ref, g_ref, wa_ref, fbias_ref, augm_ref,
               qlat_g_ref, wuq_ref, kvlat_g_ref, wukv_ref, gq_ref, gk_ref, ones_qk_ref,
               gfq_ref, gfk_ref, ones_fox_ref, tri_ref,
               qm_ref, km_ref, vm_ref, qf_ref, kf_ref, vf_ref, u_ref,
               carry_ref):
    tm = PROJ_SUB
    h = _rms(x_ref[rows, :], g_ref[...]).astype(BF16)
    p = jnp.dot(h, wa_ref[...], preferred_element_type=F32)
    misc = p[:, C_MISC:C_MISC + LANE]
    lane = lax.broadcasted_iota(jnp.int32, (tm, LANE), 1)
    cos = cos_ref[rows, :]
    sin = sin_ref[rows, :]

    cq = _rms(p[:, C_CQ:C_CQ + MLA_Q_RANK], qlat_g_ref[...]).astype(BF16)
    q = jnp.dot(cq, wuq_ref[...], preferred_element_type=F32)
    cos4 = jnp.concatenate([cos] * N_HEADS, axis=1)
    sin4 = jnp.concatenate([sin] * N_HEADS, axis=1)
    first_half = lane < (MLA_NOPE + MLA_ROPE // 2)
    r = lax.rsqrt(_head_sums(q * q, ones_qk_ref) * (1.0 / MLA_QK) + EPS)
    qg = q * gq_ref[...]
    partners = []
    for hd in range(N_HEADS):
        blk = qg[:, hd * HEAD_PAD:(hd + 1) * HEAD_PAD]
        partners.append(jnp.where(first_half, pltpu.roll(blk, HEAD_PAD - MLA_ROPE // 2, 1),
                                  pltpu.roll(blk, MLA_ROPE // 2, 1)))
    qm_ref[rows, :] = (r * (qg * cos4 + jnp.concatenate(partners, axis=1) * sin4)).astype(BF16)

    ckv = _rms(p[:, C_CKV:C_CKV + MLA_KV_RANK], kvlat_g_ref[...]).astype(BF16)
    kv = jnp.dot(ckv, wukv_ref[...], preferred_element_type=F32)
    k_rope = jnp.where((lane >= KR_LANE) & (lane < MLA_QK), misc, 0.0)
    k = kv[:, :QK_W] + jnp.concatenate([k_rope] * N_HEADS, axis=1)
    r = lax.rsqrt(_head_sums(k * k, ones_qk_ref) * (1.0 / MLA_QK) + EPS)
    krg = k_rope * gk_ref[:, :LANE]
    partner = jnp.where(first_half, pltpu.roll(krg, HEAD_PAD - MLA_ROPE // 2, 1),
                        pltpu.roll(krg, MLA_ROPE // 2, 1)) * sin
    km_ref[rows, :] = (r * (k * gk_ref[...] * cos4
                            + jnp.concatenate([partner] * N_HEADS, axis=1))).astype(BF16)
    vm_ref[rows, :] = _pad_value_heads(kv[:, QK_W:]).astype(BF16)

    z = misc + fbias_ref[...]
    log_f = jnp.minimum(z, 0.0) - jnp.log1p(jnp.exp(-jnp.abs(z)))
    tri = tri_ref[...]
    parts = _split3(log_f)
    carry = carry_ref[0:1, :]
    c_blocks = []
    for r0 in range(0, tm, CUMSUM_BLOCK):
        cb = carry
        for part in parts:
            cb = cb + jnp.dot(tri, part[r0:r0 + CUMSUM_BLOCK], preferred_element_type=F32)
        carry = cb[CUMSUM_BLOCK - 1:CUMSUM_BLOCK, :]
        c_blocks.append(cb)
    carry_ref[0:1, :] = carry
    c = jnp.concatenate(c_blocks, axis=0) * LOG2E

    fq = p[:, C_FQ:C_FQ + BRANCH_W]
    fq = fq * lax.rsqrt(_head_sums(fq * fq, ones_fox_ref) * (1.0 / FOX_HD) + EPS) * gfq_ref[...]
    fk = p[:, C_FK:C_FK + BRANCH_W]
    fk = fk * lax.rsqrt(_head_sums(fk * fk, ones_fox_ref) * (1.0 / FOX_HD) + EPS) * gfk_ref[...]
    hi = c.astype(BF16).astype(F32)
    rem = c - hi
    mid = rem.astype(BF16).astype(F32)
    lo = rem - mid
    aug_q = hi * augm_ref[0:1, :] + mid * augm_ref[1:2, :] + lo * augm_ref[2:3, :] + augm_ref[3:4, :]
    aug_k = augm_ref[4:5, :] - hi * augm_ref[5:6, :] - mid * augm_ref[6:7, :] - lo * augm_ref[7:8, :]
    q_tiles, k_tiles = [], []
    for hd in range(N_HEADS):
        pair = slice((hd // 2) * LANE, (hd // 2 + 1) * LANE)
        data = lane < FOX_HD if hd % 2 == 0 else lane >= FOX_HD
        in_aug = (lane >= _aug_lane(hd)) & (lane < _aug_lane(hd) + AUG_W)
        q_tiles.append(jnp.where(data, fq[:, pair], jnp.where(in_aug, aug_q, 0.0)))
        k_tiles.append(jnp.where(data, fk[:, pair], jnp.where(in_aug, aug_k, 0.0)))
    qf_ref[rows, :] = jnp.concatenate(q_tiles, axis=1).astype(BF16)
    kf_ref[rows, :] = jnp.concatenate(k_tiles, axis=1).astype(BF16)
    vf_ref[rows, :] = _pad_value_heads(p[:, C_FV:C_FV + BRANCH_W]).astype(BF16)

    u_ref[rows, :] = p[:, C_U:C_U + S5_W]


def _proj(x, cos, sin, w, batch, seq):
    m = x.shape[0]
    tm = TM_PROJ
    nt = seq // tm
    row = lambda b, t: (b * nt + t, 0)
    consts = [w["attn_g"], w["wa"], w["fbias"], w["augm"], w["qlat_g"], w["wuq"], w["kvlat_g"], w["wukv"],
              w["gq"], w["gk"], w["ones_qk"], w["gfq"], w["gfk"], w["ones_fox"], w["tri"]]
    out_shape = [jax.ShapeDtypeStruct((m, QK_W), BF16)] * 6 + [
        jax.ShapeDtypeStruct((m, S5_W), F32)]
    out_specs = [pl.BlockSpec((tm, QK_W), row)] * 6 + [pl.BlockSpec((tm, S5_W), row)]
    return pl.pallas_call(
        _proj_kernel,
        grid=(batch, nt),
        in_specs=[pl.BlockSpec((tm, D_MODEL), row), pl.BlockSpec((tm, LANE), row),
                  pl.BlockSpec((tm, LANE), row)] + [_const_spec(c) for c in consts],
        out_specs=out_specs,
        out_shape=out_shape,
        scratch_shapes=[pltpu.VMEM((8, LANE), F32)],
        compiler_params=_params(("arbitrary", "arbitrary")),
        name="proj",
    )(x, cos, sin, *[c[0] for c in consts])


def _attn_kernel(q_ref, k_ref, v_ref, o_ref, acc_ref, m_ref):
    seq = q_ref.shape[0]
    tk = TK
    causal = (lax.broadcasted_iota(jnp.int32, (tk, tk), 0)
              >= lax.broadcasted_iota(jnp.int32, (tk, tk), 1))
    low_half = lax.broadcasted_iota(jnp.int32, (tk, LANE), 1) < MLA_V

    for j in range(seq // tk):
        r0 = j * tk
        for h in range(N_HEADS):
            hs = slice(h * HEAD_PAD, (h + 1) * HEAD_PAD)
            s = lax.dot_general(q_ref[r0:, hs], k_ref[r0:r0 + tk, hs],
                                (((1,), (1,)), ((), ())), preferred_element_type=F32)
            top = jnp.where(causal, s[:tk], NEG)
            s = top if seq - r0 == tk else jnp.concatenate([top, s[tk:]], axis=0)
            m_cur = jnp.max(s, axis=1, keepdims=True)
            if j == 0:
                m_new = jnp.broadcast_to(m_cur, (seq, LANE))
            else:
                m_prev = m_ref[h, r0:, :]
                m_new = jnp.maximum(m_prev, m_cur)
                alpha = jnp.exp2(m_prev - m_new)
            p = jnp.exp2(s - jnp.concatenate([m_new] * (tk // LANE), axis=1))
            pv = jnp.dot(p.astype(BF16), v_ref[r0:r0 + tk, hs], preferred_element_type=F32)
            if j == 0:
                acc_ref[h] = pv
            else:
                acc_ref[h, r0:, :] = acc_ref[h, r0:, :] * alpha + pv
            if seq - r0 > tk:
                m_ref[h, r0:, :] = m_new
        outs = []
        for h in range(N_HEADS):
            a = acc_ref[h, r0:r0 + tk, :]
            outs.append(a * (1.0 / a[:, _den_lane(h):_den_lane(h) + 1]))
        o_ref[r0:r0 + tk, :] = jnp.concatenate(
            [jnp.where(low_half, outs[0], outs[1]), jnp.where(low_half, outs[2], outs[3])],
            axis=1).astype(o_ref.dtype)


def _attention(q, k, v, batch, seq):
    m = q.shape[0]
    spec = pl.BlockSpec((seq, QK_W), lambda b: (b, 0))
    return pl.pallas_call(
        _attn_kernel,
        grid=(batch,),
        in_specs=[spec, spec, spec],
        out_specs=pl.BlockSpec((seq, BRANCH_W), lambda b: (b, 0)),
        out_shape=jax.ShapeDtypeStruct((m, BRANCH_W), BF16),
        scratch_shapes=[pltpu.VMEM((N_HEADS, seq, LANE), F32),
                        pltpu.VMEM((N_HEADS, seq, LANE), F32)],
        compiler_params=_params(("arbitrary",)),
        name="attn",
    )(q, k, v)


def _s5_kernel(u_ref, bmat_ref, lam_ref, cmat_ref, d_ref, wglu_ref, bglu_ref, o_ref, st_ref):
    batch, tt, _ = u_ref.shape

    @pl.when(pl.program_id(0) == 0)
    def _():
        st_ref[...] = jnp.zeros_like(st_ref)

    u = jnp.swapaxes(u_ref[...], 0, 1).reshape(tt * batch, S5_W)
    half = tt * batch // 2
    u16 = u.astype(BF16)
    bu = jnp.concatenate([jnp.dot(u16[r0:r0 + half], bmat_ref[...], preferred_element_type=F32)
                          for r0 in (0, half)], axis=0)
    lam_re = lam_ref[0:1, :]
    lam_im = lam_ref[1:2, :]
    x_re = st_ref[:, :S5_N]
    x_im = st_ref[:, S5_N:]
    states = []
    for t in range(tt):
        r = slice(t * batch, (t + 1) * batch)
        n_re = lam_re * x_re - lam_im * x_im + bu[r, :S5_N]
        n_im = lam_re * x_im + lam_im * x_re + bu[r, S5_N:]
        x_re, x_im = n_re, n_im
        states.append(jnp.concatenate([n_re, n_im], axis=1).astype(BF16))
    st_ref[:, :S5_N] = x_re
    st_ref[:, S5_N:] = x_im

    st = jnp.concatenate(states, axis=0)
    y = jnp.concatenate([jnp.dot(st[r0:r0 + half], cmat_ref[...], preferred_element_type=F32)
                         for r0 in (0, half)], axis=0)
    y = y + d_ref[...] * u
    y = 0.5 * y * (1.0 + jnp.tanh(math.sqrt(2.0 / math.pi) * (y + 0.044715 * (y * y * y))))
    z = jnp.dot(y.astype(BF16), wglu_ref[...], preferred_element_type=F32) + bglu_ref[...]
    o = (y * jax.nn.sigmoid(z)).reshape(tt, batch, S5_W)
    o_ref[...] = jnp.swapaxes(o, 0, 1).astype(o_ref.dtype)


def _s5(u, w, batch, seq):
    consts = [w["bmat"], w["lam"], w["cmat"], w["s5_d"], w["wglu"], w["bglu"]]
    blk = pl.BlockSpec((batch, TT_S5, S5_W), lambda t: (0, t, 0))
    return pl.pallas_call(
        _s5_kernel,
        grid=(seq // TT_S5,),
        in_specs=[blk] + [_const_spec(c) for c in consts],
        out_specs=blk,
        out_shape=jax.ShapeDtypeStruct((batch, seq, S5_W), BF16),
        scratch_shapes=[pltpu.VMEM((batch, 2 * S5_N), F32)],
        compiler_params=_params(("arbitrary",)),
        name="s5",
    )(u.reshape(batch, seq, S5_W), *[c[0] for c in consts]).reshape(batch * seq, S5_W)


def _merge_kernel(x_ref, om_ref, of_ref, os_ref, g_ref, wg_ref, wbr_ref, wout_ref, o_ref):
    for r0 in range(0, x_ref.shape[0], MERGE_SUB):
        rows = slice(r0, r0 + MERGE_SUB)
        x = x_ref[rows, :]
        h = _rms(x, g_ref[...]).astype(BF16)
        merged = None
        for n, br_ref in enumerate((om_ref, of_ref, os_ref)):
            logits = jnp.dot(h, wg_ref[:, n * D_MODEL:(n + 1) * D_MODEL],
                             preferred_element_type=F32)
            proj = jnp.dot(br_ref[rows, :], wbr_ref[n], preferred_element_type=F32)
            term = jax.nn.sigmoid(logits) * proj
            merged = term if merged is None else merged + term
        o_ref[rows, :] = x + jnp.dot(merged.astype(BF16), wout_ref[...],
                                     preferred_element_type=F32)


def _merge(x, o_mla, o_fox, o_s5, w, batch, seq):
    m = x.shape[0]
    tm = TM_MERGE
    nt = seq // tm
    row = lambda b, t: (b * nt + t, 0)
    consts = [w["attn_g"], w["wg"], w["wbr"], w["wout"]]
    return pl.pallas_call(
        _merge_kernel,
        grid=(batch, nt),
        in_specs=[pl.BlockSpec((tm, D_MODEL), row)] + [pl.BlockSpec((tm, BRANCH_W), row)] * 3
                 + [_const_spec(c) for c in consts],
        out_specs=pl.BlockSpec((tm, D_MODEL), row),
        out_shape=jax.ShapeDtypeStruct((m, D_MODEL), F32),
        compiler_params=_params(("arbitrary", "arbitrary")),
        name="merge",
    )(x, o_mla, o_fox, o_s5, *[c[0] for c in consts])


FFN_CHUNK = 256


def _ffn_kernel(x_ref, g_ref, wup_ref, conv_ref, wdown_ref, o_ref, up_ref, act_ref):
    t = pl.program_id(1)
    tm = x_ref.shape[0]

    @pl.when(t == 0)
    def _():
        up_ref[0:CONV_HALO, :] = jnp.zeros((CONV_HALO, 2 * D_FF), F32)

    x = x_ref[...]
    h = _rms(x, g_ref[...]).astype(BF16)
    up_ref[CONV_HALO:CONV_HALO + tm, :] = jnp.dot(h, wup_ref[...], preferred_element_type=F32)

    def conv(cols):
        out = None
        for j in range(CONV_W):
            lo = CONV_HALO - (CONV_W - 1) + j
            term = conv_ref[j:j + 1, cols] * up_ref[lo:lo + tm, cols]
            out = term if out is None else out + term
        return out

    for c0 in range(0, D_FF, FFN_CHUNK):
        gate = conv(slice(c0, c0 + FFN_CHUNK))
        val = conv(slice(D_FF + c0, D_FF + c0 + FFN_CHUNK))
        act_ref[:, c0:c0 + FFN_CHUNK] = (gate * jax.nn.sigmoid(gate) * val).astype(BF16)

    up_ref[0:CONV_HALO, :] = up_ref[tm:tm + CONV_HALO, :]
    o_ref[...] = x + jnp.dot(act_ref[...], wdown_ref[...], preferred_element_type=F32)


def _ffn(x, w, batch, seq):
    m = x.shape[0]
    tm = TM_FFN
    nt = seq // tm
    row = lambda b, t: (b * nt + t, 0)
    consts = [w["ffn_g"], w["wup"], w["conv"], w["wdown"]]
    return pl.pallas_call(
        _ffn_kernel,
        grid=(batch, nt),
        in_specs=[pl.BlockSpec((tm, D_MODEL), row)] + [_const_spec(c) for c in consts],
        out_specs=pl.BlockSpec((tm, D_MODEL), row),
        out_shape=jax.ShapeDtypeStruct((m, D_MODEL), F32),
        scratch_shapes=[pltpu.VMEM((tm + CONV_HALO, 2 * D_FF), F32),
                        pltpu.VMEM((tm, D_FF), BF16)],
        compiler_params=_params(("arbitrary", "arbitrary")),
        name="ffn",
    )(x, *[c[0] for c in consts])


def _block_ones(n, width):
    idx = np.arange(n) // width
    return jnp.asarray((idx[:, None] == idx[None, :]).astype(np.float32), dtype=BF16)


def _pack_weights(attn_norm_g, w_in, q_lat_norm_g, w_uq, kv_lat_norm_g, w_ukv,
                  mla_q_norm_g, mla_k_norm_g, fox_q_norm_g, fox_k_norm_g, fox_f_bias,
                  s5_lambda_re, s5_lambda_im, s5_b_re, s5_b_im, s5_c_re, s5_c_im, s5_d,
                  s5_log_step, s5_w_glu, s5_b_glu, w_branch, w_out, ffn_norm_g, w_up,
                  ffn_conv_w, w_down):
    depth = w_in.shape[0]
    f32 = lambda a: a.astype(F32)
    o = np.cumsum((0, MLA_Q_RANK, MLA_KV_RANK, MLA_ROPE, BRANCH_W, BRANCH_W, BRANCH_W,
                   N_HEADS, S5_W))
    col = lambda i: w_in[:, :, o[i]:o[i + 1]]
    spans = sorted([(KR_LANE, col(2))] + [
        (_aug_lane(h), jnp.repeat(col(6)[:, :, h:h + 1], AUG_W, axis=2)) for h in range(N_HEADS)],
        key=lambda s: s[0])
    misc, lane_pos = [], 0
    for start, piece in spans + [(LANE, None)]:
        if start > lane_pos:
            misc.append(jnp.zeros((depth, D_MODEL, start - lane_pos), w_in.dtype))
        if piece is not None:
            misc.append(piece)
            lane_pos = start + piece.shape[-1]
    wa = jnp.concatenate([col(0), col(1), col(3), col(4), col(5), col(7)] + misc, axis=-1)
    wg = w_in[:, :, o[8]:]

    pad_h = lambda a: jnp.pad(a, [(0, 0)] * (a.ndim - 1) + [(0, HEAD_PAD - a.shape[-1])])
    wuq = pad_h(w_uq).reshape(depth, MLA_Q_RANK, QK_W)
    wukv = jnp.concatenate(
        [pad_h(w_ukv[..., :MLA_NOPE]).reshape(depth, MLA_KV_RANK, QK_W),
         w_ukv[..., MLA_NOPE:].reshape(depth, MLA_KV_RANK, BRANCH_W)], axis=-1)
    tile_h = lambda g: jnp.tile(g, (1, N_HEADS)).reshape(depth, 1, -1)
    gq = tile_h(pad_h(f32(mla_q_norm_g))) * (MLA_QK ** -0.5 * LOG2E)
    gk = tile_h(pad_h(f32(mla_k_norm_g)))
    gfq = tile_h(f32(fox_q_norm_g)) * (FOX_HD ** -0.5 * LOG2E)
    gfk = tile_h(f32(fox_k_norm_g))
    fbias = jnp.zeros((depth, LANE), F32)
    augm = np.zeros((8, LANE), np.float32)
    for h in range(N_HEADS):
        a = _aug_lane(h)
        fbias = fbias.at[:, a:a + AUG_W].set(f32(fox_f_bias)[:, h:h + 1])
        for j in range(3):
            augm[j, a + j] = 1.0
            augm[3, a + 3 + j] = 1.0
            augm[4, a + j] = 1.0
            augm[5 + j, a + 3 + j] = 1.0
    fbias = fbias.reshape(depth, 1, LANE)

    lam_re, lam_im = f32(s5_lambda_re), f32(s5_lambda_im)
    step = jnp.exp(f32(s5_log_step))[..., None]
    mag = jnp.exp(lam_re * step)
    a_re, a_im = mag * jnp.cos(lam_im * step), mag * jnp.sin(lam_im * step)
    den = lam_re * lam_re + lam_im * lam_im
    k_re = ((a_re - 1.0) * lam_re + a_im * lam_im) / den
    k_im = (a_im * lam_re - (a_re - 1.0) * lam_im) / den
    b_re, b_im = f32(s5_b_re), f32(s5_b_im)
    bb_re = k_re[..., None] * b_re - k_im[..., None] * b_im
    bb_im = k_re[..., None] * b_im + k_im[..., None] * b_re
    eye = jnp.eye(S5_G, dtype=F32)
    bd_in = lambda a: jnp.einsum("lgph,gk->lghkp", a, eye).reshape(depth, S5_W, S5_N)
    bmat = jnp.concatenate([bd_in(bb_re), bd_in(bb_im)], axis=-1)
    bd_out = lambda a: jnp.einsum("lghp,gk->lgpkh", a, eye).reshape(depth, S5_N, S5_W)
    cmat = jnp.concatenate([bd_out(f32(s5_c_re)), -bd_out(f32(s5_c_im))], axis=1)
    lam = jnp.stack([a_re.reshape(depth, S5_N), a_im.reshape(depth, S5_N)], axis=1)

    bf = lambda a: a.astype(BF16)
    row = lambda a: f32(a).reshape(depth, 1, -1)
    tri = jnp.asarray(np.tril(np.ones((CUMSUM_BLOCK, CUMSUM_BLOCK), np.float32)), dtype=BF16)
    shared = {"ones_qk": _block_ones(2 * LANE, HEAD_PAD), "ones_fox": _block_ones(BRANCH_W, FOX_HD),
              "tri": tri, "augm": jnp.asarray(augm)}
    stacked = {
        "attn_g": row(attn_norm_g), "wa": bf(wa), "wg": bf(wg), "fbias": fbias,
        "qlat_g": row(q_lat_norm_g), "wuq": bf(wuq), "kvlat_g": row(kv_lat_norm_g),
        "wukv": bf(wukv), "gq": gq, "gk": gk, "gfq": gfq, "gfk": gfk,
        "bmat": bf(bmat), "lam": lam, "cmat": bf(cmat), "s5_d": row(s5_d),
        "wglu": bf(s5_w_glu), "bglu": row(s5_b_glu),
        "wbr": bf(w_branch), "wout": bf(w_out),
        "ffn_g": row(ffn_norm_g), "wup": bf(w_up), "conv": f32(ffn_conv_w), "wdown": bf(w_down),
    }
    return [dict({k: (v, None) for k, v in shared.items()},
                 **{k: (v, l) for k, v in stacked.items()}) for l in range(depth)]


def kernel(x, positions, attn_norm_g, w_in, q_lat_norm_g, w_uq, kv_lat_norm_g, w_ukv, mla_q_norm_g, mla_k_norm_g, fox_q_norm_g, fox_k_norm_g, fox_f_bias, s5_lambda_re, s5_lambda_im, s5_b_re, s5_b_im, s5_c_re, s5_c_im, s5_d, s5_log_step, s5_w_glu, s5_b_glu, w_branch, w_out, ffn_norm_g, w_up, ffn_conv_w, w_down):
    batch, seq, d_model = x.shape
    assert d_model == D_MODEL and seq % TM_PROJ == 0 and seq % TK == 0 and seq % TT_S5 == 0
    assert batch % 8 == 0
    layers = _pack_weights(attn_norm_g, w_in, q_lat_norm_g, w_uq, kv_lat_norm_g, w_ukv,
                           mla_q_norm_g, mla_k_norm_g, fox_q_norm_g, fox_k_norm_g, fox_f_bias,
                           s5_lambda_re, s5_lambda_im, s5_b_re, s5_b_im, s5_c_re, s5_c_im, s5_d,
                           s5_log_step, s5_w_glu, s5_b_glu, w_branch, w_out, ffn_norm_g, w_up,
                           ffn_conv_w, w_down)
    cos, sin = _rope_tables(positions)
    xf = x.astype(F32).reshape(batch * seq, D_MODEL)
    for w in layers:
        qm, km, vm, qf, kf, vf, u = _proj(xf, cos, sin, w, batch, seq)
        o_mla = _attention(qm, km, vm, batch, seq)
        o_fox = _attention(qf, kf, vf, batch, seq)
        o_s5 = _s5(u, w, batch, seq)
        xf = _merge(xf, o_mla, o_fox, o_s5, w, batch, seq)
        xf = _ffn(xf, w, batch, seq)
    return xf.reshape(batch, seq, D_MODEL).astype(x.dtype)
```

```python
import math

import numpy as np
import jax
import jax.numpy as jnp
from jax import lax
from jax.experimental import pallas as pl
from jax.experimental.pallas import tpu as pltpu

F32 = jnp.float32
BF16 = jnp.bfloat16

D_MODEL = 1024
N_HEADS = 4
MLA_NOPE = 64
MLA_ROPE = 32
MLA_QK = MLA_NOPE + MLA_ROPE
MLA_V = 64
MLA_Q_RANK = 384
MLA_KV_RANK = 256
FOX_HD = 64
S5_G = 16
S5_H = 16
S5_P = 64
S5_W = S5_G * S5_H
S5_N = S5_G * S5_P
BRANCH_W = 256
N_BRANCH = 3
D_FF = 2816
CONV_W = 3
ROPE_THETA = 10000.0
EPS = 1e-6
NEG = -1e30
LOG2E = math.log2(math.e)

LANE = 128
HEAD_PAD = 128
QK_W = N_HEADS * HEAD_PAD
C_CQ, C_CKV, C_FQ, C_FK, C_FV, C_U, C_MISC = 0, 384, 640, 896, 1152, 1408, 1664
WA_COLS = 1792
KR_LANE = MLA_NOPE
AUG_W = 6

VMEM_LIMIT = 56 * 1024 * 1024

TM_PROJ = 1024
PROJ_SUB = 512
CUMSUM_BLOCK = 256
TK = 512
TT_S5 = 64
TM_MERGE = 1024
MERGE_SUB = 256
TM_FFN = 512
CONV_HALO = 8


def _const_spec(param):
    arr, layer = param
    if layer is None:
        return pl.BlockSpec(arr.shape, lambda *_: (0,) * arr.ndim, pipeline_mode=pl.Buffered(1))
    zeros = (0,) * (arr.ndim - 1)
    return pl.BlockSpec((None,) + arr.shape[1:], lambda *_: (layer,) + zeros,
                        pipeline_mode=pl.Buffered(1))


def _params(sem):
    return pltpu.CompilerParams(dimension_semantics=sem, vmem_limit_bytes=VMEM_LIMIT)


def _rms(x, gain):
    ms = jnp.mean(x * x, axis=-1, keepdims=True)
    return x * lax.rsqrt(ms + EPS) * gain


def _split3(x):
    hi = x.astype(BF16)
    r = x - hi.astype(F32)
    mid = r.astype(BF16)
    lo = (r - mid.astype(F32)).astype(BF16)
    return hi, mid, lo


def _head_sums(sq, ones_ref):
    sq = sq.astype(BF16)
    ones = ones_ref[...]
    outs = [jnp.dot(sq[:, c0:c0 + 2 * LANE], ones, preferred_element_type=F32)
            for c0 in range(0, sq.shape[1], 2 * LANE)]
    return outs[0] if len(outs) == 1 else jnp.concatenate(outs, axis=1)


def _aug_lane(h):
    return MLA_QK + AUG_W * h if h % 2 == 0 else AUG_W * (h + 1)


def _den_lane(h):
    return MLA_V if h % 2 == 0 else 0


def _pad_value_heads(v):
    lane = lax.broadcasted_iota(jnp.int32, (v.shape[0], LANE), 1)
    tiles = []
    for h in range(N_HEADS):
        pair = v[:, (h // 2) * LANE:(h // 2 + 1) * LANE]
        mine = lane < MLA_V if h % 2 == 0 else lane >= MLA_V
        tiles.append(jnp.where(mine, pair, jnp.where(lane == _den_lane(h), 1.0, 0.0)))
    return jnp.concatenate(tiles, axis=1)


def _rope_table_kernel(pos_ref, inv_ref, sign_ref, cos_ref, sin_ref):
    ang = pos_ref[...] * inv_ref[...]
    cos_ref[...] = jnp.cos(ang)
    sin_ref[...] = jnp.sin(ang) * sign_ref[...]


def _rope_tables(positions):
    m = positions.size
    tm = 1024
    pos = positions.astype(F32).reshape(m, 1)
    inv_freq = ROPE_THETA ** (-jnp.arange(0, MLA_ROPE, 2, dtype=F32) / MLA_ROPE)
    half = MLA_ROPE // 2
    inv_lane = jnp.zeros((LANE,), F32)
    inv_lane = inv_lane.at[MLA_NOPE:MLA_NOPE + half].set(inv_freq)
    inv_lane = inv_lane.at[MLA_NOPE + half:MLA_QK].set(inv_freq)
    sign = np.zeros((LANE,), np.float32)
    sign[MLA_NOPE:MLA_NOPE + half] = -1.0
    sign[MLA_NOPE + half:MLA_QK] = 1.0
    return pl.pallas_call(
        _rope_table_kernel,
        grid=(m // tm,),
        in_specs=[pl.BlockSpec((tm, 1), lambda i: (i, 0)),
                  pl.BlockSpec((1, LANE), lambda i: (0, 0)),
                  pl.BlockSpec((1, LANE), lambda i: (0, 0))],
        out_specs=[pl.BlockSpec((tm, LANE), lambda i: (i, 0))] * 2,
        out_shape=[jax.ShapeDtypeStruct((m, LANE), F32)] * 2,
        compiler_params=_params(("arbitrary",)),
        name="rope_tables",
    )(pos, inv_lane.reshape(1, LANE), jnp.asarray(sign).reshape(1, LANE))


def _proj_kernel(x_ref, *refs):
    carry_ref = refs[-1]

    @pl.when(pl.program_id(1) == 0)
    def _():
        carry_ref[...] = jnp.zeros_like(carry_ref)

    for r0 in range(0, x_ref.shape[0], PROJ_SUB):
        _proj_rows(slice(r0, r0 + PROJ_SUB), x_ref, *refs)


def _proj_rows(rows, x_ref, cos_ref, sin_ref, g_ref, wa_ref, fbias_ref, augm_ref,
               qlat_g_ref, wuq_ref, kvlat_g_ref, wukv_ref, gq_ref, gk_ref, ones_qk_ref,
               gfq_ref, gfk_ref, ones_fox_ref, tri_ref,
               qm_ref, km_ref, vm_ref, qf_ref, kf_ref, vf_ref, u_ref,
               carry_ref):
    tm = PROJ_SUB
    h = _rms(x_ref[rows, :], g_ref[...]).astype(BF16)
    p = jnp.dot(h, wa_ref[...], preferred_element_type=F32)
    misc = p[:, C_MISC:C_MISC + LANE]
    lane = lax.broadcasted_iota(jnp.int32, (tm, LANE), 1)
    cos = cos_ref[rows, :]
    sin = sin_ref[rows, :]

    cq = _rms(p[:, C_CQ:C_CQ + MLA_Q_RANK], qlat_g_ref[...]).astype(BF16)
    q = jnp.dot(cq, wuq_ref[...], preferred_element_type=F32)
    cos4 = jnp.concatenate([cos] * N_HEADS, axis=1)
    sin4 = jnp.concatenate([sin] * N_HEADS, axis=1)
    first_half = lane < (MLA_NOPE + MLA_ROPE // 2)
    r = lax.rsqrt(_head_sums(q * q, ones_qk_ref) * (1.0 / MLA_QK) + EPS)
    qg = q * gq_ref[...]
    partners = []
    for hd in range(N_HEADS):
        blk = qg[:, hd * HEAD_PAD:(hd + 1) * HEAD_PAD]
        partners.append(jnp.where(first_half, pltpu.roll(blk, HEAD_PAD - MLA_ROPE // 2, 1),
                                  pltpu.roll(blk, MLA_ROPE // 2, 1)))
    qm_ref[rows, :] = (r * (qg * cos4 + jnp.concatenate(partners, axis=1) * sin4)).astype(BF16)

    ckv = _rms(p[:, C_CKV:C_CKV + MLA_KV_RANK], kvlat_g_ref[...]).astype(BF16)
    kv = jnp.dot(ckv, wukv_ref[...], preferred_element_type=F32)
    k_rope = jnp.where((lane >= KR_LANE) & (lane < MLA_QK), misc, 0.0)
    k = kv[:, :QK_W] + jnp.concatenate([k_rope] * N_HEADS, axis=1)
    r = lax.rsqrt(_head_sums(k * k, ones_qk_ref) * (1.0 / MLA_QK) + EPS)
    krg = k_rope * gk_ref[:, :LANE]
    partner = jnp.where(first_half, pltpu.roll(krg, HEAD_PAD - MLA_ROPE // 2, 1),
                        pltpu.roll(krg, MLA_ROPE // 2, 1)) * sin
    km_ref[rows, :] = (r * (k * gk_ref[...] * cos4
                            + jnp.concatenate([partner] * N_HEADS, axis=1))).astype(BF16)
    vm_ref[rows, :] = _pad_value_heads(kv[:, QK_W:]).astype(BF16)

    z = misc + fbias_ref[...]
    log_f = jnp.minimum(z, 0.0) - jnp.log1p(jnp.exp(-jnp.abs(z)))
    tri = tri_ref[...]
    parts = _split3(log_f)
    carry = carry_ref[0:1, :]
    c_blocks = []
    for r0 in range(0, tm, CUMSUM_BLOCK):
        cb = carry
        for part in parts:
            cb = cb + jnp.dot(tri, part[r0:r0 + CUMSUM_BLOCK], preferred_element_type=F32)
        carry = cb[CUMSUM_BLOCK - 1:CUMSUM_BLOCK, :]
        c_blocks.append(cb)
    carry_ref[0:1, :] = carry
    c = jnp.concatenate(c_blocks, axis=0) * LOG2E

    fq = p[:, C_FQ:C_FQ + BRANCH_W]
    fq = fq * lax.rsqrt(_head_sums(fq * fq, ones_fox_ref) * (1.0 / FOX_HD) + EPS) * gfq_ref[...]
    fk = p[:, C_FK:C_FK + BRANCH_W]
    fk = fk * lax.rsqrt(_head_sums(fk * fk, ones_fox_ref) * (1.0 / FOX_HD) + EPS) * gfk_ref[...]
    hi = c.astype(BF16).astype(F32)
    rem = c - hi
    mid = rem.astype(BF16).astype(F32)
    lo = rem - mid
    aug_q = hi * augm_ref[0:1, :] + mid * augm_ref[1:2, :] + lo * augm_ref[2:3, :] + augm_ref[3:4, :]
    aug_k = augm_ref[4:5, :] - hi * augm_ref[5:6, :] - mid * augm_ref[6:7, :] - lo * augm_ref[7:8, :]
    q_tiles, k_tiles = [], []
    for hd in range(N_HEADS):
        pair = slice((hd // 2) * LANE, (hd // 2 + 1) * LANE)
        data = lane < FOX_HD if hd % 2 == 0 else lane >= FOX_HD
        in_aug = (lane >= _aug_lane(hd)) & (lane < _aug_lane(hd) + AUG_W)
        q_tiles.append(jnp.where(data, fq[:, pair], jnp.where(in_aug, aug_q, 0.0)))
        k_tiles.append(jnp.where(data, fk[:, pair], jnp.where(in_aug, aug_k, 0.0)))
    qf_ref[rows, :] = jnp.concatenate(q_tiles, axis=1).astype(BF16)
    kf_ref[rows, :] = jnp.concatenate(k_tiles, axis=1).astype(BF16)
    vf_ref[rows, :] = _pad_value_heads(p[:, C_FV:C_FV + BRANCH_W]).astype(BF16)

    u_ref[rows, :] = p[:, C_U:C_U + S5_W]


def _proj(x, cos, sin, w, batch, seq):
    m = x.shape[0]
    tm = TM_PROJ
    nt = seq // tm
    row = lambda b, t: (b * nt + t, 0)
    consts = [w["attn_g"], w["wa"], w["fbias"], w["augm"], w["qlat_g"], w["wuq"], w["kvlat_g"], w["wukv"],
              w["gq"], w["gk"], w["ones_qk"], w["gfq"], w["gfk"], w["ones_fox"], w["tri"]]
    out_shape = [jax.ShapeDtypeStruct((m, QK_W), BF16)] * 6 + [
        jax.ShapeDtypeStruct((m, S5_W), F32)]
    out_specs = [pl.BlockSpec((tm, QK_W), row)] * 6 + [pl.BlockSpec((tm, S5_W), row)]
    return pl.pallas_call(
        _proj_kernel,
        grid=(batch, nt),
        in_specs=[pl.BlockSpec((tm, D_MODEL), row), pl.BlockSpec((tm, LANE), row),
                  pl.BlockSpec((tm, LANE), row)] + [_const_spec(c) for c in consts],
        out_specs=out_specs,
        out_shape=out_shape,
        scratch_shapes=[pltpu.VMEM((8, LANE), F32)],
        compiler_params=_params(("arbitrary", "arbitrary")),
        name="proj",
    )(x, cos, sin, *[c[0] for c in consts])


def _attn_kernel(q_ref, k_ref, v_ref, o_ref, acc_ref, m_ref):
    seq = q_ref.shape[0]
    tk = TK
    causal = (lax.broadcasted_iota(jnp.int32, (tk, tk), 0)
              >= lax.broadcasted_iota(jnp.int32, (tk, tk), 1))
    low_half = lax.broadcasted_iota(jnp.int32, (tk, LANE), 1) < MLA_V

    for j in range(seq // tk):
        r0 = j * tk
        for h in range(N_HEADS):
            hs = slice(h * HEAD_PAD, (h + 1) * HEAD_PAD)
            s = lax.dot_general(q_ref[r0:, hs], k_ref[r0:r0 + tk, hs],
                                (((1,), (1,)), ((), ())), preferred_element_type=F32)
            top = jnp.where(causal, s[:tk], NEG)
            s = top if seq - r0 == tk else jnp.concatenate([top, s[tk:]], axis=0)
            m_cur = jnp.max(s, axis=1, keepdims=True)
            if j == 0:
                m_new = jnp.broadcast_to(m_cur, (seq, LANE))
            else:
                m_prev = m_ref[h, r0:, :]
                m_new = jnp.maximum(m_prev, m_cur)
                alpha = jnp.exp2(m_prev - m_new)
            p = jnp.exp2(s - jnp.concatenate([m_new] * (tk // LANE), axis=1))
            pv = jnp.dot(p.astype(BF16), v_ref[r0:r0 + tk, hs], preferred_element_type=F32)
            if j == 0:
                acc_ref[h] = pv
            else:
                acc_ref[h, r0:, :] = acc_ref[h, r0:, :] * alpha + pv
            if seq - r0 > tk:
                m_ref[h, r0:, :] = m_new
        outs = []
        for h in range(N_HEADS):
            a = acc_ref[h, r0:r0 + tk, :]
            outs.append(a * (1.0 / a[:, _den_lane(h):_den_lane(h) + 1]))
        o_ref[r0:r0 + tk, :] = jnp.concatenate(
            [jnp.where(low_half, outs[0], outs[1]), jnp.where(low_half, outs[2], outs[3])],
            axis=1).astype(o_ref.dtype)


def _attention(q, k, v, batch, seq):
    m = q.shape[0]
    spec = pl.BlockSpec((seq, QK_W), lambda b: (b, 0))
    return pl.pallas_call(
        _attn_kernel,
        grid=(batch,),
        in_specs=[spec, spec, spec],
        out_specs=pl.BlockSpec((seq, BRANCH_W), lambda b: (b, 0)),
        out_shape=jax.ShapeDtypeStruct((m, BRANCH_W), BF16),
        scratch_shapes=[pltpu.VMEM((N_HEADS, seq, LANE), F32),
                        pltpu.VMEM((N_HEADS, seq, LANE), F32)],
        compiler_params=_params(("arbitrary",)),
        name="attn",
    )(q, k, v)


def _s5_kernel(u_ref, bmat_ref, lam_ref, cmat_ref, d_ref, wglu_ref, bglu_ref, o_ref, st_ref):
    batch, tt, _ = u_ref.shape

    @pl.when(pl.program_id(0) == 0)
    def _():
        st_ref[...] = jnp.zeros_like(st_ref)

    u = jnp.swapaxes(u_ref[...], 0, 1).reshape(tt * batch, S5_W)
    half = tt * batch // 2
    u16 = u.astype(BF16)
    bu = jnp.concatenate([jnp.dot(u16[r0:r0 + half], bmat_ref[...], preferred_element_type=F32)
                          for r0 in (0, half)], axis=0)
    lam_re = lam_ref[0:1, :]
    lam_im = lam_ref[1:2, :]
    x_re = st_ref[:, :S5_N]
    x_im = st_ref[:, S5_N:]
    states = []
    for t in range(tt):
        r = slice(t * batch, (t + 1) * batch)
        n_re = lam_re * x_re - lam_im * x_im + bu[r, :S5_N]
        n_im = lam_re * x_im + lam_im * x_re + bu[r, S5_N:]
        x_re, x_im = n_re, n_im
        states.append(jnp.concatenate([n_re, n_im], axis=1).astype(BF16))
    st_ref[:, :S5_N] = x_re
    st_ref[:, S5_N:] = x_im

    st = jnp.concatenate(states, axis=0)
    y = jnp.concatenate([jnp.dot(st[r0:r0 + half], cmat_ref[...], preferred_element_type=F32)
                         for r0 in (0, half)], axis=0)
    y = y + d_ref[...] * u
    y = 0.5 * y * (1.0 + jnp.tanh(math.sqrt(2.0 / math.pi) * (y + 0.044715 * (y * y * y))))
    z = jnp.dot(y.astype(BF16), wglu_ref[...], preferred_element_type=F32) + bglu_ref[...]
    o = (y * jax.nn.sigmoid(z)).reshape(tt, batch, S5_W)
    o_ref[...] = jnp.swapaxes(o, 0, 1).astype(o_ref.dtype)


def _s5(u, w, batch, seq):
    consts = [w["bmat"], w["lam"], w["cmat"], w["s5_d"], w["wglu"], w["bglu"]]
    blk = pl.BlockSpec((batch, TT_S5, S5_W), lambda t: (0, t, 0))
    return pl.pallas_call(
        _s5_kernel,
        grid=(seq // TT_S5,),
        in_specs=[blk] + [_const_spec(c) for c in consts],
        out_specs=blk,
        out_shape=jax.ShapeDtypeStruct((batch, seq, S5_W), BF16),
        scratch_shapes=[pltpu.VMEM((batch, 2 * S5_N), F32)],
        compiler_params=_params(("arbitrary",)),
        name="s5",
    )(u.reshape(batch, seq, S5_W), *[c[0] for c in consts]).reshape(batch * seq, S5_W)


def _merge_kernel(x_ref, om_ref, of_ref, os_ref, g_ref, wg_ref, wbr_ref, wout_ref, o_ref):
    for r0 in range(0, x_ref.shape[0], MERGE_SUB):
        rows = slice(r0, r0 + MERGE_SUB)
        x = x_ref[rows, :]
        h = _rms(x, g_ref[...]).astype(BF16)
        merged = None
        for n, br_ref in enumerate((om_ref, of_ref, os_ref)):
            logits = jnp.dot(h, wg_ref[:, n * D_MODEL:(n + 1) * D_MODEL],
                             preferred_element_type=F32)
            proj = jnp.dot(br_ref[rows, :], wbr_ref[n], preferred_element_type=F32)
            term = jax.nn.sigmoid(logits) * proj
            merged = term if merged is None else merged + term
        o_ref[rows, :] = x + jnp.dot(merged.astype(BF16), wout_ref[...],
                                     preferred_element_type=F32)


def _merge(x, o_mla, o_fox, o_s5, w, batch, seq):
    m = x.shape[0]
    tm = TM_MERGE
    nt = seq // tm
    row = lambda b, t: (b * nt + t, 0)
    consts = [w["attn_g"], w["wg"], w["wbr"], w["wout"]]
    return pl.pallas_call(
        _merge_kernel,
        grid=(batch, nt),
        in_specs=[pl.BlockSpec((tm, D_MODEL), row)] + [pl.BlockSpec((tm, BRANCH_W), row)] * 3
                 + [_const_spec(c) for c in consts],
        out_specs=pl.BlockSpec((tm, D_MODEL), row),
        out_shape=jax.ShapeDtypeStruct((m, D_MODEL), F32),
        compiler_params=_params(("arbitrary", "arbitrary")),
        name="merge",
    )(x, o_mla, o_fox, o_s5, *[c[0] for c in consts])


FFN_CHUNK = 256


def _ffn_kernel(x_ref, g_ref, wup_ref, conv_ref, wdown_ref, o_ref, up_ref, act_ref):
    t = pl.program_id(1)
    tm = x_ref.shape[0]

    @pl.when(t == 0)
    def _():
        up_ref[0:CONV_HALO, :] = jnp.zeros((CONV_HALO, 2 * D_FF), F32)

    x = x_ref[...]
    h = _rms(x, g_ref[...]).astype(BF16)
    up_ref[CONV_HALO:CONV_HALO + tm, :] = jnp.dot(h, wup_ref[...], preferred_element_type=F32)

    def conv(cols):
        out = None
        for j in range(CONV_W):
            lo = CONV_HALO - (CONV_W - 1) + j
            term = conv_ref[j:j + 1, cols] * up_ref[lo:lo + tm, cols]
            out = term if out is None else out + term
        return out

    for c0 in range(0, D_FF, FFN_CHUNK):
        gate = conv(slice(c0, c0 + FFN_CHUNK))
        val = conv(slice(D_FF + c0, D_FF + c0 + FFN_CHUNK))
        act_ref[:, c0:c0 + FFN_CHUNK] = (gate * jax.nn.sigmoid(gate) * val).astype(BF16)

    up_ref[0:CONV_HALO, :] = up_ref[tm:tm + CONV_HALO, :]
    o_ref[...] = x + jnp.dot(act_ref[...], wdown_ref[...], preferred_element_type=F32)


def _ffn(x, w, batch, seq):
    m = x.shape[0]
    tm = TM_FFN
    nt = seq // tm
    row = lambda b, t: (b * nt + t, 0)
    consts = [w["ffn_g"], w["wup"], w["conv"], w["wdown"]]
    return pl.pallas_call(
        _ffn_kernel,
        grid=(batch, nt),
        in_specs=[pl.BlockSpec((tm, D_MODEL), row)] + [_const_spec(c) for c in consts],
        out_specs=pl.BlockSpec((tm, D_MODEL), row),
        out_shape=jax.ShapeDtypeStruct((m, D_MODEL), F32),
        scratch_shapes=[pltpu.VMEM((tm + CONV_HALO, 2 * D_FF), F32),
                        pltpu.VMEM((tm, D_FF), BF16)],
        compiler_params=_params(("arbitrary", "arbitrary")),
        name="ffn",
    )(x, *[c[0] for c in consts])


def _block_ones(n, width):
    idx = np.arange(n) // width
    return jnp.asarray((idx[:, None] == idx[None, :]).astype(np.float32), dtype=BF16)


def _pack_weights(attn_norm_g, w_in, q_lat_norm_g, w_uq, kv_lat_norm_g, w_ukv,
                  mla_q_norm_g, mla_k_norm_g, fox_q_norm_g, fox_k_norm_g, fox_f_bias,
                  s5_lambda_re, s5_lambda_im, s5_b_re, s5_b_im, s5_c_re, s5_c_im, s5_d,
                  s5_log_step, s5_w_glu, s5_b_glu, w_branch, w_out, ffn_norm_g, w_up,
                  ffn_conv_w, w_down):
    depth = w_in.shape[0]
    f32 = lambda a: a.astype(F32)
    o = np.cumsum((0, MLA_Q_RANK, MLA_KV_RANK, MLA_ROPE, BRANCH_W, BRANCH_W, BRANCH_W,
                   N_HEADS, S5_W))
    col = lambda i: w_in[:, :, o[i]:o[i + 1]]
    spans = sorted([(KR_LANE, col(2))] + [
        (_aug_lane(h), jnp.repeat(col(6)[:, :, h:h + 1], AUG_W, axis=2)) for h in range(N_HEADS)],
        key=lambda s: s[0])
    misc, lane_pos = [], 0
    for start, piece in spans + [(LANE, None)]:
        if start > lane_pos:
            misc.append(jnp.zeros((depth, D_MODEL, start - lane_pos), w_in.dtype))
        if piece is not None:
            misc.append(piece)
            lane_pos = start + piece.shape[-1]
    wa = jnp.concatenate([col(0), col(1), col(3), col(4), col(5), col(7)] + misc, axis=-1)
    wg = w_in[:, :, o[8]:]

    pad_h = lambda a: jnp.pad(a, [(0, 0)] * (a.ndim - 1) + [(0, HEAD_PAD - a.shape[-1])])
    wuq = pad_h(w_uq).reshape(depth, MLA_Q_RANK, QK_W)
    wukv = jnp.concatenate(
        [pad_h(w_ukv[..., :MLA_NOPE]).reshape(depth, MLA_KV_RANK, QK_W),
         w_ukv[..., MLA_NOPE:].reshape(depth, MLA_KV_RANK, BRANCH_W)], axis=-1)
    tile_h = lambda g: jnp.tile(g, (1, N_HEADS)).reshape(depth, 1, -1)
    gq = tile_h(pad_h(f32(mla_q_norm_g))) * (MLA_QK ** -0.5 * LOG2E)
    gk = tile_h(pad_h(f32(mla_k_norm_g)))
    gfq = tile_h(f32(fox_q_norm_g)) * (FOX_HD ** -0.5 * LOG2E)
    gfk = tile_h(f32(fox_k_norm_g))
    fbias = jnp.zeros((depth, LANE), F32)
    augm = np.zeros((8, LANE), np.float32)
    for h in range(N_HEADS):
        a = _aug_lane(h)
        fbias = fbias.at[:, a:a + AUG_W].set(f32(fox_f_bias)[:, h:h + 1])
        for j in range(3):
            augm[j, a + j] = 1.0
            augm[3, a + 3 + j] = 1.0
            augm[4, a + j] = 1.0
            augm[5 + j, a + 3 + j] = 1.0
    fbias = fbias.reshape(depth, 1, LANE)

    lam_re, lam_im = f32(s5_lambda_re), f32(s5_lambda_im)
    step = jnp.exp(f32(s5_log_step))[..., None]
    mag = jnp.exp(lam_re * step)
    a_re, a_im = mag * jnp.cos(lam_im * step), mag * jnp.sin(lam_im * step)
    den = lam_re * lam_re + lam_im * lam_im
    k_re = ((a_re - 1.0) * lam_re + a_im * lam_im) / den
    k_im = (a_im * lam_re - (a_re - 1.0) * lam_im) / den
    b_re, b_im = f32(s5_b_re), f32(s5_b_im)
    bb_re = k_re[..., None] * b_re - k_im[..., None] * b_im
    bb_im = k_re[..., None] * b_im + k_im[..., None] * b_re
    eye = jnp.eye(S5_G, dtype=F32)
    bd_in = lambda a: jnp.einsum("lgph,gk->lghkp", a, eye).reshape(depth, S5_W, S5_N)
    bmat = jnp.concatenate([bd_in(bb_re), bd_in(bb_im)], axis=-1)
    bd_out = lambda a: jnp.einsum("lghp,gk->lgpkh", a, eye).reshape(depth, S5_N, S5_W)
    cmat = jnp.concatenate([bd_out(f32(s5_c_re)), -bd_out(f32(s5_c_im))], axis=1)
    lam = jnp.stack([a_re.reshape(depth, S5_N), a_im.reshape(depth, S5_N)], axis=1)

    bf = lambda a: a.astype(BF16)
    row = lambda a: f32(a).reshape(depth, 1, -1)
    tri = jnp.asarray(np.tril(np.ones((CUMSUM_BLOCK, CUMSUM_BLOCK), np.float32)), dtype=BF16)
    shared = {"ones_qk": _block_ones(2 * LANE, HEAD_PAD), "ones_fox": _block_ones(BRANCH_W, FOX_HD),
              "tri": tri, "augm": jnp.asarray(augm)}
    stacked = {
        "attn_g": row(attn_norm_g), "wa": bf(wa), "wg": bf(wg), "fbias": fbias,
        "qlat_g": row(q_lat_norm_g), "wuq": bf(wuq), "kvlat_g": row(kv_lat_norm_g),
        "wukv": bf(wukv), "gq": gq, "gk": gk, "gfq": gfq, "gfk": gfk,
        "bmat": bf(bmat), "lam": lam, "cmat": bf(cmat), "s5_d": row(s5_d),
        "wglu": bf(s5_w_glu), "bglu": row(s5_b_glu),
        "wbr": bf(w_branch), "wout": bf(w_out),
        "ffn_g": row(ffn_norm_g), "wup": bf(w_up), "conv": f32(ffn_conv_w), "wdown": bf(w_down),
    }
    return [dict({k: (v, None) for k, v in shared.items()},
                 **{k: (v, l) for k, v in stacked.items()}) for l in range(depth)]


def kernel(x, positions, attn_norm_g, w_in, q_lat_norm_g, w_uq, kv_lat_norm_g, w_ukv, mla_q_norm_g, mla_k_norm_g, fox_q_norm_g, fox_k_norm_g, fox_f_bias, s5_lambda_re, s5_lambda_im, s5_b_re, s5_b_im, s5_c_re, s5_c_im, s5_d, s5_log_step, s5_w_glu, s5_b_glu, w_branch, w_out, ffn_norm_g, w_up, ffn_conv_w, w_down):
    batch, seq, d_model = x.shape
    assert d_model == D_MODEL and seq % TM_PROJ == 0 and seq % TK == 0 and seq % TT_S5 == 0
    assert batch % 8 == 0
    layers = _pack_weights(attn_norm_g, w_in, q_lat_norm_g, w_uq, kv_lat_norm_g, w_ukv,
                           mla_q_norm_g, mla_k_norm_g, fox_q_norm_g, fox_k_norm_g, fox_f_bias,
                           s5_lambda_re, s5_lambda_im, s5_b_re, s5_b_im, s5_c_re, s5_c_im, s5_d,
                           s5_log_step, s5_w_glu, s5_b_glu, w_branch, w_out, ffn_norm_g, w_up,
                           ffn_conv_w, w_down)
    cos, sin = _rope_tables(positions)
    xf = x.astype(F32).reshape(batch * seq, D_MODEL)
    for w in layers:
        qm, km, vm, qf, kf, vf, u = _proj(xf, cos, sin, w, batch, seq)
        o_mla = _attention(qm, km, vm, batch, seq)
        o_fox = _attention(qf, kf, vf, batch, seq)
        o_s5 = _s5(u, w, batch, seq)
        xf = _merge(xf, o_mla, o_fox, o_s5, w, batch, seq)
        xf = _ffn(xf, w, batch, seq)
    return xf.reshape(batch, seq, D_MODEL).astype(x.dtype)
```

```python
import math

import numpy as np
import jax
import jax.numpy as jnp
from jax import lax
from jax.experimental import pallas as pl
from jax.experimental.pallas import tpu as pltpu

F32 = jnp.float32
BF16 = jnp.bfloat16

D_MODEL = 1024
N_HEADS = 4
MLA_NOPE = 64
MLA_ROPE = 32
MLA_QK = MLA_NOPE + MLA_ROPE
MLA_V = 64
MLA_Q_RANK = 384
MLA_KV_RANK = 256
FOX_HD = 64
S5_G = 16
S5_H = 16
S5_P = 64
S5_W = S5_G * S5_H
S5_N = S5_G * S5_P
BRANCH_W = 256
N_BRANCH = 3
D_FF = 2816
CONV_W = 3
ROPE_THETA = 10000.0
EPS = 1e-6
NEG = -1e30
LOG2E = math.log2(math.e)

LANE = 128
HEAD_PAD = 128
QK_W = N_HEADS * HEAD_PAD
C_CQ, C_CKV, C_FQ, C_FK, C_FV, C_U, C_MISC = 0, 384, 640, 896, 1152, 1408, 1664
WA_COLS = 1792
KR_LANE = MLA_NOPE
AUG_W = 6

VMEM_LIMIT = 56 * 1024 * 1024

TM_PROJ = 1024
PROJ_SUB = 512
CUMSUM_BLOCK = 256
TK = 512
TT_S5 = 64
TM_MERGE = 1024
MERGE_SUB = 256
TM_FFN = 512
CONV_HALO = 8


def _const_spec(param):
    arr, layer = param
    if layer is None:
        return pl.BlockSpec(arr.shape, lambda *_: (0,) * arr.ndim, pipeline_mode=pl.Buffered(1))
    zeros = (0,) * (arr.ndim - 1)
    return pl.BlockSpec((None,) + arr.shape[1:], lambda *_: (layer,) + zeros,
                        pipeline_mode=pl.Buffered(1))


def _params(sem):
    return pltpu.CompilerParams(dimension_semantics=sem, vmem_limit_bytes=VMEM_LIMIT)


def _rms(x, gain):
    ms = jnp.mean(x * x, axis=-1, keepdims=True)
    return x * lax.rsqrt(ms + EPS) * gain


def _split3(x):
    hi = x.astype(BF16)
    r = x - hi.astype(F32)
    mid = r.astype(BF16)
    lo = (r - mid.astype(F32)).astype(BF16)
    return hi, mid, lo


def _head_sums(sq, ones_ref):
    sq = sq.astype(BF16)
    ones = ones_ref[...]
    outs = [jnp.dot(sq[:, c0:c0 + 2 * LANE], ones, preferred_element_type=F32)
            for c0 in range(0, sq.shape[1], 2 * LANE)]
    return outs[0] if len(outs) == 1 else jnp.concatenate(outs, axis=1)


def _aug_lane(h):
    return MLA_QK + AUG_W * h if h % 2 == 0 else AUG_W * (h + 1)


def _den_lane(h):
    return MLA_V if h % 2 == 0 else 0


def _pad_value_heads(v):
    lane = lax.broadcasted_iota(jnp.int32, (v.shape[0], LANE), 1)
    tiles = []
    for h in range(N_HEADS):
        pair = v[:, (h // 2) * LANE:(h // 2 + 1) * LANE]
        mine = lane < MLA_V if h % 2 == 0 else lane >= MLA_V
        tiles.append(jnp.where(mine, pair, jnp.where(lane == _den_lane(h), 1.0, 0.0)))
    return jnp.concatenate(tiles, axis=1)


def _rope_table_kernel(pos_ref, inv_ref, sign_ref, cos_ref, sin_ref):
    ang = pos_ref[...] * inv_ref[...]
    cos_ref[...] = jnp.cos(ang)
    sin_ref[...] = jnp.sin(ang) * sign_ref[...]


def _rope_tables(positions):
    m = positions.size
    tm = 1024
    pos = positions.astype(F32).reshape(m, 1)
    inv_freq = ROPE_THETA ** (-jnp.arange(0, MLA_ROPE, 2, dtype=F32) / MLA_ROPE)
    half = MLA_ROPE // 2
    inv_lane = jnp.zeros((LANE,), F32)
    inv_lane = inv_lane.at[MLA_NOPE:MLA_NOPE + half].set(inv_freq)
    inv_lane = inv_lane.at[MLA_NOPE + half:MLA_QK].set(inv_freq)
    sign = np.zeros((LANE,), np.float32)
    sign[MLA_NOPE:MLA_NOPE + half] = -1.0
    sign[MLA_NOPE + half:MLA_QK] = 1.0
    return pl.pallas_call(
        _rope_table_kernel,
        grid=(m // tm,),
        in_specs=[pl.BlockSpec((tm, 1), lambda i: (i, 0)),
                  pl.BlockSpec((1, LANE), lambda i: (0, 0)),
                  pl.BlockSpec((1, LANE), lambda i: (0, 0))],
        out_specs=[pl.BlockSpec((tm, LANE), lambda i: (i, 0))] * 2,
        out_shape=[jax.ShapeDtypeStruct((m, LANE), F32)] * 2,
        compiler_params=_params(("arbitrary",)),
        name="rope_tables",
    )(pos, inv_lane.reshape(1, LANE), jnp.asarray(sign).reshape(1, LANE))


def _proj_kernel(x_ref, *refs):
    carry_ref = refs[-1]

    @pl.when(pl.program_id(1) == 0)
    def _():
        carry_ref[...] = jnp.zeros_like(carry_ref)

    for r0 in range(0, x_ref.shape[0], PROJ_SUB):
        _proj_rows(slice(r0, r0 + PROJ_SUB), x_ref, *refs)


def _proj_rows(rows, x_ref, cos_ref, sin_ref, g_ref, wa_ref, fbias_ref, augm_ref,
               qlat_g_ref, wuq_ref, kvlat_g_ref, wukv_ref, gq_ref, gk_ref, ones_qk_ref,
               gfq_ref, gfk_ref, ones_fox_ref, tri_ref,
               qm_ref, km_ref, vm_ref, qf_ref, kf_ref, vf_ref, u_ref,
               carry_ref):
    tm = PROJ_SUB
    h = _rms(x_ref[rows, :], g_ref[...]).astype(BF16)
    p = jnp.dot(h, wa_ref[...], preferred_element_type=F32)
    misc = p[:, C_MISC:C_MISC + LANE]
    lane = lax.broadcasted_iota(jnp.int32, (tm, LANE), 1)
    cos = cos_ref[rows, :]
    sin = sin_ref[rows, :]

    cq = _rms(p[:, C_CQ:C_CQ + MLA_Q_RANK], qlat_g_ref[...]).astype(BF16)
    q = jnp.dot(cq, wuq_ref[...], preferred_element_type=F32)
    cos4 = jnp.concatenate([cos] * N_HEADS, axis=1)
    sin4 = jnp.concatenate([sin] * N_HEADS, axis=1)
    first_half = lane < (MLA_NOPE + MLA_ROPE // 2)
    r = lax.rsqrt(_head_sums(q * q, ones_qk_ref) * (1.0 / MLA_QK) + EPS)
    qg = q * gq_ref[...]
    partners = []
    for hd in range(N_HEADS):
        blk = qg[:, hd * HEAD_PAD:(hd + 1) * HEAD_PAD]
        partners.append(jnp.where(first_half, pltpu.roll(blk, HEAD_PAD - MLA_ROPE // 2, 1),
                                  pltpu.roll(blk, MLA_ROPE // 2, 1)))
    qm_ref[rows, :] = (r * (qg * cos4 + jnp.concatenate(partners, axis=1) * sin4)).astype(BF16)

    ckv = _rms(p[:, C_CKV:C_CKV + MLA_KV_RANK], kvlat_g_ref[...]).astype(BF16)
    kv = jnp.dot(ckv, wukv_ref[...], preferred_element_type=F32)
    k_rope = jnp.where((lane >= KR_LANE) & (lane < MLA_QK), misc, 0.0)
    k = kv[:, :QK_W] + jnp.concatenate([k_rope] * N_HEADS, axis=1)
    r = lax.rsqrt(_head_sums(k * k, ones_qk_ref) * (1.0 / MLA_QK) + EPS)
    krg = k_rope * gk_ref[:, :LANE]
    partner = jnp.where(first_half, pltpu.roll(krg, HEAD_PAD - MLA_ROPE // 2, 1),
                        pltpu.roll(krg, MLA_ROPE // 2, 1)) * sin
    km_ref[rows, :] = (r * (k * gk_ref[...] * cos4
                            + jnp.concatenate([partner] * N_HEADS, axis=1))).astype(BF16)
    vm_ref[rows, :] = _pad_value_heads(kv[:, QK_W:]).astype(BF16)

    z = misc + fbias_ref[...]
    log_f = jnp.minimum(z, 0.0) - jnp.log1p(jnp.exp(-jnp.abs(z)))
    tri = tri_ref[...]
    parts = _split3(log_f)
    carry = carry_ref[0:1, :]
    c_blocks = []
    for r0 in range(0, tm, CUMSUM_BLOCK):
        cb = carry
        for part in parts:
            cb = cb + jnp.dot(tri, part[r0:r0 + CUMSUM_BLOCK], preferred_element_type=F32)
        carry = cb[CUMSUM_BLOCK - 1:CUMSUM_BLOCK, :]
        c_blocks.append(cb)
    carry_ref[0:1, :] = carry
    c = jnp.concatenate(c_blocks, axis=0) * LOG2E

    fq = p[:, C_FQ:C_FQ + BRANCH_W]
    fq = fq * lax.rsqrt(_head_sums(fq * fq, ones_fox_ref) * (1.0 / FOX_HD) + EPS) * gfq_ref[...]
    fk = p[:, C_FK:C_FK + BRANCH_W]
    fk = fk * lax.rsqrt(_head_sums(fk * fk, ones_fox_ref) * (1.0 / FOX_HD) + EPS) * gfk_ref[...]
    hi = c.astype(BF16).astype(F32)
    rem = c - hi
    mid = rem.astype(BF16).astype(F32)
    lo = rem - mid
    aug_q = hi * augm_ref[0:1, :] + mid * augm_ref[1:2, :] + lo * augm_ref[2:3, :] + augm_ref[3:4, :]
    aug_k = augm_ref[4:5, :] - hi * augm_ref[5:6, :] - mid * augm_ref[6:7, :] - lo * augm_ref[7:8, :]
    q_tiles, k_tiles = [], []
    for hd in range(N_HEADS):
        pair = slice((hd // 2) * LANE, (hd // 2 + 1) * LANE)
        data = lane < FOX_HD if hd % 2 == 0 else lane >= FOX_HD
        in_aug = (lane >= _aug_lane(hd)) & (lane < _aug_lane(hd) + AUG_W)
        q_tiles.append(jnp.where(data, fq[:, pair], jnp.where(in_aug, aug_q, 0.0)))
        k_tiles.append(jnp.where(data, fk[:, pair], jnp.where(in_aug, aug_k, 0.0)))
    qf_ref[rows, :] = jnp.concatenate(q_tiles, axis=1).astype(BF16)
    kf_ref[rows, :] = jnp.concatenate(k_tiles, axis=1).astype(BF16)
    vf_ref[rows, :] = _pad_value_heads(p[:, C_FV:C_FV + BRANCH_W]).astype(BF16)

    u_ref[rows, :] = p[:, C_U:C_U + S5_W]


def _proj(x, cos, sin, w, batch, seq):
    m = x.shape[0]
    tm = TM_PROJ
    nt = seq // tm
    row = lambda b, t: (b * nt + t, 0)
    consts = [w["attn_g"], w["wa"], w["fbias"], w["augm"], w["qlat_g"], w["wuq"], w["kvlat_g"], w["wukv"],
              w["gq"], w["gk"], w["ones_qk"], w["gfq"], w["gfk"], w["ones_fox"], w["tri"]]
    out_shape = [jax.ShapeDtypeStruct((m, QK_W), BF16)] * 6 + [
        jax.ShapeDtypeStruct((m, S5_W), F32)]
    out_specs = [pl.BlockSpec((tm, QK_W), row)] * 6 + [pl.BlockSpec((tm, S5_W), row)]
    return pl.pallas_call(
        _proj_kernel,
        grid=(batch, nt),
        in_specs=[pl.BlockSpec((tm, D_MODEL), row), pl.BlockSpec((tm, LANE), row),
                  pl.BlockSpec((tm, LANE), row)] + [_const_spec(c) for c in consts],
        out_specs=out_specs,
        out_shape=out_shape,
        scratch_shapes=[pltpu.VMEM((8, LANE), F32)],
        compiler_params=_params(("arbitrary", "arbitrary")),
        name="proj",
    )(x, cos, sin, *[c[0] for c in consts])


def _attn_kernel(q_ref, k_ref, v_ref, o_ref, acc_ref, m_ref):
    seq = q_ref.shape[0]
    tk = TK
    causal = (lax.broadcasted_iota(jnp.int32, (tk, tk), 0)
              >= lax.broadcasted_iota(jnp.int32, (tk, tk), 1))
    low_half = lax.broadcasted_iota(jnp.int32, (tk, LANE), 1) < MLA_V

    for j in range(seq // tk):
        r0 = j * tk
        for h in range(N_HEADS):
            hs = slice(h * HEAD_PAD, (h + 1) * HEAD_PAD)
            s = lax.dot_general(q_ref[r0:, hs], k_ref[r0:r0 + tk, hs],
                                (((1,), (1,)), ((), ())), preferred_element_type=F32)
            top = jnp.where(causal, s[:tk], NEG)
            s = top if seq - r0 == tk else jnp.concatenate([top, s[tk:]], axis=0)
            m_cur = jnp.max(s, axis=1, keepdims=True)
            if j == 0:
                m_new = jnp.broadcast_to(m_cur, (seq, LANE))
            else:
                m_prev = m_ref[h, r0:, :]
                m_new = jnp.maximum(m_prev, m_cur)
                alpha = jnp.exp2(m_prev - m_new)
            p = jnp.exp2(s - jnp.concatenate([m_new] * (tk // LANE), axis=1))
            pv = jnp.dot(p.astype(BF16), v_ref[r0:r0 + tk, hs], preferred_element_type=F32)
            if j == 0:
                acc_ref[h] = pv
            else:
                acc_ref[h, r0:, :] = acc_ref[h, r0:, :] * alpha + pv
            if seq - r0 > tk:
                m_ref[h, r0:, :] = m_new
        outs = []
        for h in range(N_HEADS):
            a = acc_ref[h, r0:r0 + tk, :]
            outs.append(a * (1.0 / a[:, _den_lane(h):_den_lane(h) + 1]))
        o_ref[r0:r0 + tk, :] = jnp.concatenate(
            [jnp.where(low_half, outs[0], outs[1]), jnp.where(low_half, outs[2], outs[3])],
            axis=1).astype(o_ref.dtype)


def _attention(q, k, v, batch, seq):
    m = q.shape[0]
    spec = pl.BlockSpec((seq, QK_W), lambda b: (b, 0))
    return pl.pallas_call(
        _attn_kernel,
        grid=(batch,),
        in_specs=[spec, spec, spec],
        out_specs=pl.BlockSpec((seq, BRANCH_W), lambda b: (b, 0)),
        out_shape=jax.ShapeDtypeStruct((m, BRANCH_W), BF16),
        scratch_shapes=[pltpu.VMEM((N_HEADS, seq, LANE), F32),
                        pltpu.VMEM((N_HEADS, seq, LANE), F32)],
        compiler_params=_params(("arbitrary",)),
        name="attn",
    )(q, k, v)


def _s5_kernel(u_ref, bmat_ref, lam_ref, cmat_ref, d_ref, wglu_ref, bglu_ref, o_ref, st_ref):
    batch, tt, _ = u_ref.shape

    @pl.when(pl.program_id(0) == 0)
    def _():
        st_ref[...] = jnp.zeros_like(st_ref)

    u = jnp.swapaxes(u_ref[...], 0, 1).reshape(tt * batch, S5_W)
    half = tt * batch // 2
    u16 = u.astype(BF16)
    bu = jnp.concatenate([jnp.dot(u16[r0:r0 + half], bmat_ref[...], preferred_element_type=F32)
                          for r0 in (0, half)], axis=0)
    lam_re = lam_ref[0:1, :]
    lam_im = lam_ref[1:2, :]
    x_re = st_ref[:, :S5_N]
    x_im = st_ref[:, S5_N:]
    states = []
    for t in range(tt):
        r = slice(t * batch, (t + 1) * batch)
        n_re = lam_re * x_re - lam_im * x_im + bu[r, :S5_N]
        n_im = lam_re * x_im + lam_im * x_re + bu[r, S5_N:]
        x_re, x_im = n_re, n_im
        states.append(jnp.concatenate([n_re, n_im], axis=1).astype(BF16))
    st_ref[:, :S5_N] = x_re
    st_ref[:, S5_N:] = x_im

    st = jnp.concatenate(states, axis=0)
    y = jnp.concatenate([jnp.dot(st[r0:r0 + half], cmat_ref[...], preferred_element_type=F32)
                         for r0 in (0, half)], axis=0)
    y = y + d_ref[...] * u
    y = 0.5 * y * (1.0 + jnp.tanh(math.sqrt(2.0 / math.pi) * (y + 0.044715 * (y * y * y))))
    z = jnp.dot(y.astype(BF16), wglu_ref[...], preferred_element_type=F32) + bglu_ref[...]
    o = (y * jax.nn.sigmoid(z)).reshape(tt, batch, S5_W)
    o_ref[...] = jnp.swapaxes(o, 0, 1).astype(o_ref.dtype)


def _s5(u, w, batch, seq):
    consts = [w["bmat"], w["lam"], w["cmat"], w["s5_d"], w["wglu"], w["bglu"]]
    blk = pl.BlockSpec((batch, TT_S5, S5_W), lambda t: (0, t, 0))
    return pl.pallas_call(
        _s5_kernel,
        grid=(seq // TT_S5,),
        in_specs=[blk] + [_const_spec(c) for c in consts],
        out_specs=blk,
        out_shape=jax.ShapeDtypeStruct((batch, seq, S5_W), BF16),
        scratch_shapes=[pltpu.VMEM((batch, 2 * S5_N), F32)],
        compiler_params=_params(("arbitrary",)),
        name="s5",
    )(u.reshape(batch, seq, S5_W), *[c[0] for c in consts]).reshape(batch * seq, S5_W)


def _merge_kernel(x_ref, om_ref, of_ref, os_ref, g_ref, wg_ref, wbr_ref, wout_ref, o_ref):
    for r0 in range(0, x_ref.shape[0], MERGE_SUB):
        rows = slice(r0, r0 + MERGE_SUB)
        x = x_ref[rows, :]
        h = _rms(x, g_ref[...]).astype(BF16)
        merged = None
        for n, br_ref in enumerate((om_ref, of_ref, os_ref)):
            logits = jnp.dot(h, wg_ref[:, n * D_MODEL:(n + 1) * D_MODEL],
                             preferred_element_type=F32)
            proj = jnp.dot(br_ref[rows, :], wbr_ref[n], preferred_element_type=F32)
            term = jax.nn.sigmoid(logits) * proj
            merged = term if merged is None else merged + term
        o_ref[rows, :] = x + jnp.dot(merged.astype(BF16), wout_ref[...],
                                     preferred_element_type=F32)


def _merge(x, o_mla, o_fox, o_s5, w, batch, seq):
    m = x.shape[0]
    tm = TM_MERGE
    nt = seq // tm
    row = lambda b, t: (b * nt + t, 0)
    consts = [w["attn_g"], w["wg"], w["wbr"], w["wout"]]
    return pl.pallas_call(
        _merge_kernel,
        grid=(batch, nt),
        in_specs=[pl.BlockSpec((tm, D_MODEL), row)] + [pl.BlockSpec((tm, BRANCH_W), row)] * 3
                 + [_const_spec(c) for c in consts],
        out_specs=pl.BlockSpec((tm, D_MODEL), row),
        out_shape=jax.ShapeDtypeStruct((m, D_MODEL), F32),
        compiler_params=_params(("arbitrary", "arbitrary")),
        name="merge",
    )(x, o_mla, o_fox, o_s5, *[c[0] for c in consts])


FFN_CHUNK = 256
FFN_DOWN_CUTS = (0, 1024, 2048, D_FF)


def _ffn_kernel(x_ref, g_ref, wup_ref, conv_ref, wdown_ref, o_ref, up_ref, act_ref):
    t = pl.program_id(1)
    tm = x_ref.shape[0]

    @pl.when(t == 0)
    def _():
        up_ref[0:CONV_HALO, :] = jnp.zeros((CONV_HALO, 2 * D_FF), F32)

    x = x_ref[...]
    h = _rms(x, g_ref[...]).astype(BF16)
    up_ref[CONV_HALO:CONV_HALO + tm, :] = jnp.dot(h, wup_ref[...], preferred_element_type=F32)

    def conv(cols):
        out = None
        for j in range(CONV_W):
            lo = CONV_HALO - (CONV_W - 1) + j
            term = conv_ref[j:j + 1, cols] * up_ref[lo:lo + tm, cols]
            out = term if out is None else out + term
        return out

    out = x
    for c0 in range(0, D_FF, FFN_CHUNK):
        gate = conv(slice(c0, c0 + FFN_CHUNK))
        val = conv(slice(D_FF + c0, D_FF + c0 + FFN_CHUNK))
        act_ref[:, c0:c0 + FFN_CHUNK] = (gate * jax.nn.sigmoid(gate) * val).astype(BF16)
        c1 = c0 + FFN_CHUNK
        if c1 in FFN_DOWN_CUTS:
            k0 = FFN_DOWN_CUTS[FFN_DOWN_CUTS.index(c1) - 1]
            out = out + jnp.dot(act_ref[:, k0:c1], wdown_ref[k0:c1, :],
                                preferred_element_type=F32)

    up_ref[0:CONV_HALO, :] = up_ref[tm:tm + CONV_HALO, :]
    o_ref[...] = out


def _ffn(x, w, batch, seq):
    m = x.shape[0]
    tm = TM_FFN
    nt = seq // tm
    row = lambda b, t: (b * nt + t, 0)
    consts = [w["ffn_g"], w["wup"], w["conv"], w["wdown"]]
    return pl.pallas_call(
        _ffn_kernel,
        grid=(batch, nt),
        in_specs=[pl.BlockSpec((tm, D_MODEL), row)] + [_const_spec(c) for c in consts],
        out_specs=pl.BlockSpec((tm, D_MODEL), row),
        out_shape=jax.ShapeDtypeStruct((m, D_MODEL), F32),
        scratch_shapes=[pltpu.VMEM((tm + CONV_HALO, 2 * D_FF), F32),
                        pltpu.VMEM((tm, D_FF), BF16)],
        compiler_params=_params(("arbitrary", "arbitrary")),
        name="ffn",
    )(x, *[c[0] for c in consts])


def _block_ones(n, width):
    idx = np.arange(n) // width
    return jnp.asarray((idx[:, None] == idx[None, :]).astype(np.float32), dtype=BF16)


def _pack_weights(attn_norm_g, w_in, q_lat_norm_g, w_uq, kv_lat_norm_g, w_ukv,
                  mla_q_norm_g, mla_k_norm_g, fox_q_norm_g, fox_k_norm_g, fox_f_bias,
                  s5_lambda_re, s5_lambda_im, s5_b_re, s5_b_im, s5_c_re, s5_c_im, s5_d,
                  s5_log_step, s5_w_glu, s5_b_glu, w_branch, w_out, ffn_norm_g, w_up,
                  ffn_conv_w, w_down):
    depth = w_in.shape[0]
    f32 = lambda a: a.astype(F32)
    o = np.cumsum((0, MLA_Q_RANK, MLA_KV_RANK, MLA_ROPE, BRANCH_W, BRANCH_W, BRANCH_W,
                   N_HEADS, S5_W))
    col = lambda i: w_in[:, :, o[i]:o[i + 1]]
    spans = sorted([(KR_LANE, col(2))] + [
        (_aug_lane(h), jnp.repeat(col(6)[:, :, h:h + 1], AUG_W, axis=2)) for h in range(N_HEADS)],
        key=lambda s: s[0])
    misc, lane_pos = [], 0
    for start, piece in spans + [(LANE, None)]:
        if start > lane_pos:
            misc.append(jnp.zeros((depth, D_MODEL, start - lane_pos), w_in.dtype))
        if piece is not None:
            misc.append(piece)
            lane_pos = start + piece.shape[-1]
    wa = jnp.concatenate([col(0), col(1), col(3), col(4), col(5), col(7)] + misc, axis=-1)
    wg = w_in[:, :, o[8]:]

    pad_h = lambda a: jnp.pad(a, [(0, 0)] * (a.ndim - 1) + [(0, HEAD_PAD - a.shape[-1])])
    wuq = pad_h(w_uq).reshape(depth, MLA_Q_RANK, QK_W)
    wukv = jnp.concatenate(
        [pad_h(w_ukv[..., :MLA_NOPE]).reshape(depth, MLA_KV_RANK, QK_W),
         w_ukv[..., MLA_NOPE:].reshape(depth, MLA_KV_RANK, BRANCH_W)], axis=-1)
    tile_h = lambda g: jnp.tile(g, (1, N_HEADS)).reshape(depth, 1, -1)
    gq = tile_h(pad_h(f32(mla_q_norm_g))) * (MLA_QK ** -0.5 * LOG2E)
    gk = tile_h(pad_h(f32(mla_k_norm_g)))
    gfq = tile_h(f32(fox_q_norm_g)) * (FOX_HD ** -0.5 * LOG2E)
    gfk = tile_h(f32(fox_k_norm_g))
    fbias = jnp.zeros((depth, LANE), F32)
    augm = np.zeros((8, LANE), np.float32)
    for h in range(N_HEADS):
        a = _aug_lane(h)
        fbias = fbias.at[:, a:a + AUG_W].set(f32(fox_f_bias)[:, h:h + 1])
        for j in range(3):
            augm[j, a + j] = 1.0
            augm[3, a + 3 + j] = 1.0
            augm[4, a + j] = 1.0
            augm[5 + j, a + 3 + j] = 1.0
    fbias = fbias.reshape(depth, 1, LANE)

    lam_re, lam_im = f32(s5_lambda_re), f32(s5_lambda_im)
    step = jnp.exp(f32(s5_log_step))[..., None]
    mag = jnp.exp(lam_re * step)
    a_re, a_im = mag * jnp.cos(lam_im * step), mag * jnp.sin(lam_im * step)
    den = lam_re * lam_re + lam_im * lam_im
    k_re = ((a_re - 1.0) * lam_re + a_im * lam_im) / den
    k_im = (a_im * lam_re - (a_re - 1.0) * lam_im) / den
    b_re, b_im = f32(s5_b_re), f32(s5_b_im)
    bb_re = k_re[..., None] * b_re - k_im[..., None] * b_im
    bb_im = k_re[..., None] * b_im + k_im[..., None] * b_re
    eye = jnp.eye(S5_G, dtype=F32)
    bd_in = lambda a: jnp.einsum("lgph,gk->lghkp", a, eye).reshape(depth, S5_W, S5_N)
    bmat = jnp.concatenate([bd_in(bb_re), bd_in(bb_im)], axis=-1)
    bd_out = lambda a: jnp.einsum("lghp,gk->lgpkh", a, eye).reshape(depth, S5_N, S5_W)
    cmat = jnp.concatenate([bd_out(f32(s5_c_re)), -bd_out(f32(s5_c_im))], axis=1)
    lam = jnp.stack([a_re.reshape(depth, S5_N), a_im.reshape(depth, S5_N)], axis=1)

    bf = lambda a: a.astype(BF16)
    row = lambda a: f32(a).reshape(depth, 1, -1)
    tri = jnp.asarray(np.tril(np.ones((CUMSUM_BLOCK, CUMSUM_BLOCK), np.float32)), dtype=BF16)
    shared = {"ones_qk": _block_ones(2 * LANE, HEAD_PAD), "ones_fox": _block_ones(BRANCH_W, FOX_HD),
              "tri": tri, "augm": jnp.asarray(augm)}
    stacked = {
        "attn_g": row(attn_norm_g), "wa": bf(wa), "wg": bf(wg), "fbias": fbias,
        "qlat_g": row(q_lat_norm_g), "wuq": bf(wuq), "kvlat_g": row(kv_lat_norm_g),
        "wukv": bf(wukv), "gq": gq, "gk": gk, "gfq": gfq, "gfk": gfk,
        "bmat": bf(bmat), "lam": lam, "cmat": bf(cmat), "s5_d": row(s5_d),
        "wglu": bf(s5_w_glu), "bglu": row(s5_b_glu),
        "wbr": bf(w_branch), "wout": bf(w_out),
        "ffn_g": row(ffn_norm_g), "wup": bf(w_up), "conv": f32(ffn_conv_w), "wdown": bf(w_down),
    }
    return [dict({k: (v, None) for k, v in shared.items()},
                 **{k: (v, l) for k, v in stacked.items()}) for l in range(depth)]


def kernel(x, positions, attn_norm_g, w_in, q_lat_norm_g, w_uq, kv_lat_norm_g, w_ukv, mla_q_norm_g, mla_k_norm_g, fox_q_norm_g, fox_k_norm_g, fox_f_bias, s5_lambda_re, s5_lambda_im, s5_b_re, s5_b_im, s5_c_re, s5_c_im, s5_d, s5_log_step, s5_w_glu, s5_b_glu, w_branch, w_out, ffn_norm_g, w_up, ffn_conv_w, w_down):
    batch, seq, d_model = x.shape
    assert d_model == D_MODEL and seq % TM_PROJ == 0 and seq % TK == 0 and seq % TT_S5 == 0
    assert batch % 8 == 0
    layers = _pack_weights(attn_norm_g, w_in, q_lat_norm_g, w_uq, kv_lat_norm_g, w_ukv,
                           mla_q_norm_g, mla_k_norm_g, fox_q_norm_g, fox_k_norm_g, fox_f_bias,
                           s5_lambda_re, s5_lambda_im, s5_b_re, s5_b_im, s5_c_re, s5_c_im, s5_d,
                           s5_log_step, s5_w_glu, s5_b_glu, w_branch, w_out, ffn_norm_g, w_up,
                           ffn_conv_w, w_down)
    cos, sin = _rope_tables(positions)
    xf = x.astype(F32).reshape(batch * seq, D_MODEL)
    for w in layers:
        qm, km, vm, qf, kf, vf, u = _proj(xf, cos, sin, w, batch, seq)
        o_mla = _attention(qm, km, vm, batch, seq)
        o_fox = _attention(qf, kf, vf, batch, seq)
        o_s5 = _s5(u, w, batch, seq)
        xf = _merge(xf, o_mla, o_fox, o_s5, w, batch, seq)
        xf = _ffn(xf, w, batch, seq)
    return xf.reshape(batch, seq, D_MODEL).astype(x.dtype)
```

```python
import math

import numpy as np
import jax
import jax.numpy as jnp
from jax import lax
from jax.experimental import pallas as pl
from jax.experimental.pallas import tpu as pltpu

F32 = jnp.float32
BF16 = jnp.bfloat16

D_MODEL = 1024
N_HEADS = 4
MLA_NOPE = 64
MLA_ROPE = 32
MLA_QK = MLA_NOPE + MLA_ROPE
MLA_V = 64
MLA_Q_RANK = 384
MLA_KV_RANK = 256
FOX_HD = 64
S5_G = 16
S5_H = 16
S5_P = 64
S5_W = S5_G * S5_H
S5_N = S5_G * S5_P
BRANCH_W = 256
N_BRANCH = 3
D_FF = 2816
CONV_W = 3
ROPE_THETA = 10000.0
EPS = 1e-6
NEG = -1e30
LOG2E = math.log2(math.e)

LANE = 128
HEAD_PAD = 128
QK_W = N_HEADS * HEAD_PAD
C_CQ, C_CKV, C_FQ, C_FK, C_FV, C_U, C_MISC = 0, 384, 640, 896, 1152, 1408, 1664
WA_COLS = 1792
KR_LANE = MLA_NOPE
AUG_W = 6

VMEM_LIMIT = 56 * 1024 * 1024

TM_PROJ = 1024
PROJ_SUB = 512
CUMSUM_BLOCK = 256
TK = 512
TT_S5 = 64
TM_MERGE = 1024
MERGE_SUB = 256
TM_FFN = 512
CONV_HALO = 8


def _const_spec(param):
    arr, layer = param
    if layer is None:
        return pl.BlockSpec(arr.shape, lambda *_: (0,) * arr.ndim, pipeline_mode=pl.Buffered(1))
    zeros = (0,) * (arr.ndim - 1)
    return pl.BlockSpec((None,) + arr.shape[1:], lambda *_: (layer,) + zeros,
                        pipeline_mode=pl.Buffered(1))


def _params(sem):
    return pltpu.CompilerParams(dimension_semantics=sem, vmem_limit_bytes=VMEM_LIMIT)


def _rms(x, gain):
    ms = jnp.mean(x * x, axis=-1, keepdims=True)
    return x * lax.rsqrt(ms + EPS) * gain


def _split3(x):
    hi = x.astype(BF16)
    r = x - hi.astype(F32)
    mid = r.astype(BF16)
    lo = (r - mid.astype(F32)).astype(BF16)
    return hi, mid, lo


def _head_sums(sq, ones_ref):
    sq = sq.astype(BF16)
    ones = ones_ref[...]
    outs = [jnp.dot(sq[:, c0:c0 + 2 * LANE], ones, preferred_element_type=F32)
            for c0 in range(0, sq.shape[1], 2 * LANE)]
    return outs[0] if len(outs) == 1 else jnp.concatenate(outs, axis=1)


def _aug_lane(h):
    return MLA_QK + AUG_W * h if h % 2 == 0 else AUG_W * (h + 1)


def _den_lane(h):
    return MLA_V if h % 2 == 0 else 0


def _pad_value_heads(v):
    lane = lax.broadcasted_iota(jnp.int32, (v.shape[0], LANE), 1)
    tiles = []
    for h in range(N_HEADS):
        pair = v[:, (h // 2) * LANE:(h // 2 + 1) * LANE]
        mine = lane < MLA_V if h % 2 == 0 else lane >= MLA_V
        tiles.append(jnp.where(mine, pair, jnp.where(lane == _den_lane(h), 1.0, 0.0)))
    return jnp.concatenate(tiles, axis=1)


def _rope_table_kernel(pos_ref, inv_ref, sign_ref, cos_ref, sin_ref):
    ang = pos_ref[...] * inv_ref[...]
    cos_ref[...] = jnp.cos(ang)
    sin_ref[...] = jnp.sin(ang) * sign_ref[...]


def _rope_tables(positions):
    m = positions.size
    tm = 1024
    pos = positions.astype(F32).reshape(m, 1)
    inv_freq = ROPE_THETA ** (-jnp.arange(0, MLA_ROPE, 2, dtype=F32) / MLA_ROPE)
    half = MLA_ROPE // 2
    inv_lane = jnp.zeros((LANE,), F32)
    inv_lane = inv_lane.at[MLA_NOPE:MLA_NOPE + half].set(inv_freq)
    inv_lane = inv_lane.at[MLA_NOPE + half:MLA_QK].set(inv_freq)
    sign = np.zeros((LANE,), np.float32)
    sign[MLA_NOPE:MLA_NOPE + half] = -1.0
    sign[MLA_NOPE + half:MLA_QK] = 1.0
    return pl.pallas_call(
        _rope_table_kernel,
        grid=(m // tm,),
        in_specs=[pl.BlockSpec((tm, 1), lambda i: (i, 0)),
                  pl.BlockSpec((1, LANE), lambda i: (0, 0)),
                  pl.BlockSpec((1, LANE), lambda i: (0, 0))],
        out_specs=[pl.BlockSpec((tm, LANE), lambda i: (i, 0))] * 2,
        out_shape=[jax.ShapeDtypeStruct((m, LANE), F32)] * 2,
        compiler_params=_params(("arbitrary",)),
        name="rope_tables",
    )(pos, inv_lane.reshape(1, LANE), jnp.asarray(sign).reshape(1, LANE))


def _proj_kernel(x_ref, *refs):
    carry_ref = refs[-1]

    @pl.when(pl.program_id(1) == 0)
    def _():
        carry_ref[...] = jnp.zeros_like(carry_ref)

    for r0 in range(0, x_ref.shape[0], PROJ_SUB):
        _proj_rows(slice(r0, r0 + PROJ_SUB), x_ref, *refs)


def _proj_rows(rows, x_ref, cos_ref, sin_ref, g_ref, wa_ref, fbias_ref, augm_ref,
               qlat_g_ref, wuq_ref, kvlat_g_ref, wukv_ref, gq_ref, gk_ref, ones_qk_ref,
               gfq_ref, gfk_ref, ones_fox_ref, tri_ref,
               qm_ref, km_ref, vm_ref, qf_ref, kf_ref, vf_ref, u_ref,
               carry_ref):
    tm = PROJ_SUB
    h = _rms(x_ref[rows, :], g_ref[...]).astype(BF16)
    p = jnp.dot(h, wa_ref[...], preferred_element_type=F32)
    misc = p[:, C_MISC:C_MISC + LANE]
    lane = lax.broadcasted_iota(jnp.int32, (tm, LANE), 1)
    cos = cos_ref[rows, :]
    sin = sin_ref[rows, :]

    cq = _rms(p[:, C_CQ:C_CQ + MLA_Q_RANK], qlat_g_ref[...]).astype(BF16)
    q = jnp.dot(cq, wuq_ref[...], preferred_element_type=F32)
    cos4 = jnp.concatenate([cos] * N_HEADS, axis=1)
    sin4 = jnp.concatenate([sin] * N_HEADS, axis=1)
    first_half = lane < (MLA_NOPE + MLA_ROPE // 2)
    r = lax.rsqrt(_head_sums(q * q, ones_qk_ref) * (1.0 / MLA_QK) + EPS)
    qg = q * gq_ref[...]
    partners = []
    for hd in range(N_HEADS):
        blk = qg[:, hd * HEAD_PAD:(hd + 1) * HEAD_PAD]
        partners.append(jnp.where(first_half, pltpu.roll(blk, HEAD_PAD - MLA_ROPE // 2, 1),
                                  pltpu.roll(blk, MLA_ROPE // 2, 1)))
    qm_ref[rows, :] = (r * (qg * cos4 + jnp.concatenate(partners, axis=1) * sin4)).astype(BF16)

    ckv = _rms(p[:, C_CKV:C_CKV + MLA_KV_RANK], kvlat_g_ref[...]).astype(BF16)
    kv = jnp.dot(ckv, wukv_ref[...], preferred_element_type=F32)
    k_rope = jnp.where((lane >= KR_LANE) & (lane < MLA_QK), misc, 0.0)
    k = kv[:, :QK_W] + jnp.concatenate([k_rope] * N_HEADS, axis=1)
    r = lax.rsqrt(_head_sums(k * k, ones_qk_ref) * (1.0 / MLA_QK) + EPS)
    krg = k_rope * gk_ref[:, :LANE]
    partner = jnp.where(first_half, pltpu.roll(krg, HEAD_PAD - MLA_ROPE // 2, 1),
                        pltpu.roll(krg, MLA_ROPE // 2, 1)) * sin
    km_ref[rows, :] = (r * (k * gk_ref[...] * cos4
                            + jnp.concatenate([partner] * N_HEADS, axis=1))).astype(BF16)
    vm_ref[rows, :] = _pad_value_heads(kv[:, QK_W:]).astype(BF16)

    z = misc + fbias_ref[...]
    log_f = jnp.minimum(z, 0.0) - jnp.log1p(jnp.exp(-jnp.abs(z)))
    tri = tri_ref[...]
    parts = _split3(log_f)
    carry = carry_ref[0:1, :]
    c_blocks = []
    for r0 in range(0, tm, CUMSUM_BLOCK):
        cb = carry
        for part in parts:
            cb = cb + jnp.dot(tri, part[r0:r0 + CUMSUM_BLOCK], preferred_element_type=F32)
        carry = cb[CUMSUM_BLOCK - 1:CUMSUM_BLOCK, :]
        c_blocks.append(cb)
    carry_ref[0:1, :] = carry
    c = jnp.concatenate(c_blocks, axis=0) * LOG2E

    fq = p[:, C_FQ:C_FQ + BRANCH_W]
    fq = fq * lax.rsqrt(_head_sums(fq * fq, ones_fox_ref) * (1.0 / FOX_HD) + EPS) * gfq_ref[...]
    fk = p[:, C_FK:C_FK + BRANCH_W]
    fk = fk * lax.rsqrt(_head_sums(fk * fk, ones_fox_ref) * (1.0 / FOX_HD) + EPS) * gfk_ref[...]
    hi = c.astype(BF16).astype(F32)
    rem = c - hi
    mid = rem.astype(BF16).astype(F32)
    lo = rem - mid
    aug_q = hi * augm_ref[0:1, :] + mid * augm_ref[1:2, :] + lo * augm_ref[2:3, :] + augm_ref[3:4, :]
    aug_k = augm_ref[4:5, :] - hi * augm_ref[5:6, :] - mid * augm_ref[6:7, :] - lo * augm_ref[7:8, :]
    q_tiles, k_tiles = [], []
    for hd in range(N_HEADS):
        pair = slice((hd // 2) * LANE, (hd // 2 + 1) * LANE)
        data = lane < FOX_HD if hd % 2 == 0 else lane >= FOX_HD
        in_aug = (lane >= _aug_lane(hd)) & (lane < _aug_lane(hd) + AUG_W)
        q_tiles.append(jnp.where(data, fq[:, pair], jnp.where(in_aug, aug_q, 0.0)))
        k_tiles.append(jnp.where(data, fk[:, pair], jnp.where(in_aug, aug_k, 0.0)))
    qf_ref[rows, :] = jnp.concatenate(q_tiles, axis=1).astype(BF16)
    kf_ref[rows, :] = jnp.concatenate(k_tiles, axis=1).astype(BF16)
    vf_ref[rows, :] = _pad_value_heads(p[:, C_FV:C_FV + BRANCH_W]).astype(BF16)

    u_ref[rows, :] = p[:, C_U:C_U + S5_W]


def _proj(x, cos, sin, w, batch, seq):
    m = x.shape[0]
    tm = TM_PROJ
    nt = seq // tm
    row = lambda b, t: (b * nt + t, 0)
    consts = [w["attn_g"], w["wa"], w["fbias"], w["augm"], w["qlat_g"], w["wuq"], w["kvlat_g"], w["wukv"],
              w["gq"], w["gk"], w["ones_qk"], w["gfq"], w["gfk"], w["ones_fox"], w["tri"]]
    out_shape = [jax.ShapeDtypeStruct((m, QK_W), BF16)] * 6 + [
        jax.ShapeDtypeStruct((m, S5_W), F32)]
    out_specs = [pl.BlockSpec((tm, QK_W), row)] * 6 + [pl.BlockSpec((tm, S5_W), row)]
    return pl.pallas_call(
        _proj_kernel,
        grid=(batch, nt),
        in_specs=[pl.BlockSpec((tm, D_MODEL), row), pl.BlockSpec((tm, LANE), row),
                  pl.BlockSpec((tm, LANE), row)] + [_const_spec(c) for c in consts],
        out_specs=out_specs,
        out_shape=out_shape,
        scratch_shapes=[pltpu.VMEM((8, LANE), F32)],
        compiler_params=_params(("arbitrary", "arbitrary")),
        name="proj",
    )(x, cos, sin, *[c[0] for c in consts])


def _attn_kernel(q_ref, k_ref, v_ref, o_ref, acc_ref, m_ref):
    seq = q_ref.shape[0]
    tk = TK
    causal = (lax.broadcasted_iota(jnp.int32, (tk, tk), 0)
              >= lax.broadcasted_iota(jnp.int32, (tk, tk), 1))
    low_half = lax.broadcasted_iota(jnp.int32, (tk, LANE), 1) < MLA_V

    for j in range(seq // tk):
        r0 = j * tk
        for h in range(N_HEADS):
            hs = slice(h * HEAD_PAD, (h + 1) * HEAD_PAD)
            s = lax.dot_general(q_ref[r0:, hs], k_ref[r0:r0 + tk, hs],
                                (((1,), (1,)), ((), ())), preferred_element_type=F32)
            top = jnp.where(causal, s[:tk], NEG)
            s = top if seq - r0 == tk else jnp.concatenate([top, s[tk:]], axis=0)
            m_cur = jnp.max(s, axis=1, keepdims=True)
            if j == 0:
                m_new = jnp.broadcast_to(m_cur, (seq, LANE))
            else:
                m_prev = m_ref[h, r0:, :]
                m_new = jnp.maximum(m_prev, m_cur)
                alpha = jnp.exp2(m_prev - m_new)
            p = jnp.exp2(s - jnp.concatenate([m_new] * (tk // LANE), axis=1))
            pv = jnp.dot(p.astype(BF16), v_ref[r0:r0 + tk, hs], preferred_element_type=F32)
            if j == 0:
                acc_ref[h] = pv
            else:
                acc_ref[h, r0:, :] = acc_ref[h, r0:, :] * alpha + pv
            if seq - r0 > tk:
                m_ref[h, r0:, :] = m_new
        outs = []
        for h in range(N_HEADS):
            a = acc_ref[h, r0:r0 + tk, :]
            outs.append(a * (1.0 / a[:, _den_lane(h):_den_lane(h) + 1]))
        o_ref[r0:r0 + tk, :] = jnp.concatenate(
            [jnp.where(low_half, outs[0], outs[1]), jnp.where(low_half, outs[2], outs[3])],
            axis=1).astype(o_ref.dtype)


def _attention(q, k, v, batch, seq):
    m = q.shape[0]
    spec = pl.BlockSpec((seq, QK_W), lambda b: (b, 0))
    return pl.pallas_call(
        _attn_kernel,
        grid=(batch,),
        in_specs=[spec, spec, spec],
        out_specs=pl.BlockSpec((seq, BRANCH_W), lambda b: (b, 0)),
        out_shape=jax.ShapeDtypeStruct((m, BRANCH_W), BF16),
        scratch_shapes=[pltpu.VMEM((N_HEADS, seq, LANE), F32),
                        pltpu.VMEM((N_HEADS, seq, LANE), F32)],
        compiler_params=_params(("arbitrary",)),
        name="attn",
    )(q, k, v)


def _s5_kernel(u_ref, bmat_ref, lam_ref, cmat_ref, d_ref, wglu_ref, bglu_ref, o_ref, st_ref):
    batch, tt, _ = u_ref.shape

    @pl.when(pl.program_id(0) == 0)
    def _():
        st_ref[...] = jnp.zeros_like(st_ref)

    u = jnp.swapaxes(u_ref[...], 0, 1).reshape(tt * batch, S5_W)
    half = tt * batch // 2
    u16 = u.astype(BF16)
    bu = jnp.concatenate([jnp.dot(u16[r0:r0 + half], bmat_ref[...], preferred_element_type=F32)
                          for r0 in (0, half)], axis=0)
    lam_re = lam_ref[0:1, :]
    lam_im = lam_ref[1:2, :]
    x_re = st_ref[:, :S5_N]
    x_im = st_ref[:, S5_N:]
    states = []
    for t in range(tt):
        r = slice(t * batch, (t + 1) * batch)
        n_re = lam_re * x_re - lam_im * x_im + bu[r, :S5_N]
        n_im = lam_re * x_im + lam_im * x_re + bu[r, S5_N:]
        x_re, x_im = n_re, n_im
        states.append(jnp.concatenate([n_re, n_im], axis=1).astype(BF16))
    st_ref[:, :S5_N] = x_re
    st_ref[:, S5_N:] = x_im

    st = jnp.concatenate(states, axis=0)
    y = jnp.concatenate([jnp.dot(st[r0:r0 + half], cmat_ref[...], preferred_element_type=F32)
                         for r0 in (0, half)], axis=0)
    y = y + d_ref[...] * u
    y = 0.5 * y * (1.0 + jnp.tanh(math.sqrt(2.0 / math.pi) * (y + 0.044715 * (y * y * y))))
    z = jnp.dot(y.astype(BF16), wglu_ref[...], preferred_element_type=F32) + bglu_ref[...]
    o = (y * jax.nn.sigmoid(z)).reshape(tt, batch, S5_W)
    o_ref[...] = jnp.swapaxes(o, 0, 1).astype(o_ref.dtype)


def _s5(u, w, batch, seq):
    consts = [w["bmat"], w["lam"], w["cmat"], w["s5_d"], w["wglu"], w["bglu"]]
    blk = pl.BlockSpec((batch, TT_S5, S5_W), lambda t: (0, t, 0))
    return pl.pallas_call(
        _s5_kernel,
        grid=(seq // TT_S5,),
        in_specs=[blk] + [_const_spec(c) for c in consts],
        out_specs=blk,
        out_shape=jax.ShapeDtypeStruct((batch, seq, S5_W), BF16),
        scratch_shapes=[pltpu.VMEM((batch, 2 * S5_N), F32)],
        compiler_params=_params(("arbitrary",)),
        name="s5",
    )(u.reshape(batch, seq, S5_W), *[c[0] for c in consts]).reshape(batch * seq, S5_W)


def _merge_kernel(x_ref, om_ref, of_ref, os_ref, g_ref, wg_ref, wbr_ref, wout_ref, o_ref):
    for r0 in range(0, x_ref.shape[0], MERGE_SUB):
        rows = slice(r0, r0 + MERGE_SUB)
        x = x_ref[rows, :]
        h = _rms(x, g_ref[...]).astype(BF16)
        merged = None
        for n, br_ref in enumerate((om_ref, of_ref, os_ref)):
            logits = jnp.dot(h, wg_ref[:, n * D_MODEL:(n + 1) * D_MODEL],
                             preferred_element_type=F32)
            proj = jnp.dot(br_ref[rows, :], wbr_ref[n], preferred_element_type=F32)
            term = jax.nn.sigmoid(logits) * proj
            merged = term if merged is None else merged + term
        o_ref[rows, :] = x + jnp.dot(merged.astype(BF16), wout_ref[...],
                                     preferred_element_type=F32)


def _merge(x, o_mla, o_fox, o_s5, w, batch, seq):
    m = x.shape[0]
    tm = TM_MERGE
    nt = seq // tm
    row = lambda b, t: (b * nt + t, 0)
    consts = [w["attn_g"], w["wg"], w["wbr"], w["wout"]]
    return pl.pallas_call(
        _merge_kernel,
        grid=(batch, nt),
        in_specs=[pl.BlockSpec((tm, D_MODEL), row)] + [pl.BlockSpec((tm, BRANCH_W), row)] * 3
                 + [_const_spec(c) for c in consts],
        out_specs=pl.BlockSpec((tm, D_MODEL), row),
        out_shape=jax.ShapeDtypeStruct((m, D_MODEL), F32),
        compiler_params=_params(("arbitrary", "arbitrary")),
        name="merge",
    )(x, o_mla, o_fox, o_s5, *[c[0] for c in consts])


FFN_CHUNK = 256
FFN_DOWN_CUTS = (0, 512, 1024, 1536, 2048, 2560, D_FF)


def _ffn_kernel(x_ref, g_ref, wup_ref, conv_ref, wdown_ref, o_ref, up_ref, act_ref):
    t = pl.program_id(1)
    tm = x_ref.shape[0]

    @pl.when(t == 0)
    def _():
        up_ref[0:CONV_HALO, :] = jnp.zeros((CONV_HALO, 2 * D_FF), F32)

    x = x_ref[...]
    h = _rms(x, g_ref[...]).astype(BF16)
    up_ref[CONV_HALO:CONV_HALO + tm, :] = jnp.dot(h, wup_ref[...], preferred_element_type=F32)

    def conv(cols):
        out = None
        for j in range(CONV_W):
            lo = CONV_HALO - (CONV_W - 1) + j
            term = conv_ref[j:j + 1, cols] * up_ref[lo:lo + tm, cols]
            out = term if out is None else out + term
        return out

    out = x
    for c0 in range(0, D_FF, FFN_CHUNK):
        gate = conv(slice(c0, c0 + FFN_CHUNK))
        val = conv(slice(D_FF + c0, D_FF + c0 + FFN_CHUNK))
        act_ref[:, c0:c0 + FFN_CHUNK] = (gate * jax.nn.sigmoid(gate) * val).astype(BF16)
        c1 = c0 + FFN_CHUNK
        if c1 in FFN_DOWN_CUTS:
            k0 = FFN_DOWN_CUTS[FFN_DOWN_CUTS.index(c1) - 1]
            out = out + jnp.dot(act_ref[:, k0:c1], wdown_ref[k0:c1, :],
                                preferred_element_type=F32)

    up_ref[0:CONV_HALO, :] = up_ref[tm:tm + CONV_HALO, :]
    o_ref[...] = out


def _ffn(x, w, batch, seq):
    m = x.shape[0]
    tm = TM_FFN
    nt = seq // tm
    row = lambda b, t: (b * nt + t, 0)
    consts = [w["ffn_g"], w["wup"], w["conv"], w["wdown"]]
    return pl.pallas_call(
        _ffn_kernel,
        grid=(batch, nt),
        in_specs=[pl.BlockSpec((tm, D_MODEL), row)] + [_const_spec(c) for c in consts],
        out_specs=pl.BlockSpec((tm, D_MODEL), row),
        out_shape=jax.ShapeDtypeStruct((m, D_MODEL), F32),
        scratch_shapes=[pltpu.VMEM((tm + CONV_HALO, 2 * D_FF), F32),
                        pltpu.VMEM((tm, D_FF), BF16)],
        compiler_params=_params(("arbitrary", "arbitrary")),
        name="ffn",
    )(x, *[c[0] for c in consts])


def _block_ones(n, width):
    idx = np.arange(n) // width
    return jnp.asarray((idx[:, None] == idx[None, :]).astype(np.float32), dtype=BF16)


def _pack_weights(attn_norm_g, w_in, q_lat_norm_g, w_uq, kv_lat_norm_g, w_ukv,
                  mla_q_norm_g, mla_k_norm_g, fox_q_norm_g, fox_k_norm_g, fox_f_bias,
                  s5_lambda_re, s5_lambda_im, s5_b_re, s5_b_im, s5_c_re, s5_c_im, s5_d,
                  s5_log_step, s5_w_glu, s5_b_glu, w_branch, w_out, ffn_norm_g, w_up,
                  ffn_conv_w, w_down):
    depth = w_in.shape[0]
    f32 = lambda a: a.astype(F32)
    o = np.cumsum((0, MLA_Q_RANK, MLA_KV_RANK, MLA_ROPE, BRANCH_W, BRANCH_W, BRANCH_W,
                   N_HEADS, S5_W))
    col = lambda i: w_in[:, :, o[i]:o[i + 1]]
    spans = sorted([(KR_LANE, col(2))] + [
        (_aug_lane(h), jnp.repeat(col(6)[:, :, h:h + 1], AUG_W, axis=2)) for h in range(N_HEADS)],
        key=lambda s: s[0])
    misc, lane_pos = [], 0
    for start, piece in spans + [(LANE, None)]:
        if start > lane_pos:
            misc.append(jnp.zeros((depth, D_MODEL, start - lane_pos), w_in.dtype))
        if piece is not None:
            misc.append(piece)
            lane_pos = start + piece.shape[-1]
    wa = jnp.concatenate([col(0), col(1), col(3), col(4), col(5), col(7)] + misc, axis=-1)
    wg = w_in[:, :, o[8]:]

    pad_h = lambda a: jnp.pad(a, [(0, 0)] * (a.ndim - 1) + [(0, HEAD_PAD - a.shape[-1])])
    wuq = pad_h(w_uq).reshape(depth, MLA_Q_RANK, QK_W)
    wukv = jnp.concatenate(
        [pad_h(w_ukv[..., :MLA_NOPE]).reshape(depth, MLA_KV_RANK, QK_W),
         w_ukv[..., MLA_NOPE:].reshape(depth, MLA_KV_RANK, BRANCH_W)], axis=-1)
    tile_h = lambda g: jnp.tile(g, (1, N_HEADS)).reshape(depth, 1, -1)
    gq = tile_h(pad_h(f32(mla_q_norm_g))) * (MLA_QK ** -0.5 * LOG2E)
    gk = tile_h(pad_h(f32(mla_k_norm_g)))
    gfq = tile_h(f32(fox_q_norm_g)) * (FOX_HD ** -0.5 * LOG2E)
    gfk = tile_h(f32(fox_k_norm_g))
    fbias = jnp.zeros((depth, LANE), F32)
    augm = np.zeros((8, LANE), np.float32)
    for h in range(N_HEADS):
        a = _aug_lane(h)
        fbias = fbias.at[:, a:a + AUG_W].set(f32(fox_f_bias)[:, h:h + 1])
        for j in range(3):
            augm[j, a + j] = 1.0
            augm[3, a + 3 + j] = 1.0
            augm[4, a + j] = 1.0
            augm[5 + j, a + 3 + j] = 1.0
    fbias = fbias.reshape(depth, 1, LANE)

    lam_re, lam_im = f32(s5_lambda_re), f32(s5_lambda_im)
    step = jnp.exp(f32(s5_log_step))[..., None]
    mag = jnp.exp(lam_re * step)
    a_re, a_im = mag * jnp.cos(lam_im * step), mag * jnp.sin(lam_im * step)
    den = lam_re * lam_re + lam_im * lam_im
    k_re = ((a_re - 1.0) * lam_re + a_im * lam_im) / den
    k_im = (a_im * lam_re - (a_re - 1.0) * lam_im) / den
    b_re, b_im = f32(s5_b_re), f32(s5_b_im)
    bb_re = k_re[..., None] * b_re - k_im[..., None] * b_im
    bb_im = k_re[..., None] * b_im + k_im[..., None] * b_re
    eye = jnp.eye(S5_G, dtype=F32)
    bd_in = lambda a: jnp.einsum("lgph,gk->lghkp", a, eye).reshape(depth, S5_W, S5_N)
    bmat = jnp.concatenate([bd_in(bb_re), bd_in(bb_im)], axis=-1)
    bd_out = lambda a: jnp.einsum("lghp,gk->lgpkh", a, eye).reshape(depth, S5_N, S5_W)
    cmat = jnp.concatenate([bd_out(f32(s5_c_re)), -bd_out(f32(s5_c_im))], axis=1)
    lam = jnp.stack([a_re.reshape(depth, S5_N), a_im.reshape(depth, S5_N)], axis=1)

    bf = lambda a: a.astype(BF16)
    row = lambda a: f32(a).reshape(depth, 1, -1)
    tri = jnp.asarray(np.tril(np.ones((CUMSUM_BLOCK, CUMSUM_BLOCK), np.float32)), dtype=BF16)
    shared = {"ones_qk": _block_ones(2 * LANE, HEAD_PAD), "ones_fox": _block_ones(BRANCH_W, FOX_HD),
              "tri": tri, "augm": jnp.asarray(augm)}
    stacked = {
        "attn_g": row(attn_norm_g), "wa": bf(wa), "wg": bf(wg), "fbias": fbias,
        "qlat_g": row(q_lat_norm_g), "wuq": bf(wuq), "kvlat_g": row(kv_lat_norm_g),
        "wukv": bf(wukv), "gq": gq, "gk": gk, "gfq": gfq, "gfk": gfk,
        "bmat": bf(bmat), "lam": lam, "cmat": bf(cmat), "s5_d": row(s5_d),
        "wglu": bf(s5_w_glu), "bglu": row(s5_b_glu),
        "wbr": bf(w_branch), "wout": bf(w_out),
        "ffn_g": row(ffn_norm_g), "wup": bf(w_up), "conv": f32(ffn_conv_w), "wdown": bf(w_down),
    }
    return [dict({k: (v, None) for k, v in shared.items()},
                 **{k: (v, l) for k, v in stacked.items()}) for l in range(depth)]


def kernel(x, positions, attn_norm_g, w_in, q_lat_norm_g, w_uq, kv_lat_norm_g, w_ukv, mla_q_norm_g, mla_k_norm_g, fox_q_norm_g, fox_k_norm_g, fox_f_bias, s5_lambda_re, s5_lambda_im, s5_b_re, s5_b_im, s5_c_re, s5_c_im, s5_d, s5_log_step, s5_w_glu, s5_b_glu, w_branch, w_out, ffn_norm_g, w_up, ffn_conv_w, w_down):
    batch, seq, d_model = x.shape
    assert d_model == D_MODEL and seq % TM_PROJ == 0 and seq % TK == 0 and seq % TT_S5 == 0
    assert batch % 8 == 0
    layers = _pack_weights(attn_norm_g, w_in, q_lat_norm_g, w_uq, kv_lat_norm_g, w_ukv,
                           mla_q_norm_g, mla_k_norm_g, fox_q_norm_g, fox_k_norm_g, fox_f_bias,
                           s5_lambda_re, s5_lambda_im, s5_b_re, s5_b_im, s5_c_re, s5_c_im, s5_d,
                           s5_log_step, s5_w_glu, s5_b_glu, w_branch, w_out, ffn_norm_g, w_up,
                           ffn_conv_w, w_down)
    cos, sin = _rope_tables(positions)
    xf = x.astype(F32).reshape(batch * seq, D_MODEL)
    for w in layers:
        qm, km, vm, qf, kf, vf, u = _proj(xf, cos, sin, w, batch, seq)
        o_mla = _attention(qm, km, vm, batch, seq)
        o_fox = _attention(qf, kf, vf, batch, seq)
        o_s5 = _s5(u, w, batch, seq)
        xf = _merge(xf, o_mla, o_fox, o_s5, w, batch, seq)
        xf = _ffn(xf, w, batch, seq)
    return xf.reshape(batch, seq, D_MODEL).astype(x.dtype)
```

```python
import math

import numpy as np
import jax
import jax.numpy as jnp
from jax import lax
from jax.experimental import pallas as pl
from jax.experimental.pallas import tpu as pltpu

F32 = jnp.float32
BF16 = jnp.bfloat16

D_MODEL = 1024
N_HEADS = 4
MLA_NOPE = 64
MLA_ROPE = 32
MLA_QK = MLA_NOPE + MLA_ROPE
MLA_V = 64
MLA_Q_RANK = 384
MLA_KV_RANK = 256
FOX_HD = 64
S5_G = 16
S5_H = 16
S5_P = 64
S5_W = S5_G * S5_H
S5_N = S5_G * S5_P
BRANCH_W = 256
N_BRANCH = 3
D_FF = 2816
CONV_W = 3
ROPE_THETA = 10000.0
EPS = 1e-6
NEG = -1e30
LOG2E = math.log2(math.e)

LANE = 128
HEAD_PAD = 128
QK_W = N_HEADS * HEAD_PAD
C_CQ, C_CKV, C_FQ, C_FK, C_FV, C_U, C_MISC = 0, 384, 640, 896, 1152, 1408, 1664
WA_COLS = 1792
KR_LANE = MLA_NOPE
AUG_W = 6

VMEM_LIMIT = 56 * 1024 * 1024

TM_PROJ = 1024
PROJ_SUB = 512
CUMSUM_BLOCK = 256
TK = 512
TT_S5 = 64
TM_MERGE = 1024
MERGE_SUB = 256
TM_FFN = 1024
FFN_SUB = 512
CONV_HALO = 8


def _const_spec(param):
    arr, layer = param
    if layer is None:
        return pl.BlockSpec(arr.shape, lambda *_: (0,) * arr.ndim, pipeline_mode=pl.Buffered(1))
    zeros = (0,) * (arr.ndim - 1)
    return pl.BlockSpec((None,) + arr.shape[1:], lambda *_: (layer,) + zeros,
                        pipeline_mode=pl.Buffered(1))


def _params(sem):
    return pltpu.CompilerParams(dimension_semantics=sem, vmem_limit_bytes=VMEM_LIMIT)


def _rms(x, gain):
    ms = jnp.mean(x * x, axis=-1, keepdims=True)
    return x * lax.rsqrt(ms + EPS) * gain


def _split3(x):
    hi = x.astype(BF16)
    r = x - hi.astype(F32)
    mid = r.astype(BF16)
    lo = (r - mid.astype(F32)).astype(BF16)
    return hi, mid, lo


def _head_sums(sq, ones_ref):
    sq = sq.astype(BF16)
    ones = ones_ref[...]
    outs = [jnp.dot(sq[:, c0:c0 + 2 * LANE], ones, preferred_element_type=F32)
            for c0 in range(0, sq.shape[1], 2 * LANE)]
    return outs[0] if len(outs) == 1 else jnp.concatenate(outs, axis=1)


def _aug_lane(h):
    return MLA_QK + AUG_W * h if h % 2 == 0 else AUG_W * (h + 1)


def _den_lane(h):
    return MLA_V if h % 2 == 0 else 0


def _pad_value_heads(v):
    lane = lax.broadcasted_iota(jnp.int32, (v.shape[0], LANE), 1)
    tiles = []
    for h in range(N_HEADS):
        pair = v[:, (h // 2) * LANE:(h // 2 + 1) * LANE]
        mine = lane < MLA_V if h % 2 == 0 else lane >= MLA_V
        tiles.append(jnp.where(mine, pair, jnp.where(lane == _den_lane(h), 1.0, 0.0)))
    return jnp.concatenate(tiles, axis=1)


def _rope_table_kernel(pos_ref, inv_ref, sign_ref, cos_ref, sin_ref):
    ang = pos_ref[...] * inv_ref[...]
    cos_ref[...] = jnp.cos(ang)
    sin_ref[...] = jnp.sin(ang) * sign_ref[...]


def _rope_tables(positions):
    m = positions.size
    tm = 1024
    pos = positions.astype(F32).reshape(m, 1)
    inv_freq = ROPE_THETA ** (-jnp.arange(0, MLA_ROPE, 2, dtype=F32) / MLA_ROPE)
    half = MLA_ROPE // 2
    inv_lane = jnp.zeros((LANE,), F32)
    inv_lane = inv_lane.at[MLA_NOPE:MLA_NOPE + half].set(inv_freq)
    inv_lane = inv_lane.at[MLA_NOPE + half:MLA_QK].set(inv_freq)
    sign = np.zeros((LANE,), np.float32)
    sign[MLA_NOPE:MLA_NOPE + half] = -1.0
    sign[MLA_NOPE + half:MLA_QK] = 1.0
    return pl.pallas_call(
        _rope_table_kernel,
        grid=(m // tm,),
        in_specs=[pl.BlockSpec((tm, 1), lambda i: (i, 0)),
                  pl.BlockSpec((1, LANE), lambda i: (0, 0)),
                  pl.BlockSpec((1, LANE), lambda i: (0, 0))],
        out_specs=[pl.BlockSpec((tm, LANE), lambda i: (i, 0))] * 2,
        out_shape=[jax.ShapeDtypeStruct((m, LANE), F32)] * 2,
        compiler_params=_params(("arbitrary",)),
        name="rope_tables",
    )(pos, inv_lane.reshape(1, LANE), jnp.asarray(sign).reshape(1, LANE))


def _proj_kernel(x_ref, *refs):
    carry_ref = refs[-1]

    @pl.when(pl.program_id(1) == 0)
    def _():
        carry_ref[...] = jnp.zeros_like(carry_ref)

    for r0 in range(0, x_ref.shape[0], PROJ_SUB):
        _proj_rows(slice(r0, r0 + PROJ_SUB), x_ref, *refs)


def _proj_rows(rows, x_ref, cos_ref, sin_ref, g_ref, wa_ref, fbias_ref, augm_ref,
               qlat_g_ref, wuq_ref, kvlat_g_ref, wukv_ref, gq_ref, gk_ref, ones_qk_ref,
               gfq_ref, gfk_ref, ones_fox_ref, tri_ref,
               qm_ref, km_ref, vm_ref, qf_ref, kf_ref, vf_ref, u_ref,
               carry_ref):
    tm = PROJ_SUB
    h = _rms(x_ref[rows, :], g_ref[...]).astype(BF16)
    p = jnp.dot(h, wa_ref[...], preferred_element_type=F32)
    misc = p[:, C_MISC:C_MISC + LANE]
    lane = lax.broadcasted_iota(jnp.int32, (tm, LANE), 1)
    cos = cos_ref[rows, :]
    sin = sin_ref[rows, :]

    cq = _rms(p[:, C_CQ:C_CQ + MLA_Q_RANK], qlat_g_ref[...]).astype(BF16)
    q = jnp.dot(cq, wuq_ref[...], preferred_element_type=F32)
    cos4 = jnp.concatenate([cos] * N_HEADS, axis=1)
    sin4 = jnp.concatenate([sin] * N_HEADS, axis=1)
    first_half = lane < (MLA_NOPE + MLA_ROPE // 2)
    r = lax.rsqrt(_head_sums(q * q, ones_qk_ref) * (1.0 / MLA_QK) + EPS)
    qg = q * gq_ref[...]
    partners = []
    for hd in range(N_HEADS):
        blk = qg[:, hd * HEAD_PAD:(hd + 1) * HEAD_PAD]
        partners.append(jnp.where(first_half, pltpu.roll(blk, HEAD_PAD - MLA_ROPE // 2, 1),
                                  pltpu.roll(blk, MLA_ROPE // 2, 1)))
    qm_ref[rows, :] = (r * (qg * cos4 + jnp.concatenate(partners, axis=1) * sin4)).astype(BF16)

    ckv = _rms(p[:, C_CKV:C_CKV + MLA_KV_RANK], kvlat_g_ref[...]).astype(BF16)
    kv = jnp.dot(ckv, wukv_ref[...], preferred_element_type=F32)
    k_rope = jnp.where((lane >= KR_LANE) & (lane < MLA_QK), misc, 0.0)
    k = kv[:, :QK_W] + jnp.concatenate([k_rope] * N_HEADS, axis=1)
    r = lax.rsqrt(_head_sums(k * k, ones_qk_ref) * (1.0 / MLA_QK) + EPS)
    krg = k_rope * gk_ref[:, :LANE]
    partner = jnp.where(first_half, pltpu.roll(krg, HEAD_PAD - MLA_ROPE // 2, 1),
                        pltpu.roll(krg, MLA_ROPE // 2, 1)) * sin
    km_ref[rows, :] = (r * (k * gk_ref[...] * cos4
                            + jnp.concatenate([partner] * N_HEADS, axis=1))).astype(BF16)
    vm_ref[rows, :] = _pad_value_heads(kv[:, QK_W:]).astype(BF16)

    z = misc + fbias_ref[...]
    log_f = jnp.minimum(z, 0.0) - jnp.log1p(jnp.exp(-jnp.abs(z)))
    tri = tri_ref[...]
    parts = _split3(log_f)
    carry = carry_ref[0:1, :]
    c_blocks = []
    for r0 in range(0, tm, CUMSUM_BLOCK):
        cb = carry
        for part in parts:
            cb = cb + jnp.dot(tri, part[r0:r0 + CUMSUM_BLOCK], preferred_element_type=F32)
        carry = cb[CUMSUM_BLOCK - 1:CUMSUM_BLOCK, :]
        c_blocks.append(cb)
    carry_ref[0:1, :] = carry
    c = jnp.concatenate(c_blocks, axis=0) * LOG2E

    fq = p[:, C_FQ:C_FQ + BRANCH_W]
    fq = fq * lax.rsqrt(_head_sums(fq * fq, ones_fox_ref) * (1.0 / FOX_HD) + EPS) * gfq_ref[...]
    fk = p[:, C_FK:C_FK + BRANCH_W]
    fk = fk * lax.rsqrt(_head_sums(fk * fk, ones_fox_ref) * (1.0 / FOX_HD) + EPS) * gfk_ref[...]
    hi = c.astype(BF16).astype(F32)
    rem = c - hi
    mid = rem.astype(BF16).astype(F32)
    lo = rem - mid
    aug_q = hi * augm_ref[0:1, :] + mid * augm_ref[1:2, :] + lo * augm_ref[2:3, :] + augm_ref[3:4, :]
    aug_k = augm_ref[4:5, :] - hi * augm_ref[5:6, :] - mid * augm_ref[6:7, :] - lo * augm_ref[7:8, :]
    q_tiles, k_tiles = [], []
    for hd in range(N_HEADS):
        pair = slice((hd // 2) * LANE, (hd // 2 + 1) * LANE)
        data = lane < FOX_HD if hd % 2 == 0 else lane >= FOX_HD
        in_aug = (lane >= _aug_lane(hd)) & (lane < _aug_lane(hd) + AUG_W)
        q_tiles.append(jnp.where(data, fq[:, pair], jnp.where(in_aug, aug_q, 0.0)))
        k_tiles.append(jnp.where(data, fk[:, pair], jnp.where(in_aug, aug_k, 0.0)))
    qf_ref[rows, :] = jnp.concatenate(q_tiles, axis=1).astype(BF16)
    kf_ref[rows, :] = jnp.concatenate(k_tiles, axis=1).astype(BF16)
    vf_ref[rows, :] = _pad_value_heads(p[:, C_FV:C_FV + BRANCH_W]).astype(BF16)

    u_ref[rows, :] = p[:, C_U:C_U + S5_W]


def _proj(x, cos, sin, w, batch, seq):
    m = x.shape[0]
    tm = TM_PROJ
    nt = seq // tm
    row = lambda b, t: (b * nt + t, 0)
    consts = [w["attn_g"], w["wa"], w["fbias"], w["augm"], w["qlat_g"], w["wuq"], w["kvlat_g"], w["wukv"],
              w["gq"], w["gk"], w["ones_qk"], w["gfq"], w["gfk"], w["ones_fox"], w["tri"]]
    out_shape = [jax.ShapeDtypeStruct((m, QK_W), BF16)] * 6 + [
        jax.ShapeDtypeStruct((m, S5_W), F32)]
    out_specs = [pl.BlockSpec((tm, QK_W), row)] * 6 + [pl.BlockSpec((tm, S5_W), row)]
    return pl.pallas_call(
        _proj_kernel,
        grid=(batch, nt),
        in_specs=[pl.BlockSpec((tm, D_MODEL), row), pl.BlockSpec((tm, LANE), row),
                  pl.BlockSpec((tm, LANE), row)] + [_const_spec(c) for c in consts],
        out_specs=out_specs,
        out_shape=out_shape,
        scratch_shapes=[pltpu.VMEM((8, LANE), F32)],
        compiler_params=_params(("arbitrary", "arbitrary")),
        name="proj",
    )(x, cos, sin, *[c[0] for c in consts])


def _attn_kernel(q_ref, k_ref, v_ref, o_ref, acc_ref, m_ref):
    seq = q_ref.shape[0]
    tk = TK
    causal = (lax.broadcasted_iota(jnp.int32, (tk, tk), 0)
              >= lax.broadcasted_iota(jnp.int32, (tk, tk), 1))
    low_half = lax.broadcasted_iota(jnp.int32, (tk, LANE), 1) < MLA_V

    for j in range(seq // tk):
        r0 = j * tk
        for h in range(N_HEADS):
            hs = slice(h * HEAD_PAD, (h + 1) * HEAD_PAD)
            s = lax.dot_general(q_ref[r0:, hs], k_ref[r0:r0 + tk, hs],
                                (((1,), (1,)), ((), ())), preferred_element_type=F32)
            top = jnp.where(causal, s[:tk], NEG)
            s = top if seq - r0 == tk else jnp.concatenate([top, s[tk:]], axis=0)
            m_cur = jnp.max(s, axis=1, keepdims=True)
            if j == 0:
                m_new = jnp.broadcast_to(m_cur, (seq, LANE))
            else:
                m_prev = m_ref[h, r0:, :]
                m_new = jnp.maximum(m_prev, m_cur)
                alpha = jnp.exp2(m_prev - m_new)
            p = jnp.exp2(s - jnp.concatenate([m_new] * (tk // LANE), axis=1))
            pv = jnp.dot(p.astype(BF16), v_ref[r0:r0 + tk, hs], preferred_element_type=F32)
            if j == 0:
                acc_ref[h] = pv
            else:
                acc_ref[h, r0:, :] = acc_ref[h, r0:, :] * alpha + pv
            if seq - r0 > tk:
                m_ref[h, r0:, :] = m_new
        outs = []
        for h in range(N_HEADS):
            a = acc_ref[h, r0:r0 + tk, :]
            outs.append(a * (1.0 / a[:, _den_lane(h):_den_lane(h) + 1]))
        o_ref[r0:r0 + tk, :] = jnp.concatenate(
            [jnp.where(low_half, outs[0], outs[1]), jnp.where(low_half, outs[2], outs[3])],
            axis=1).astype(o_ref.dtype)


def _attention(q, k, v, batch, seq):
    m = q.shape[0]
    spec = pl.BlockSpec((seq, QK_W), lambda b: (b, 0))
    return pl.pallas_call(
        _attn_kernel,
        grid=(batch,),
        in_specs=[spec, spec, spec],
        out_specs=pl.BlockSpec((seq, BRANCH_W), lambda b: (b, 0)),
        out_shape=jax.ShapeDtypeStruct((m, BRANCH_W), BF16),
        scratch_shapes=[pltpu.VMEM((N_HEADS, seq, LANE), F32),
                        pltpu.VMEM((N_HEADS, seq, LANE), F32)],
        compiler_params=_params(("arbitrary",)),
        name="attn",
    )(q, k, v)


def _s5_kernel(u_ref, bmat_ref, lam_ref, cmat_ref, d_ref, wglu_ref, bglu_ref, o_ref, st_ref):
    batch, tt, _ = u_ref.shape

    @pl.when(pl.program_id(0) == 0)
    def _():
        st_ref[...] = jnp.zeros_like(st_ref)

    u = jnp.swapaxes(u_ref[...], 0, 1).reshape(tt * batch, S5_W)
    half = tt * batch // 2
    u16 = u.astype(BF16)
    bu = jnp.concatenate([jnp.dot(u16[r0:r0 + half], bmat_ref[...], preferred_element_type=F32)
                          for r0 in (0, half)], axis=0)
    lam_re = lam_ref[0:1, :]
    lam_im = lam_ref[1:2, :]
    x_re = st_ref[:, :S5_N]
    x_im = st_ref[:, S5_N:]
    states = []
    for t in range(tt):
        r = slice(t * batch, (t + 1) * batch)
        n_re = lam_re * x_re - lam_im * x_im + bu[r, :S5_N]
        n_im = lam_re * x_im + lam_im * x_re + bu[r, S5_N:]
        x_re, x_im = n_re, n_im
        states.append(jnp.concatenate([n_re, n_im], axis=1).astype(BF16))
    st_ref[:, :S5_N] = x_re
    st_ref[:, S5_N:] = x_im

    st = jnp.concatenate(states, axis=0)
    y = jnp.concatenate([jnp.dot(st[r0:r0 + half], cmat_ref[...], preferred_element_type=F32)
                         for r0 in (0, half)], axis=0)
    y = y + d_ref[...] * u
    y = 0.5 * y * (1.0 + jnp.tanh(math.sqrt(2.0 / math.pi) * (y + 0.044715 * (y * y * y))))
    z = jnp.dot(y.astype(BF16), wglu_ref[...], preferred_element_type=F32) + bglu_ref[...]
    o = (y * jax.nn.sigmoid(z)).reshape(tt, batch, S5_W)
    o_ref[...] = jnp.swapaxes(o, 0, 1).astype(o_ref.dtype)


def _s5(u, w, batch, seq):
    consts = [w["bmat"], w["lam"], w["cmat"], w["s5_d"], w["wglu"], w["bglu"]]
    blk = pl.BlockSpec((batch, TT_S5, S5_W), lambda t: (0, t, 0))
    return pl.pallas_call(
        _s5_kernel,
        grid=(seq // TT_S5,),
        in_specs=[blk] + [_const_spec(c) for c in consts],
        out_specs=blk,
        out_shape=jax.ShapeDtypeStruct((batch, seq, S5_W), BF16),
        scratch_shapes=[pltpu.VMEM((batch, 2 * S5_N), F32)],
        compiler_params=_params(("arbitrary",)),
        name="s5",
    )(u.reshape(batch, seq, S5_W), *[c[0] for c in consts]).reshape(batch * seq, S5_W)


def _merge_kernel(x_ref, om_ref, of_ref, os_ref, g_ref, wg_ref, wbr_ref, wout_ref, o_ref):
    for r0 in range(0, x_ref.shape[0], MERGE_SUB):
        rows = slice(r0, r0 + MERGE_SUB)
        x = x_ref[rows, :]
        h = _rms(x, g_ref[...]).astype(BF16)
        merged = None
        for n, br_ref in enumerate((om_ref, of_ref, os_ref)):
            logits = jnp.dot(h, wg_ref[:, n * D_MODEL:(n + 1) * D_MODEL],
                             preferred_element_type=F32)
            proj = jnp.dot(br_ref[rows, :], wbr_ref[n], preferred_element_type=F32)
            term = jax.nn.sigmoid(logits) * proj
            merged = term if merged is None else merged + term
        o_ref[rows, :] = x + jnp.dot(merged.astype(BF16), wout_ref[...],
                                     preferred_element_type=F32)


def _merge(x, o_mla, o_fox, o_s5, w, batch, seq):
    m = x.shape[0]
    tm = TM_MERGE
    nt = seq // tm
    row = lambda b, t: (b * nt + t, 0)
    consts = [w["attn_g"], w["wg"], w["wbr"], w["wout"]]
    return pl.pallas_call(
        _merge_kernel,
        grid=(batch, nt),
        in_specs=[pl.BlockSpec((tm, D_MODEL), row)] + [pl.BlockSpec((tm, BRANCH_W), row)] * 3
                 + [_const_spec(c) for c in consts],
        out_specs=pl.BlockSpec((tm, D_MODEL), row),
        out_shape=jax.ShapeDtypeStruct((m, D_MODEL), F32),
        compiler_params=_params(("arbitrary", "arbitrary")),
        name="merge",
    )(x, o_mla, o_fox, o_s5, *[c[0] for c in consts])


FFN_CHUNK = 256
FFN_DOWN_CUTS = (0, 512, 1024, 1536, 2048, 2560, D_FF)


def _ffn_kernel(x_ref, g_ref, wup_ref, conv_ref, wdown_ref, o_ref, up_ref, act_ref):
    tm = FFN_SUB

    @pl.when(pl.program_id(1) == 0)
    def _():
        up_ref[0:CONV_HALO, :] = jnp.zeros((CONV_HALO, 2 * D_FF), F32)

    def conv(cols):
        out = None
        for j in range(CONV_W):
            lo = CONV_HALO - (CONV_W - 1) + j
            term = conv_ref[j:j + 1, cols] * up_ref[lo:lo + tm, cols]
            out = term if out is None else out + term
        return out

    def sub_tile(i, carry):
        rows = pl.ds(pl.multiple_of(i * tm, tm), tm)
        x = x_ref[rows, :]
        h = _rms(x, g_ref[...]).astype(BF16)
        up_ref[CONV_HALO:CONV_HALO + tm, :] = jnp.dot(h, wup_ref[...], preferred_element_type=F32)
        out = x
        for c0 in range(0, D_FF, FFN_CHUNK):
            gate = conv(slice(c0, c0 + FFN_CHUNK))
            val = conv(slice(D_FF + c0, D_FF + c0 + FFN_CHUNK))
            act_ref[:, c0:c0 + FFN_CHUNK] = (gate * jax.nn.sigmoid(gate) * val).astype(BF16)
            c1 = c0 + FFN_CHUNK
            if c1 in FFN_DOWN_CUTS:
                k0 = FFN_DOWN_CUTS[FFN_DOWN_CUTS.index(c1) - 1]
                out = out + jnp.dot(act_ref[:, k0:c1], wdown_ref[k0:c1, :],
                                    preferred_element_type=F32)
        up_ref[0:CONV_HALO, :] = up_ref[tm:tm + CONV_HALO, :]
        o_ref[rows, :] = out
        return carry

    lax.fori_loop(0, x_ref.shape[0] // tm, sub_tile, 0)


def _ffn(x, w, batch, seq):
    m = x.shape[0]
    tm = TM_FFN
    nt = seq // tm
    row = lambda b, t: (b * nt + t, 0)
    consts = [w["ffn_g"], w["wup"], w["conv"], w["wdown"]]
    return pl.pallas_call(
        _ffn_kernel,
        grid=(batch, nt),
        in_specs=[pl.BlockSpec((tm, D_MODEL), row)] + [_const_spec(c) for c in consts],
        out_specs=pl.BlockSpec((tm, D_MODEL), row),
        out_shape=jax.ShapeDtypeStruct((m, D_MODEL), F32),
        scratch_shapes=[pltpu.VMEM((FFN_SUB + CONV_HALO, 2 * D_FF), F32),
                        pltpu.VMEM((FFN_SUB, D_FF), BF16)],
        compiler_params=_params(("arbitrary", "arbitrary")),
        name="ffn",
    )(x, *[c[0] for c in consts])


def _block_ones(n, width):
    idx = np.arange(n) // width
    return jnp.asarray((idx[:, None] == idx[None, :]).astype(np.float32), dtype=BF16)


def _pack_weights(attn_norm_g, w_in, q_lat_norm_g, w_uq, kv_lat_norm_g, w_ukv,
                  mla_q_norm_g, mla_k_norm_g, fox_q_norm_g, fox_k_norm_g, fox_f_bias,
                  s5_lambda_re, s5_lambda_im, s5_b_re, s5_b_im, s5_c_re, s5_c_im, s5_d,
                  s5_log_step, s5_w_glu, s5_b_glu, w_branch, w_out, ffn_norm_g, w_up,
                  ffn_conv_w, w_down):
    depth = w_in.shape[0]
    f32 = lambda a: a.astype(F32)
    o = np.cumsum((0, MLA_Q_RANK, MLA_KV_RANK, MLA_ROPE, BRANCH_W, BRANCH_W, BRANCH_W,
                   N_HEADS, S5_W))
    col = lambda i: w_in[:, :, o[i]:o[i + 1]]
    spans = sorted([(KR_LANE, col(2))] + [
        (_aug_lane(h), jnp.repeat(col(6)[:, :, h:h + 1], AUG_W, axis=2)) for h in range(N_HEADS)],
        key=lambda s: s[0])
    misc, lane_pos = [], 0
    for start, piece in spans + [(LANE, None)]:
        if start > lane_pos:
            misc.append(jnp.zeros((depth, D_MODEL, start - lane_pos), w_in.dtype))
        if piece is not None:
            misc.append(piece)
            lane_pos = start + piece.shape[-1]
    wa = jnp.concatenate([col(0), col(1), col(3), col(4), col(5), col(7)] + misc, axis=-1)
    wg = w_in[:, :, o[8]:]

    pad_h = lambda a: jnp.pad(a, [(0, 0)] * (a.ndim - 1) + [(0, HEAD_PAD - a.shape[-1])])
    wuq = pad_h(w_uq).reshape(depth, MLA_Q_RANK, QK_W)
    wukv = jnp.concatenate(
        [pad_h(w_ukv[..., :MLA_NOPE]).reshape(depth, MLA_KV_RANK, QK_W),
         w_ukv[..., MLA_NOPE:].reshape(depth, MLA_KV_RANK, BRANCH_W)], axis=-1)
    tile_h = lambda g: jnp.tile(g, (1, N_HEADS)).reshape(depth, 1, -1)
    gq = tile_h(pad_h(f32(mla_q_norm_g))) * (MLA_QK ** -0.5 * LOG2E)
    gk = tile_h(pad_h(f32(mla_k_norm_g)))
    gfq = tile_h(f32(fox_q_norm_g)) * (FOX_HD ** -0.5 * LOG2E)
    gfk = tile_h(f32(fox_k_norm_g))
    fbias = jnp.zeros((depth, LANE), F32)
    augm = np.zeros((8, LANE), np.float32)
    for h in range(N_HEADS):
        a = _aug_lane(h)
        fbias = fbias.at[:, a:a + AUG_W].set(f32(fox_f_bias)[:, h:h + 1])
        for j in range(3):
            augm[j, a + j] = 1.0
            augm[3, a + 3 + j] = 1.0
            augm[4, a + j] = 1.0
            augm[5 + j, a + 3 + j] = 1.0
    fbias = fbias.reshape(depth, 1, LANE)

    lam_re, lam_im = f32(s5_lambda_re), f32(s5_lambda_im)
    step = jnp.exp(f32(s5_log_step))[..., None]
    mag = jnp.exp(lam_re * step)
    a_re, a_im = mag * jnp.cos(lam_im * step), mag * jnp.sin(lam_im * step)
    den = lam_re * lam_re + lam_im * lam_im
    k_re = ((a_re - 1.0) * lam_re + a_im * lam_im) / den
    k_im = (a_im * lam_re - (a_re - 1.0) * lam_im) / den
    b_re, b_im = f32(s5_b_re), f32(s5_b_im)
    bb_re = k_re[..., None] * b_re - k_im[..., None] * b_im
    bb_im = k_re[..., None] * b_im + k_im[..., None] * b_re
    eye = jnp.eye(S5_G, dtype=F32)
    bd_in = lambda a: jnp.einsum("lgph,gk->lghkp", a, eye).reshape(depth, S5_W, S5_N)
    bmat = jnp.concatenate([bd_in(bb_re), bd_in(bb_im)], axis=-1)
    bd_out = lambda a: jnp.einsum("lghp,gk->lgpkh", a, eye).reshape(depth, S5_N, S5_W)
    cmat = jnp.concatenate([bd_out(f32(s5_c_re)), -bd_out(f32(s5_c_im))], axis=1)
    lam = jnp.stack([a_re.reshape(depth, S5_N), a_im.reshape(depth, S5_N)], axis=1)

    bf = lambda a: a.astype(BF16)
    row = lambda a: f32(a).reshape(depth, 1, -1)
    tri = jnp.asarray(np.tril(np.ones((CUMSUM_BLOCK, CUMSUM_BLOCK), np.float32)), dtype=BF16)
    shared = {"ones_qk": _block_ones(2 * LANE, HEAD_PAD), "ones_fox": _block_ones(BRANCH_W, FOX_HD),
              "tri": tri, "augm": jnp.asarray(augm)}
    stacked = {
        "attn_g": row(attn_norm_g), "wa": bf(wa), "wg": bf(wg), "fbias": fbias,
        "qlat_g": row(q_lat_norm_g), "wuq": bf(wuq), "kvlat_g": row(kv_lat_norm_g),
        "wukv": bf(wukv), "gq": gq, "gk": gk, "gfq": gfq, "gfk": gfk,
        "bmat": bf(bmat), "lam": lam, "cmat": bf(cmat), "s5_d": row(s5_d),
        "wglu": bf(s5_w_glu), "bglu": row(s5_b_glu),
        "wbr": bf(w_branch), "wout": bf(w_out),
        "ffn_g": row(ffn_norm_g), "wup": bf(w_up), "conv": f32(ffn_conv_w), "wdown": bf(w_down),
    }
    return [dict({k: (v, None) for k, v in shared.items()},
                 **{k: (v, l) for k, v in stacked.items()}) for l in range(depth)]


def kernel(x, positions, attn_norm_g, w_in, q_lat_norm_g, w_uq, kv_lat_norm_g, w_ukv, mla_q_norm_g, mla_k_norm_g, fox_q_norm_g, fox_k_norm_g, fox_f_bias, s5_lambda_re, s5_lambda_im, s5_b_re, s5_b_im, s5_c_re, s5_c_im, s5_d, s5_log_step, s5_w_glu, s5_b_glu, w_branch, w_out, ffn_norm_g, w_up, ffn_conv_w, w_down):
    batch, seq, d_model = x.shape
    assert d_model == D_MODEL and seq % TK == 0 and seq % TT_S5 == 0
    assert seq % TM_PROJ == 0 and seq % TM_MERGE == 0 and seq % TM_FFN == 0
    assert batch % 8 == 0
    layers = _pack_weights(attn_norm_g, w_in, q_lat_norm_g, w_uq, kv_lat_norm_g, w_ukv,
                           mla_q_norm_g, mla_k_norm_g, fox_q_norm_g, fox_k_norm_g, fox_f_bias,
                           s5_lambda_re, s5_lambda_im, s5_b_re, s5_b_im, s5_c_re, s5_c_im, s5_d,
                           s5_log_step, s5_w_glu, s5_b_glu, w_branch, w_out, ffn_norm_g, w_up,
                           ffn_conv_w, w_down)
    cos, sin = _rope_tables(positions)
    xf = x.astype(F32).reshape(batch * seq, D_MODEL)
    for w in layers:
        qm, km, vm, qf, kf, vf, u = _proj(xf, cos, sin, w, batch, seq)
        o_mla = _attention(qm, km, vm, batch, seq)
        o_fox = _attention(qf, kf, vf, batch, seq)
        o_s5 = _s5(u, w, batch, seq)
        xf = _merge(xf, o_mla, o_fox, o_s5, w, batch, seq)
        xf = _ffn(xf, w, batch, seq)
    return xf.reshape(batch, seq, D_MODEL).astype(x.dtype)
```

```python
import math

import numpy as np
import jax
import jax.numpy as jnp
from jax import lax
from jax.experimental import pallas as pl
from jax.experimental.pallas import tpu as pltpu

F32 = jnp.float32
BF16 = jnp.bfloat16

D_MODEL = 1024
N_HEADS = 4
MLA_NOPE = 64
MLA_ROPE = 32
MLA_QK = MLA_NOPE + MLA_ROPE
MLA_V = 64
MLA_Q_RANK = 384
MLA_KV_RANK = 256
FOX_HD = 64
S5_G = 16
S5_H = 16
S5_P = 64
S5_W = S5_G * S5_H
S5_N = S5_G * S5_P
BRANCH_W = 256
N_BRANCH = 3
D_FF = 2816
CONV_W = 3
ROPE_THETA = 10000.0
EPS = 1e-6
NEG = -1e30
LOG2E = math.log2(math.e)

LANE = 128
HEAD_PAD = 128
QK_W = N_HEADS * HEAD_PAD
C_CQ, C_CKV, C_FQ, C_FK, C_FV, C_U, C_MISC = 0, 384, 640, 896, 1152, 1408, 1664
WA_COLS = 1792
KR_LANE = MLA_NOPE
AUG_W = 6

VMEM_LIMIT = 56 * 1024 * 1024

TM_PROJ = 1024
PROJ_SUB = 512
CUMSUM_BLOCK = 256
TK = 512
TT_S5 = 64
TM_MERGE = 1024
MERGE_SUB = 256
TM_FFN = 1024
FFN_SUB = 512
CONV_HALO = 8


def _const_spec(param):
    arr, layer = param[:2]
    if layer is None:
        return pl.BlockSpec(arr.shape, lambda *_: (0,) * arr.ndim, pipeline_mode=pl.Buffered(1))
    if len(param) == 3:
        width, blk = param[2]
        return pl.BlockSpec((None, arr.shape[1], width), lambda *_: (layer, 0, blk),
                            pipeline_mode=pl.Buffered(1))
    zeros = (0,) * (arr.ndim - 1)
    return pl.BlockSpec((None,) + arr.shape[1:], lambda *_: (layer,) + zeros,
                        pipeline_mode=pl.Buffered(1))


def _params(sem):
    return pltpu.CompilerParams(dimension_semantics=sem, vmem_limit_bytes=VMEM_LIMIT)


def _rms(x, gain):
    ms = jnp.mean(x * x, axis=-1, keepdims=True)
    return x * lax.rsqrt(ms + EPS) * gain


def _split3(x):
    hi = x.astype(BF16)
    r = x - hi.astype(F32)
    mid = r.astype(BF16)
    lo = (r - mid.astype(F32)).astype(BF16)
    return hi, mid, lo


def _head_sums(sq, ones_ref):
    sq = sq.astype(BF16)
    ones = ones_ref[...]
    outs = [jnp.dot(sq[:, c0:c0 + 2 * LANE], ones, preferred_element_type=F32)
            for c0 in range(0, sq.shape[1], 2 * LANE)]
    return outs[0] if len(outs) == 1 else jnp.concatenate(outs, axis=1)


def _aug_lane(h):
    return MLA_QK + AUG_W * h if h % 2 == 0 else AUG_W * (h + 1)


def _den_lane(h):
    return MLA_V if h % 2 == 0 else 0


def _pad_value_heads(v):
    lane = lax.broadcasted_iota(jnp.int32, (v.shape[0], LANE), 1)
    tiles = []
    for h in range(N_HEADS):
        pair = v[:, (h // 2) * LANE:(h // 2 + 1) * LANE]
        mine = lane < MLA_V if h % 2 == 0 else lane >= MLA_V
        tiles.append(jnp.where(mine, pair, jnp.where(lane == _den_lane(h), 1.0, 0.0)))
    return jnp.concatenate(tiles, axis=1)


def _rope_table_kernel(pos_ref, inv_ref, sign_ref, cos_ref, sin_ref):
    ang = pos_ref[...] * inv_ref[...]
    cos_ref[...] = jnp.cos(ang)
    sin_ref[...] = jnp.sin(ang) * sign_ref[...]


def _rope_tables(positions):
    m = positions.size
    tm = 1024
    pos = positions.astype(F32).reshape(m, 1)
    inv_freq = ROPE_THETA ** (-jnp.arange(0, MLA_ROPE, 2, dtype=F32) / MLA_ROPE)
    half = MLA_ROPE // 2
    inv_lane = jnp.zeros((LANE,), F32)
    inv_lane = inv_lane.at[MLA_NOPE:MLA_NOPE + half].set(inv_freq)
    inv_lane = inv_lane.at[MLA_NOPE + half:MLA_QK].set(inv_freq)
    sign = np.zeros((LANE,), np.float32)
    sign[MLA_NOPE:MLA_NOPE + half] = -1.0
    sign[MLA_NOPE + half:MLA_QK] = 1.0
    return pl.pallas_call(
        _rope_table_kernel,
        grid=(m // tm,),
        in_specs=[pl.BlockSpec((tm, 1), lambda i: (i, 0)),
                  pl.BlockSpec((1, LANE), lambda i: (0, 0)),
                  pl.BlockSpec((1, LANE), lambda i: (0, 0))],
        out_specs=[pl.BlockSpec((tm, LANE), lambda i: (i, 0))] * 2,
        out_shape=[jax.ShapeDtypeStruct((m, LANE), F32)] * 2,
        compiler_params=_params(("arbitrary",)),
        name="rope_tables",
    )(pos, inv_lane.reshape(1, LANE), jnp.asarray(sign).reshape(1, LANE))


def _proj_kernel(x_ref, *refs):
    carry_ref = refs[-1]

    @pl.when(pl.program_id(1) == 0)
    def _():
        carry_ref[...] = jnp.zeros_like(carry_ref)

    for r0 in range(0, x_ref.shape[0], PROJ_SUB):
        _proj_rows(slice(r0, r0 + PROJ_SUB), x_ref, *refs)


def _proj_rows(rows, x_ref, cos_ref, sin_ref, g_ref, wa_ref, fbias_ref, augm_ref,
               qlat_g_ref, wuq_ref, kvlat_g_ref, wukv_ref, gq_ref, gk_ref, ones_qk_ref,
               gfq_ref, gfk_ref, ones_fox_ref, tri_ref,
               qm_ref, km_ref, vm_ref, qf_ref, kf_ref, vf_ref, u_ref,
               carry_ref):
    tm = PROJ_SUB
    h = _rms(x_ref[rows, :], g_ref[...]).astype(BF16)
    p = jnp.dot(h, wa_ref[...], preferred_element_type=F32)
    misc = p[:, C_MISC:C_MISC + LANE]
    lane = lax.broadcasted_iota(jnp.int32, (tm, LANE), 1)
    cos = cos_ref[rows, :]
    sin = sin_ref[rows, :]

    cq = _rms(p[:, C_CQ:C_CQ + MLA_Q_RANK], qlat_g_ref[...]).astype(BF16)
    q = jnp.dot(cq, wuq_ref[...], preferred_element_type=F32)
    cos4 = jnp.concatenate([cos] * N_HEADS, axis=1)
    sin4 = jnp.concatenate([sin] * N_HEADS, axis=1)
    first_half = lane < (MLA_NOPE + MLA_ROPE // 2)
    r = lax.rsqrt(_head_sums(q * q, ones_qk_ref) * (1.0 / MLA_QK) + EPS)
    qg = q * gq_ref[...]
    partners = []
    for hd in range(N_HEADS):
        blk = qg[:, hd * HEAD_PAD:(hd + 1) * HEAD_PAD]
        partners.append(jnp.where(first_half, pltpu.roll(blk, HEAD_PAD - MLA_ROPE // 2, 1),
                                  pltpu.roll(blk, MLA_ROPE // 2, 1)))
    qm_ref[rows, :] = (r * (qg * cos4 + jnp.concatenate(partners, axis=1) * sin4)).astype(BF16)

    ckv = _rms(p[:, C_CKV:C_CKV + MLA_KV_RANK], kvlat_g_ref[...]).astype(BF16)
    kv = jnp.dot(ckv, wukv_ref[...], preferred_element_type=F32)
    k_rope = jnp.where((lane >= KR_LANE) & (lane < MLA_QK), misc, 0.0)
    k = kv[:, :QK_W] + jnp.concatenate([k_rope] * N_HEADS, axis=1)
    r = lax.rsqrt(_head_sums(k * k, ones_qk_ref) * (1.0 / MLA_QK) + EPS)
    krg = k_rope * gk_ref[:, :LANE]
    partner = jnp.where(first_half, pltpu.roll(krg, HEAD_PAD - MLA_ROPE // 2, 1),
                        pltpu.roll(krg, MLA_ROPE // 2, 1)) * sin
    km_ref[rows, :] = (r * (k * gk_ref[...] * cos4
                            + jnp.concatenate([partner] * N_HEADS, axis=1))).astype(BF16)
    vm_ref[rows, :] = _pad_value_heads(kv[:, QK_W:]).astype(BF16)

    z = misc + fbias_ref[...]
    log_f = jnp.minimum(z, 0.0) - jnp.log1p(jnp.exp(-jnp.abs(z)))
    tri = tri_ref[...]
    parts = _split3(log_f)
    carry = carry_ref[0:1, :]
    c_blocks = []
    for r0 in range(0, tm, CUMSUM_BLOCK):
        cb = carry
        for part in parts:
            cb = cb + jnp.dot(tri, part[r0:r0 + CUMSUM_BLOCK], preferred_element_type=F32)
        carry = cb[CUMSUM_BLOCK - 1:CUMSUM_BLOCK, :]
        c_blocks.append(cb)
    carry_ref[0:1, :] = carry
    c = jnp.concatenate(c_blocks, axis=0) * LOG2E

    fq = p[:, C_FQ:C_FQ + BRANCH_W]
    fq = fq * lax.rsqrt(_head_sums(fq * fq, ones_fox_ref) * (1.0 / FOX_HD) + EPS) * gfq_ref[...]
    fk = p[:, C_FK:C_FK + BRANCH_W]
    fk = fk * lax.rsqrt(_head_sums(fk * fk, ones_fox_ref) * (1.0 / FOX_HD) + EPS) * gfk_ref[...]
    hi = c.astype(BF16).astype(F32)
    rem = c - hi
    mid = rem.astype(BF16).astype(F32)
    lo = rem - mid
    aug_q = hi * augm_ref[0:1, :] + mid * augm_ref[1:2, :] + lo * augm_ref[2:3, :] + augm_ref[3:4, :]
    aug_k = augm_ref[4:5, :] - hi * augm_ref[5:6, :] - mid * augm_ref[6:7, :] - lo * augm_ref[7:8, :]
    q_tiles, k_tiles = [], []
    for hd in range(N_HEADS):
        pair = slice((hd // 2) * LANE, (hd // 2 + 1) * LANE)
        data = lane < FOX_HD if hd % 2 == 0 else lane >= FOX_HD
        in_aug = (lane >= _aug_lane(hd)) & (lane < _aug_lane(hd) + AUG_W)
        q_tiles.append(jnp.where(data, fq[:, pair], jnp.where(in_aug, aug_q, 0.0)))
        k_tiles.append(jnp.where(data, fk[:, pair], jnp.where(in_aug, aug_k, 0.0)))
    qf_ref[rows, :] = jnp.concatenate(q_tiles, axis=1).astype(BF16)
    kf_ref[rows, :] = jnp.concatenate(k_tiles, axis=1).astype(BF16)
    vf_ref[rows, :] = _pad_value_heads(p[:, C_FV:C_FV + BRANCH_W]).astype(BF16)

    u_ref[rows, :] = p[:, C_U:C_U + S5_W]


def _proj(x, cos, sin, w, batch, seq):
    m = x.shape[0]
    tm = TM_PROJ
    nt = seq // tm
    row = lambda b, t: (b * nt + t, 0)
    consts = [w["attn_g"], w["wa"], w["fbias"], w["augm"], w["qlat_g"], w["wuq"], w["kvlat_g"], w["wukv"],
              w["gq"], w["gk"], w["ones_qk"], w["gfq"], w["gfk"], w["ones_fox"], w["tri"]]
    out_shape = [jax.ShapeDtypeStruct((m, QK_W), BF16)] * 6 + [
        jax.ShapeDtypeStruct((m, S5_W), F32)]
    out_specs = [pl.BlockSpec((tm, QK_W), row)] * 6 + [pl.BlockSpec((tm, S5_W), row)]
    return pl.pallas_call(
        _proj_kernel,
        grid=(batch, nt),
        in_specs=[pl.BlockSpec((tm, D_MODEL), row), pl.BlockSpec((tm, LANE), row),
                  pl.BlockSpec((tm, LANE), row)] + [_const_spec(c) for c in consts],
        out_specs=out_specs,
        out_shape=out_shape,
        scratch_shapes=[pltpu.VMEM((8, LANE), F32)],
        compiler_params=_params(("arbitrary", "arbitrary")),
        name="proj",
    )(x, cos, sin, *[c[0] for c in consts])


def _attn_kernel(q_ref, k_ref, v_ref, o_ref, acc_ref, m_ref):
    seq = q_ref.shape[0]
    tk = TK
    causal = (lax.broadcasted_iota(jnp.int32, (tk, tk), 0)
              >= lax.broadcasted_iota(jnp.int32, (tk, tk), 1))
    low_half = lax.broadcasted_iota(jnp.int32, (tk, LANE), 1) < MLA_V

    for j in range(seq // tk):
        r0 = j * tk
        for h in range(N_HEADS):
            hs = slice(h * HEAD_PAD, (h + 1) * HEAD_PAD)
            s = lax.dot_general(q_ref[r0:, hs], k_ref[r0:r0 + tk, hs],
                                (((1,), (1,)), ((), ())), preferred_element_type=F32)
            top = jnp.where(causal, s[:tk], NEG)
            s = top if seq - r0 == tk else jnp.concatenate([top, s[tk:]], axis=0)
            m_cur = jnp.max(s, axis=1, keepdims=True)
            if j == 0:
                m_new = jnp.broadcast_to(m_cur, (seq, LANE))
            else:
                m_prev = m_ref[h, r0:, :]
                m_new = jnp.maximum(m_prev, m_cur)
                alpha = jnp.exp2(m_prev - m_new)
            p = jnp.exp2(s - jnp.concatenate([m_new] * (tk // LANE), axis=1))
            pv = jnp.dot(p.astype(BF16), v_ref[r0:r0 + tk, hs], preferred_element_type=F32)
            if j == 0:
                acc_ref[h] = pv
            else:
                acc_ref[h, r0:, :] = acc_ref[h, r0:, :] * alpha + pv
            if seq - r0 > tk:
                m_ref[h, r0:, :] = m_new
        outs = []
        for h in range(N_HEADS):
            a = acc_ref[h, r0:r0 + tk, :]
            outs.append(a * (1.0 / a[:, _den_lane(h):_den_lane(h) + 1]))
        o_ref[r0:r0 + tk, :] = jnp.concatenate(
            [jnp.where(low_half, outs[0], outs[1]), jnp.where(low_half, outs[2], outs[3])],
            axis=1).astype(o_ref.dtype)


def _attention(q, k, v, batch, seq):
    m = q.shape[0]
    spec = pl.BlockSpec((seq, QK_W), lambda b: (b, 0))
    return pl.pallas_call(
        _attn_kernel,
        grid=(batch,),
        in_specs=[spec, spec, spec],
        out_specs=pl.BlockSpec((seq, BRANCH_W), lambda b: (b, 0)),
        out_shape=jax.ShapeDtypeStruct((m, BRANCH_W), BF16),
        scratch_shapes=[pltpu.VMEM((N_HEADS, seq, LANE), F32),
                        pltpu.VMEM((N_HEADS, seq, LANE), F32)],
        compiler_params=_params(("arbitrary",)),
        name="attn",
    )(q, k, v)


def _s5_kernel(u_ref, bmat_ref, lam_ref, cmat_ref, d_ref, wglu_ref, bglu_ref, o_ref, st_ref):
    batch, tt, _ = u_ref.shape

    @pl.when(pl.program_id(0) == 0)
    def _():
        st_ref[...] = jnp.zeros_like(st_ref)

    u = jnp.swapaxes(u_ref[...], 0, 1).reshape(tt * batch, S5_W)
    half = tt * batch // 2
    u16 = u.astype(BF16)
    bu = jnp.concatenate([jnp.dot(u16[r0:r0 + half], bmat_ref[...], preferred_element_type=F32)
                          for r0 in (0, half)], axis=0)
    lam_re = lam_ref[0:1, :]
    lam_im = lam_ref[1:2, :]
    x_re = st_ref[:, :S5_N]
    x_im = st_ref[:, S5_N:]
    states = []
    for t in range(tt):
        r = slice(t * batch, (t + 1) * batch)
        n_re = lam_re * x_re - lam_im * x_im + bu[r, :S5_N]
        n_im = lam_re * x_im + lam_im * x_re + bu[r, S5_N:]
        x_re, x_im = n_re, n_im
        states.append(jnp.concatenate([n_re, n_im], axis=1).astype(BF16))
    st_ref[:, :S5_N] = x_re
    st_ref[:, S5_N:] = x_im

    st = jnp.concatenate(states, axis=0)
    y = jnp.concatenate([jnp.dot(st[r0:r0 + half], cmat_ref[...], preferred_element_type=F32)
                         for r0 in (0, half)], axis=0)
    y = y + d_ref[...] * u
    y = 0.5 * y * (1.0 + jnp.tanh(math.sqrt(2.0 / math.pi) * (y + 0.044715 * (y * y * y))))
    z = jnp.dot(y.astype(BF16), wglu_ref[...], preferred_element_type=F32) + bglu_ref[...]
    o = (y * jax.nn.sigmoid(z)).reshape(tt, batch, S5_W)
    o_ref[...] = jnp.swapaxes(o, 0, 1).astype(o_ref.dtype)


def _s5(u, w, batch, seq):
    consts = [w["bmat"], w["lam"], w["cmat"], w["s5_d"], w["wglu"], w["bglu"]]
    blk = pl.BlockSpec((batch, TT_S5, S5_W), lambda t: (0, t, 0))
    return pl.pallas_call(
        _s5_kernel,
        grid=(seq // TT_S5,),
        in_specs=[blk] + [_const_spec(c) for c in consts],
        out_specs=blk,
        out_shape=jax.ShapeDtypeStruct((batch, seq, S5_W), BF16),
        scratch_shapes=[pltpu.VMEM((batch, 2 * S5_N), F32)],
        compiler_params=_params(("arbitrary",)),
        name="s5",
    )(u.reshape(batch, seq, S5_W), *[c[0] for c in consts]).reshape(batch * seq, S5_W)


def _merge_kernel(x_ref, om_ref, of_ref, os_ref, g_ref, wg_ref, wbr_ref, wout_ref, o_ref):
    for r0 in range(0, x_ref.shape[0], MERGE_SUB):
        rows = slice(r0, r0 + MERGE_SUB)
        x = x_ref[rows, :]
        h = _rms(x, g_ref[...]).astype(BF16)
        merged = None
        for n, br_ref in enumerate((om_ref, of_ref, os_ref)):
            logits = jnp.dot(h, wg_ref[:, n * D_MODEL:(n + 1) * D_MODEL],
                             preferred_element_type=F32)
            proj = jnp.dot(br_ref[rows, :], wbr_ref[n], preferred_element_type=F32)
            term = jax.nn.sigmoid(logits) * proj
            merged = term if merged is None else merged + term
        o_ref[rows, :] = x + jnp.dot(merged.astype(BF16), wout_ref[...],
                                     preferred_element_type=F32)


def _merge(x, o_mla, o_fox, o_s5, w, batch, seq):
    m = x.shape[0]
    tm = TM_MERGE
    nt = seq // tm
    row = lambda b, t: (b * nt + t, 0)
    consts = [w["attn_g"], w["wg"], w["wbr"], w["wout"]]
    return pl.pallas_call(
        _merge_kernel,
        grid=(batch, nt),
        in_specs=[pl.BlockSpec((tm, D_MODEL), row)] + [pl.BlockSpec((tm, BRANCH_W), row)] * 3
                 + [_const_spec(c) for c in consts],
        out_specs=pl.BlockSpec((tm, D_MODEL), row),
        out_shape=jax.ShapeDtypeStruct((m, D_MODEL), F32),
        compiler_params=_params(("arbitrary", "arbitrary")),
        name="merge",
    )(x, o_mla, o_fox, o_s5, *[c[0] for c in consts])


FFN_CHUNK = 256
FFN_DOWN_CUTS = (0, 512, 1024, 1536, 2048, 2560, D_FF)


def _ffn_kernel(x_ref, g_ref, wup_ref, conv_ref, wdown_ref, o_ref, up_ref, act_ref):
    tm = FFN_SUB

    @pl.when(pl.program_id(1) == 0)
    def _():
        up_ref[0:CONV_HALO, :] = jnp.zeros((CONV_HALO, 2 * D_FF), F32)

    def conv(cols):
        out = None
        for j in range(CONV_W):
            lo = CONV_HALO - (CONV_W - 1) + j
            term = conv_ref[j:j + 1, cols] * up_ref[lo:lo + tm, cols]
            out = term if out is None else out + term
        return out

    def sub_tile(i, carry):
        rows = pl.ds(pl.multiple_of(i * tm, tm), tm)
        x = x_ref[rows, :]
        h = _rms(x, g_ref[...]).astype(BF16)
        up_ref[CONV_HALO:CONV_HALO + tm, :] = jnp.dot(h, wup_ref[...], preferred_element_type=F32)
        out = x
        for c0 in range(0, D_FF, FFN_CHUNK):
            gate = conv(slice(c0, c0 + FFN_CHUNK))
            val = conv(slice(D_FF + c0, D_FF + c0 + FFN_CHUNK))
            act_ref[:, c0:c0 + FFN_CHUNK] = (gate * jax.nn.sigmoid(gate) * val).astype(BF16)
            c1 = c0 + FFN_CHUNK
            if c1 in FFN_DOWN_CUTS:
                k0 = FFN_DOWN_CUTS[FFN_DOWN_CUTS.index(c1) - 1]
                out = out + jnp.dot(act_ref[:, k0:c1], wdown_ref[k0:c1, :],
                                    preferred_element_type=F32)
        up_ref[0:CONV_HALO, :] = up_ref[tm:tm + CONV_HALO, :]
        o_ref[rows, :] = out
        return carry

    lax.fori_loop(0, x_ref.shape[0] // tm, sub_tile, 0)


def _ffn(x, w, batch, seq):
    m = x.shape[0]
    tm = TM_FFN
    nt = seq // tm
    row = lambda b, t: (b * nt + t, 0)
    consts = [w["ffn_g"], w["wup"], w["conv"], w["wdown"]]
    return pl.pallas_call(
        _ffn_kernel,
        grid=(batch, nt),
        in_specs=[pl.BlockSpec((tm, D_MODEL), row)] + [_const_spec(c) for c in consts],
        out_specs=pl.BlockSpec((tm, D_MODEL), row),
        out_shape=jax.ShapeDtypeStruct((m, D_MODEL), F32),
        scratch_shapes=[pltpu.VMEM((FFN_SUB + CONV_HALO, 2 * D_FF), F32),
                        pltpu.VMEM((FFN_SUB, D_FF), BF16)],
        compiler_params=_params(("arbitrary", "arbitrary")),
        name="ffn",
    )(x, *[c[0] for c in consts])


def _block_ones(n, width):
    idx = np.arange(n) // width
    return jnp.asarray((idx[:, None] == idx[None, :]).astype(np.float32), dtype=BF16)


def _pack_weights(attn_norm_g, w_in, q_lat_norm_g, w_uq, kv_lat_norm_g, w_ukv,
                  mla_q_norm_g, mla_k_norm_g, fox_q_norm_g, fox_k_norm_g, fox_f_bias,
                  s5_lambda_re, s5_lambda_im, s5_b_re, s5_b_im, s5_c_re, s5_c_im, s5_d,
                  s5_log_step, s5_w_glu, s5_b_glu, w_branch, w_out, ffn_norm_g, w_up,
                  ffn_conv_w, w_down):
    depth = w_in.shape[0]
    f32 = lambda a: a.astype(F32)
    o = np.cumsum((0, MLA_Q_RANK, MLA_KV_RANK, MLA_ROPE, BRANCH_W, BRANCH_W, BRANCH_W,
                   N_HEADS, S5_W))
    col = lambda i: w_in[:, :, o[i]:o[i + 1]]
    spans = sorted([(KR_LANE, col(2))] + [
        (_aug_lane(h), jnp.repeat(col(6)[:, :, h:h + 1], AUG_W, axis=2)) for h in range(N_HEADS)],
        key=lambda s: s[0])
    misc, lane_pos = [], 0
    for start, piece in spans + [(LANE, None)]:
        if start > lane_pos:
            misc.append(jnp.zeros((depth, D_MODEL, start - lane_pos), w_in.dtype))
        if piece is not None:
            misc.append(piece)
            lane_pos = start + piece.shape[-1]
    gate_w = N_BRANCH * D_MODEL
    pad_w = -gate_w % WA_COLS
    assert (gate_w + pad_w) % WA_COLS == 0
    w_all = jnp.concatenate(
        [w_in[:, :, o[8]:], jnp.zeros((depth, D_MODEL, pad_w), w_in.dtype),
         col(0), col(1), col(3), col(4), col(5), col(7)] + misc, axis=-1).astype(BF16)
    wa_block = (WA_COLS, (gate_w + pad_w) // WA_COLS)
    wg_block = (gate_w, 0)

    pad_h = lambda a: jnp.pad(a, [(0, 0)] * (a.ndim - 1) + [(0, HEAD_PAD - a.shape[-1])])
    wuq = pad_h(w_uq).reshape(depth, MLA_Q_RANK, QK_W)
    wukv = jnp.concatenate(
        [pad_h(w_ukv[..., :MLA_NOPE]).reshape(depth, MLA_KV_RANK, QK_W),
         w_ukv[..., MLA_NOPE:].reshape(depth, MLA_KV_RANK, BRANCH_W)], axis=-1)
    tile_h = lambda g: jnp.tile(g, (1, N_HEADS)).reshape(depth, 1, -1)
    gq = tile_h(pad_h(f32(mla_q_norm_g))) * (MLA_QK ** -0.5 * LOG2E)
    gk = tile_h(pad_h(f32(mla_k_norm_g)))
    gfq = tile_h(f32(fox_q_norm_g)) * (FOX_HD ** -0.5 * LOG2E)
    gfk = tile_h(f32(fox_k_norm_g))
    fbias = jnp.zeros((depth, LANE), F32)
    augm = np.zeros((8, LANE), np.float32)
    for h in range(N_HEADS):
        a = _aug_lane(h)
        fbias = fbias.at[:, a:a + AUG_W].set(f32(fox_f_bias)[:, h:h + 1])
        for j in range(3):
            augm[j, a + j] = 1.0
            augm[3, a + 3 + j] = 1.0
            augm[4, a + j] = 1.0
            augm[5 + j, a + 3 + j] = 1.0
    fbias = fbias.reshape(depth, 1, LANE)

    lam_re, lam_im = f32(s5_lambda_re), f32(s5_lambda_im)
    step = jnp.exp(f32(s5_log_step))[..., None]
    mag = jnp.exp(lam_re * step)
    a_re, a_im = mag * jnp.cos(lam_im * step), mag * jnp.sin(lam_im * step)
    den = lam_re * lam_re + lam_im * lam_im
    k_re = ((a_re - 1.0) * lam_re + a_im * lam_im) / den
    k_im = (a_im * lam_re - (a_re - 1.0) * lam_im) / den
    b_re, b_im = f32(s5_b_re), f32(s5_b_im)
    bb_re = k_re[..., None] * b_re - k_im[..., None] * b_im
    bb_im = k_re[..., None] * b_im + k_im[..., None] * b_re
    eye = jnp.eye(S5_G, dtype=F32)
    bd_in = lambda a: jnp.einsum("lgph,gk->lghkp", a, eye).reshape(depth, S5_W, S5_N)
    bmat = jnp.concatenate([bd_in(bb_re), bd_in(bb_im)], axis=-1)
    bd_out = lambda a: jnp.einsum("lghp,gk->lgpkh", a, eye).reshape(depth, S5_N, S5_W)
    cmat = jnp.concatenate([bd_out(f32(s5_c_re)), -bd_out(f32(s5_c_im))], axis=1)
    lam = jnp.stack([a_re.reshape(depth, S5_N), a_im.reshape(depth, S5_N)], axis=1)

    bf = lambda a: a.astype(BF16)
    row = lambda a: f32(a).reshape(depth, 1, -1)
    tri = jnp.asarray(np.tril(np.ones((CUMSUM_BLOCK, CUMSUM_BLOCK), np.float32)), dtype=BF16)
    shared = {"ones_qk": _block_ones(2 * LANE, HEAD_PAD), "ones_fox": _block_ones(BRANCH_W, FOX_HD),
              "tri": tri, "augm": jnp.asarray(augm)}
    stacked = {
        "attn_g": row(attn_norm_g), "fbias": fbias,
        "qlat_g": row(q_lat_norm_g), "wuq": bf(wuq), "kvlat_g": row(kv_lat_norm_g),
        "wukv": bf(wukv), "gq": gq, "gk": gk, "gfq": gfq, "gfk": gfk,
        "bmat": bf(bmat), "lam": lam, "cmat": bf(cmat), "s5_d": row(s5_d),
        "wglu": bf(s5_w_glu), "bglu": row(s5_b_glu),
        "wbr": bf(w_branch), "wout": bf(w_out),
        "ffn_g": row(ffn_norm_g), "wup": bf(w_up), "conv": f32(ffn_conv_w), "wdown": bf(w_down),
    }
    return [dict({k: (v, None) for k, v in shared.items()},
                 wa=(w_all, l, wa_block), wg=(w_all, l, wg_block),
                 **{k: (v, l) for k, v in stacked.items()}) for l in range(depth)]


def kernel(x, positions, attn_norm_g, w_in, q_lat_norm_g, w_uq, kv_lat_norm_g, w_ukv, mla_q_norm_g, mla_k_norm_g, fox_q_norm_g, fox_k_norm_g, fox_f_bias, s5_lambda_re, s5_lambda_im, s5_b_re, s5_b_im, s5_c_re, s5_c_im, s5_d, s5_log_step, s5_w_glu, s5_b_glu, w_branch, w_out, ffn_norm_g, w_up, ffn_conv_w, w_down):
    batch, seq, d_model = x.shape
    assert d_model == D_MODEL and seq % TK == 0 and seq % TT_S5 == 0
    assert seq % TM_PROJ == 0 and seq % TM_MERGE == 0 and seq % TM_FFN == 0
    assert batch % 8 == 0
    layers = _pack_weights(attn_norm_g, w_in, q_lat_norm_g, w_uq, kv_lat_norm_g, w_ukv,
                           mla_q_norm_g, mla_k_norm_g, fox_q_norm_g, fox_k_norm_g, fox_f_bias,
                           s5_lambda_re, s5_lambda_im, s5_b_re, s5_b_im, s5_c_re, s5_c_im, s5_d,
                           s5_log_step, s5_w_glu, s5_b_glu, w_branch, w_out, ffn_norm_g, w_up,
                           ffn_conv_w, w_down)
    cos, sin = _rope_tables(positions)
    xf = x.astype(F32).reshape(batch * seq, D_MODEL)
    for w in layers:
        qm, km, vm, qf, kf, vf, u = _proj(xf, cos, sin, w, batch, seq)
        o_mla = _attention(qm, km, vm, batch, seq)
        o_fox = _attention(qf, kf, vf, batch, seq)
        o_s5 = _s5(u, w, batch, seq)
        xf = _merge(xf, o_mla, o_fox, o_s5, w, batch, seq)
        xf = _ffn(xf, w, batch, seq)
    return xf.reshape(batch, seq, D_MODEL).astype(x.dtype)
```

```python
import math

import numpy as np
import jax
import jax.numpy as jnp
from jax import lax
from jax.experimental import pallas as pl
from jax.experimental.pallas import tpu as pltpu

F32 = jnp.float32
BF16 = jnp.bfloat16

D_MODEL = 1024
N_HEADS = 4
MLA_NOPE = 64
MLA_ROPE = 32
MLA_QK = MLA_NOPE + MLA_ROPE
MLA_V = 64
MLA_Q_RANK = 384
MLA_KV_RANK = 256
FOX_HD = 64
S5_G = 16
S5_H = 16
S5_P = 64
S5_W = S5_G * S5_H
S5_N = S5_G * S5_P
BRANCH_W = 256
N_BRANCH = 3
D_FF = 2816
CONV_W = 3
ROPE_THETA = 10000.0
EPS = 1e-6
NEG = -1e30
LOG2E = math.log2(math.e)

LANE = 128
HEAD_PAD = 128
QK_W = N_HEADS * HEAD_PAD
C_CQ, C_CKV, C_FQ, C_FK, C_FV, C_U, C_MISC = 0, 384, 640, 896, 1152, 1408, 1664
WA_COLS = 1792
KR_LANE = MLA_NOPE
AUG_W = 6

VMEM_LIMIT = 56 * 1024 * 1024

TM_PROJ = 1024
PROJ_SUB = 512
CUMSUM_BLOCK = 256
TK = 512
TT_S5 = 128
TM_MERGE = 1024
MERGE_SUB = 256
TM_FFN = 1024
FFN_SUB = 512
CONV_HALO = 8


def _const_spec(param):
    arr, layer = param
    if layer is None:
        return pl.BlockSpec(arr.shape, lambda *_: (0,) * arr.ndim, pipeline_mode=pl.Buffered(1))
    zeros = (0,) * (arr.ndim - 1)
    return pl.BlockSpec((None,) + arr.shape[1:], lambda *_: (layer,) + zeros,
                        pipeline_mode=pl.Buffered(1))


def _params(sem):
    return pltpu.CompilerParams(dimension_semantics=sem, vmem_limit_bytes=VMEM_LIMIT)


def _rms(x, gain):
    ms = jnp.mean(x * x, axis=-1, keepdims=True)
    return x * lax.rsqrt(ms + EPS) * gain


def _split3(x):
    hi = x.astype(BF16)
    r = x - hi.astype(F32)
    mid = r.astype(BF16)
    lo = (r - mid.astype(F32)).astype(BF16)
    return hi, mid, lo


def _head_sums(sq, ones_ref):
    sq = sq.astype(BF16)
    ones = ones_ref[...]
    outs = [jnp.dot(sq[:, c0:c0 + 2 * LANE], ones, preferred_element_type=F32)
            for c0 in range(0, sq.shape[1], 2 * LANE)]
    return outs[0] if len(outs) == 1 else jnp.concatenate(outs, axis=1)


def _aug_lane(h):
    return MLA_QK + AUG_W * h if h % 2 == 0 else AUG_W * (h + 1)


def _den_lane(h):
    return MLA_V if h % 2 == 0 else 0


def _pad_value_heads(v):
    lane = lax.broadcasted_iota(jnp.int32, (v.shape[0], LANE), 1)
    tiles = []
    for h in range(N_HEADS):
        pair = v[:, (h // 2) * LANE:(h // 2 + 1) * LANE]
        mine = lane < MLA_V if h % 2 == 0 else lane >= MLA_V
        tiles.append(jnp.where(mine, pair, jnp.where(lane == _den_lane(h), 1.0, 0.0)))
    return jnp.concatenate(tiles, axis=1)


def _rope_table_kernel(pos_ref, inv_ref, sign_ref, cos_ref, sin_ref):
    ang = pos_ref[...] * inv_ref[...]
    cos_ref[...] = jnp.cos(ang)
    sin_ref[...] = jnp.sin(ang) * sign_ref[...]


def _rope_tables(positions):
    m = positions.size
    tm = 1024
    pos = positions.astype(F32).reshape(m, 1)
    inv_freq = ROPE_THETA ** (-jnp.arange(0, MLA_ROPE, 2, dtype=F32) / MLA_ROPE)
    half = MLA_ROPE // 2
    inv_lane = jnp.zeros((LANE,), F32)
    inv_lane = inv_lane.at[MLA_NOPE:MLA_NOPE + half].set(inv_freq)
    inv_lane = inv_lane.at[MLA_NOPE + half:MLA_QK].set(inv_freq)
    sign = np.zeros((LANE,), np.float32)
    sign[MLA_NOPE:MLA_NOPE + half] = -1.0
    sign[MLA_NOPE + half:MLA_QK] = 1.0
    return pl.pallas_call(
        _rope_table_kernel,
        grid=(m // tm,),
        in_specs=[pl.BlockSpec((tm, 1), lambda i: (i, 0)),
                  pl.BlockSpec((1, LANE), lambda i: (0, 0)),
                  pl.BlockSpec((1, LANE), lambda i: (0, 0))],
        out_specs=[pl.BlockSpec((tm, LANE), lambda i: (i, 0))] * 2,
        out_shape=[jax.ShapeDtypeStruct((m, LANE), F32)] * 2,
        compiler_params=_params(("arbitrary",)),
        name="rope_tables",
    )(pos, inv_lane.reshape(1, LANE), jnp.asarray(sign).reshape(1, LANE))


def _proj_kernel(x_ref, *refs):
    carry_ref = refs[-1]

    @pl.when(pl.program_id(1) == 0)
    def _():
        carry_ref[...] = jnp.zeros_like(carry_ref)

    for r0 in range(0, x_ref.shape[0], PROJ_SUB):
        _proj_rows(slice(r0, r0 + PROJ_SUB), x_ref, *refs)


def _proj_rows(rows, x_ref, cos_ref, sin_ref, g_ref, wa_ref, fbias_ref, augm_ref,
               qlat_g_ref, wuq_ref, kvlat_g_ref, wukv_ref, gq_ref, gk_ref, ones_qk_ref,
               gfq_ref, gfk_ref, ones_fox_ref, tri_ref,
               qm_ref, km_ref, vm_ref, qf_ref, kf_ref, vf_ref, u_ref,
               carry_ref):
    tm = PROJ_SUB
    h = _rms(x_ref[rows, :], g_ref[...]).astype(BF16)
    p = jnp.dot(h, wa_ref[...], preferred_element_type=F32)
    misc = p[:, C_MISC:C_MISC + LANE]
    lane = lax.broadcasted_iota(jnp.int32, (tm, LANE), 1)
    cos = cos_ref[rows, :]
    sin = sin_ref[rows, :]

    cq = _rms(p[:, C_CQ:C_CQ + MLA_Q_RANK], qlat_g_ref[...]).astype(BF16)
    q = jnp.dot(cq, wuq_ref[...], preferred_element_type=F32)
    cos4 = jnp.concatenate([cos] * N_HEADS, axis=1)
    sin4 = jnp.concatenate([sin] * N_HEADS, axis=1)
    first_half = lane < (MLA_NOPE + MLA_ROPE // 2)
    r = lax.rsqrt(_head_sums(q * q, ones_qk_ref) * (1.0 / MLA_QK) + EPS)
    qg = q * gq_ref[...]
    partners = []
    for hd in range(N_HEADS):
        blk = qg[:, hd * HEAD_PAD:(hd + 1) * HEAD_PAD]
        partners.append(jnp.where(first_half, pltpu.roll(blk, HEAD_PAD - MLA_ROPE // 2, 1),
                                  pltpu.roll(blk, MLA_ROPE // 2, 1)))
    qm_ref[rows, :] = (r * (qg * cos4 + jnp.concatenate(partners, axis=1) * sin4)).astype(BF16)

    ckv = _rms(p[:, C_CKV:C_CKV + MLA_KV_RANK], kvlat_g_ref[...]).astype(BF16)
    kv = jnp.dot(ckv, wukv_ref[...], preferred_element_type=F32)
    k_rope = jnp.where((lane >= KR_LANE) & (lane < MLA_QK), misc, 0.0)
    k = kv[:, :QK_W] + jnp.concatenate([k_rope] * N_HEADS, axis=1)
    r = lax.rsqrt(_head_sums(k * k, ones_qk_ref) * (1.0 / MLA_QK) + EPS)
    krg = k_rope * gk_ref[:, :LANE]
    partner = jnp.where(first_half, pltpu.roll(krg, HEAD_PAD - MLA_ROPE // 2, 1),
                        pltpu.roll(krg, MLA_ROPE // 2, 1)) * sin
    km_ref[rows, :] = (r * (k * gk_ref[...] * cos4
                            + jnp.concatenate([partner] * N_HEADS, axis=1))).astype(BF16)
    vm_ref[rows, :] = _pad_value_heads(kv[:, QK_W:]).astype(BF16)

    z = misc + fbias_ref[...]
    log_f = jnp.minimum(z, 0.0) - jnp.log1p(jnp.exp(-jnp.abs(z)))
    tri = tri_ref[...]
    parts = _split3(log_f)
    carry = carry_ref[0:1, :]
    c_blocks = []
    for r0 in range(0, tm, CUMSUM_BLOCK):
        cb = carry
        for part in parts:
            cb = cb + jnp.dot(tri, part[r0:r0 + CUMSUM_BLOCK], preferred_element_type=F32)
        carry = cb[CUMSUM_BLOCK - 1:CUMSUM_BLOCK, :]
        c_blocks.append(cb)
    carry_ref[0:1, :] = carry
    c = jnp.concatenate(c_blocks, axis=0) * LOG2E

    fq = p[:, C_FQ:C_FQ + BRANCH_W]
    fq = fq * lax.rsqrt(_head_sums(fq * fq, ones_fox_ref) * (1.0 / FOX_HD) + EPS) * gfq_ref[...]
    fk = p[:, C_FK:C_FK + BRANCH_W]
    fk = fk * lax.rsqrt(_head_sums(fk * fk, ones_fox_ref) * (1.0 / FOX_HD) + EPS) * gfk_ref[...]
    hi = c.astype(BF16).astype(F32)
    rem = c - hi
    mid = rem.astype(BF16).astype(F32)
    lo = rem - mid
    aug_q = hi * augm_ref[0:1, :] + mid * augm_ref[1:2, :] + lo * augm_ref[2:3, :] + augm_ref[3:4, :]
    aug_k = augm_ref[4:5, :] - hi * augm_ref[5:6, :] - mid * augm_ref[6:7, :] - lo * augm_ref[7:8, :]
    q_tiles, k_tiles = [], []
    for hd in range(N_HEADS):
        pair = slice((hd // 2) * LANE, (hd // 2 + 1) * LANE)
        data = lane < FOX_HD if hd % 2 == 0 else lane >= FOX_HD
        in_aug = (lane >= _aug_lane(hd)) & (lane < _aug_lane(hd) + AUG_W)
        q_tiles.append(jnp.where(data, fq[:, pair], jnp.where(in_aug, aug_q, 0.0)))
        k_tiles.append(jnp.where(data, fk[:, pair], jnp.where(in_aug, aug_k, 0.0)))
    qf_ref[rows, :] = jnp.concatenate(q_tiles, axis=1).astype(BF16)
    kf_ref[rows, :] = jnp.concatenate(k_tiles, axis=1).astype(BF16)
    vf_ref[rows, :] = _pad_value_heads(p[:, C_FV:C_FV + BRANCH_W]).astype(BF16)

    u_ref[rows, :] = p[:, C_U:C_U + S5_W]


def _proj(x, cos, sin, w, batch, seq):
    m = x.shape[0]
    tm = TM_PROJ
    nt = seq // tm
    row = lambda b, t: (b * nt + t, 0)
    consts = [w["attn_g"], w["wa"], w["fbias"], w["augm"], w["qlat_g"], w["wuq"], w["kvlat_g"], w["wukv"],
              w["gq"], w["gk"], w["ones_qk"], w["gfq"], w["gfk"], w["ones_fox"], w["tri"]]
    out_shape = [jax.ShapeDtypeStruct((m, QK_W), BF16)] * 6 + [
        jax.ShapeDtypeStruct((m, S5_W), F32)]
    out_specs = [pl.BlockSpec((tm, QK_W), row)] * 6 + [pl.BlockSpec((tm, S5_W), row)]
    return pl.pallas_call(
        _proj_kernel,
        grid=(batch, nt),
        in_specs=[pl.BlockSpec((tm, D_MODEL), row), pl.BlockSpec((tm, LANE), row),
                  pl.BlockSpec((tm, LANE), row)] + [_const_spec(c) for c in consts],
        out_specs=out_specs,
        out_shape=out_shape,
        scratch_shapes=[pltpu.VMEM((8, LANE), F32)],
        compiler_params=_params(("arbitrary", "arbitrary")),
        name="proj",
    )(x, cos, sin, *[c[0] for c in consts])


def _attn_kernel(q_ref, k_ref, v_ref, o_ref, acc_ref, m_ref):
    seq = q_ref.shape[0]
    tk = TK
    causal = (lax.broadcasted_iota(jnp.int32, (tk, tk), 0)
              >= lax.broadcasted_iota(jnp.int32, (tk, tk), 1))
    low_half = lax.broadcasted_iota(jnp.int32, (tk, LANE), 1) < MLA_V

    for j in range(seq // tk):
        r0 = j * tk
        for h in range(N_HEADS):
            hs = slice(h * HEAD_PAD, (h + 1) * HEAD_PAD)
            s = lax.dot_general(q_ref[r0:, hs], k_ref[r0:r0 + tk, hs],
                                (((1,), (1,)), ((), ())), preferred_element_type=F32)
            top = jnp.where(causal, s[:tk], NEG)
            s = top if seq - r0 == tk else jnp.concatenate([top, s[tk:]], axis=0)
            m_cur = jnp.max(s, axis=1, keepdims=True)
            if j == 0:
                m_new = jnp.broadcast_to(m_cur, (seq, LANE))
            else:
                m_prev = m_ref[h, r0:, :]
                m_new = jnp.maximum(m_prev, m_cur)
                alpha = jnp.exp2(m_prev - m_new)
            p = jnp.exp2(s - jnp.concatenate([m_new] * (tk // LANE), axis=1))
            pv = jnp.dot(p.astype(BF16), v_ref[r0:r0 + tk, hs], preferred_element_type=F32)
            if j == 0:
                acc_ref[h] = pv
            else:
                acc_ref[h, r0:, :] = acc_ref[h, r0:, :] * alpha + pv
            if seq - r0 > tk:
                m_ref[h, r0:, :] = m_new
        outs = []
        for h in range(N_HEADS):
            a = acc_ref[h, r0:r0 + tk, :]
            outs.append(a * (1.0 / a[:, _den_lane(h):_den_lane(h) + 1]))
        o_ref[r0:r0 + tk, :] = jnp.concatenate(
            [jnp.where(low_half, outs[0], outs[1]), jnp.where(low_half, outs[2], outs[3])],
            axis=1).astype(o_ref.dtype)


def _attention(q, k, v, batch, seq):
    m = q.shape[0]
    spec = pl.BlockSpec((seq, QK_W), lambda b: (b, 0))
    return pl.pallas_call(
        _attn_kernel,
        grid=(batch,),
        in_specs=[spec, spec, spec],
        out_specs=pl.BlockSpec((seq, BRANCH_W), lambda b: (b, 0)),
        out_shape=jax.ShapeDtypeStruct((m, BRANCH_W), BF16),
        scratch_shapes=[pltpu.VMEM((N_HEADS, seq, LANE), F32),
                        pltpu.VMEM((N_HEADS, seq, LANE), F32)],
        compiler_params=_params(("arbitrary",)),
        name="attn",
    )(q, k, v)


def _s5_kernel(u_ref, bmat_ref, lam_ref, cmat_ref, d_ref, wglu_ref, bglu_ref, o_ref, st_ref):
    batch, tt, _ = u_ref.shape

    @pl.when(pl.program_id(0) == 0)
    def _():
        st_ref[...] = jnp.zeros_like(st_ref)

    u = jnp.swapaxes(u_ref[...], 0, 1).reshape(tt * batch, S5_W)
    half = tt * batch // 2
    u16 = u.astype(BF16)
    bu = jnp.concatenate([jnp.dot(u16[r0:r0 + half], bmat_ref[...], preferred_element_type=F32)
                          for r0 in (0, half)], axis=0)
    lam_re = lam_ref[0:1, :]
    lam_im = lam_ref[1:2, :]
    x_re = st_ref[:, :S5_N]
    x_im = st_ref[:, S5_N:]
    states = []
    for t in range(tt):
        r = slice(t * batch, (t + 1) * batch)
        n_re = lam_re * x_re - lam_im * x_im + bu[r, :S5_N]
        n_im = lam_re * x_im + lam_im * x_re + bu[r, S5_N:]
        x_re, x_im = n_re, n_im
        states.append(jnp.concatenate([n_re, n_im], axis=1).astype(BF16))
    st_ref[:, :S5_N] = x_re
    st_ref[:, S5_N:] = x_im

    st = jnp.concatenate(states, axis=0)
    y = jnp.concatenate([jnp.dot(st[r0:r0 + half], cmat_ref[...], preferred_element_type=F32)
                         for r0 in (0, half)], axis=0)
    y = y + d_ref[...] * u
    y = 0.5 * y * (1.0 + jnp.tanh(math.sqrt(2.0 / math.pi) * (y + 0.044715 * (y * y * y))))
    z = jnp.dot(y.astype(BF16), wglu_ref[...], preferred_element_type=F32) + bglu_ref[...]
    o = (y * jax.nn.sigmoid(z)).reshape(tt, batch, S5_W)
    o_ref[...] = jnp.swapaxes(o, 0, 1).astype(o_ref.dtype)


def _s5(u, w, batch, seq):
    consts = [w["bmat"], w["lam"], w["cmat"], w["s5_d"], w["wglu"], w["bglu"]]
    blk = pl.BlockSpec((batch, TT_S5, S5_W), lambda t: (0, t, 0))
    return pl.pallas_call(
        _s5_kernel,
        grid=(seq // TT_S5,),
        in_specs=[blk] + [_const_spec(c) for c in consts],
        out_specs=blk,
        out_shape=jax.ShapeDtypeStruct((batch, seq, S5_W), BF16),
        scratch_shapes=[pltpu.VMEM((batch, 2 * S5_N), F32)],
        compiler_params=_params(("arbitrary",)),
        name="s5",
    )(u.reshape(batch, seq, S5_W), *[c[0] for c in consts]).reshape(batch * seq, S5_W)


def _merge_kernel(x_ref, om_ref, of_ref, os_ref, g_ref, wg_ref, wbr_ref, wout_ref, o_ref):
    for r0 in range(0, x_ref.shape[0], MERGE_SUB):
        rows = slice(r0, r0 + MERGE_SUB)
        x = x_ref[rows, :]
        h = _rms(x, g_ref[...]).astype(BF16)
        merged = None
        for n, br_ref in enumerate((om_ref, of_ref, os_ref)):
            logits = jnp.dot(h, wg_ref[:, n * D_MODEL:(n + 1) * D_MODEL],
                             preferred_element_type=F32)
            proj = jnp.dot(br_ref[rows, :], wbr_ref[n], preferred_element_type=F32)
            term = jax.nn.sigmoid(logits) * proj
            merged = term if merged is None else merged + term
        o_ref[rows, :] = x + jnp.dot(merged.astype(BF16), wout_ref[...],
                                     preferred_element_type=F32)


def _merge(x, o_mla, o_fox, o_s5, w, batch, seq):
    m = x.shape[0]
    tm = TM_MERGE
    nt = seq // tm
    row = lambda b, t: (b * nt + t, 0)
    consts = [w["attn_g"], w["wg"], w["wbr"], w["wout"]]
    return pl.pallas_call(
        _merge_kernel,
        grid=(batch, nt),
        in_specs=[pl.BlockSpec((tm, D_MODEL), row)] + [pl.BlockSpec((tm, BRANCH_W), row)] * 3
                 + [_const_spec(c) for c in consts],
        out_specs=pl.BlockSpec((tm, D_MODEL), row),
        out_shape=jax.ShapeDtypeStruct((m, D_MODEL), F32),
        compiler_params=_params(("arbitrary", "arbitrary")),
        name="merge",
    )(x, o_mla, o_fox, o_s5, *[c[0] for c in consts])


FFN_CHUNK = 256
FFN_DOWN_CUTS = (0, 512, 1024, 1536, 2048, 2560, D_FF)


def _ffn_kernel(x_ref, g_ref, wup_ref, conv_ref, wdown_ref, o_ref, up_ref, act_ref):
    tm = FFN_SUB

    @pl.when(pl.program_id(1) == 0)
    def _():
        up_ref[0:CONV_HALO, :] = jnp.zeros((CONV_HALO, 2 * D_FF), F32)

    def conv(cols):
        out = None
        for j in range(CONV_W):
            lo = CONV_HALO - (CONV_W - 1) + j
            term = conv_ref[j:j + 1, cols] * up_ref[lo:lo + tm, cols]
            out = term if out is None else out + term
        return out

    def sub_tile(i, carry):
        rows = pl.ds(pl.multiple_of(i * tm, tm), tm)
        x = x_ref[rows, :]
        h = _rms(x, g_ref[...]).astype(BF16)
        up_ref[CONV_HALO:CONV_HALO + tm, :] = jnp.dot(h, wup_ref[...], preferred_element_type=F32)
        out = x
        for c0 in range(0, D_FF, FFN_CHUNK):
            gate = conv(slice(c0, c0 + FFN_CHUNK))
            val = conv(slice(D_FF + c0, D_FF + c0 + FFN_CHUNK))
            act_ref[:, c0:c0 + FFN_CHUNK] = (gate * jax.nn.sigmoid(gate) * val).astype(BF16)
            c1 = c0 + FFN_CHUNK
            if c1 in FFN_DOWN_CUTS:
                k0 = FFN_DOWN_CUTS[FFN_DOWN_CUTS.index(c1) - 1]
                out = out + jnp.dot(act_ref[:, k0:c1], wdown_ref[k0:c1, :],
                                    preferred_element_type=F32)
        up_ref[0:CONV_HALO, :] = up_ref[tm:tm + CONV_HALO, :]
        o_ref[rows, :] = out
        return carry

    lax.fori_loop(0, x_ref.shape[0] // tm, sub_tile, 0)


def _ffn(x, w, batch, seq):
    m = x.shape[0]
    tm = TM_FFN
    nt = seq // tm
    row = lambda b, t: (b * nt + t, 0)
    consts = [w["ffn_g"], w["wup"], w["conv"], w["wdown"]]
    return pl.pallas_call(
        _ffn_kernel,
        grid=(batch, nt),
        in_specs=[pl.BlockSpec((tm, D_MODEL), row)] + [_const_spec(c) for c in consts],
        out_specs=pl.BlockSpec((tm, D_MODEL), row),
        out_shape=jax.ShapeDtypeStruct((m, D_MODEL), F32),
        scratch_shapes=[pltpu.VMEM((FFN_SUB + CONV_HALO, 2 * D_FF), F32),
                        pltpu.VMEM((FFN_SUB, D_FF), BF16)],
        compiler_params=_params(("arbitrary", "arbitrary")),
        name="ffn",
    )(x, *[c[0] for c in consts])


def _block_ones(n, width):
    idx = np.arange(n) // width
    return jnp.asarray((idx[:, None] == idx[None, :]).astype(np.float32), dtype=BF16)


def _pack_weights(attn_norm_g, w_in, q_lat_norm_g, w_uq, kv_lat_norm_g, w_ukv,
                  mla_q_norm_g, mla_k_norm_g, fox_q_norm_g, fox_k_norm_g, fox_f_bias,
                  s5_lambda_re, s5_lambda_im, s5_b_re, s5_b_im, s5_c_re, s5_c_im, s5_d,
                  s5_log_step, s5_w_glu, s5_b_glu, w_branch, w_out, ffn_norm_g, w_up,
                  ffn_conv_w, w_down):
    depth = w_in.shape[0]
    f32 = lambda a: a.astype(F32)
    o = np.cumsum((0, MLA_Q_RANK, MLA_KV_RANK, MLA_ROPE, BRANCH_W, BRANCH_W, BRANCH_W,
                   N_HEADS, S5_W))
    col = lambda i: w_in[:, :, o[i]:o[i + 1]]
    spans = sorted([(KR_LANE, col(2))] + [
        (_aug_lane(h), jnp.repeat(col(6)[:, :, h:h + 1], AUG_W, axis=2)) for h in range(N_HEADS)],
        key=lambda s: s[0])
    misc, lane_pos = [], 0
    for start, piece in spans + [(LANE, None)]:
        if start > lane_pos:
            misc.append(jnp.zeros((depth, D_MODEL, start - lane_pos), w_in.dtype))
        if piece is not None:
            misc.append(piece)
            lane_pos = start + piece.shape[-1]
    wa = jnp.concatenate([col(0), col(1), col(3), col(4), col(5), col(7)] + misc, axis=-1)
    wg = w_in[:, :, o[8]:]

    pad_h = lambda a: jnp.pad(a, [(0, 0)] * (a.ndim - 1) + [(0, HEAD_PAD - a.shape[-1])])
    wuq = pad_h(w_uq).reshape(depth, MLA_Q_RANK, QK_W)
    wukv = jnp.concatenate(
        [pad_h(w_ukv[..., :MLA_NOPE]).reshape(depth, MLA_KV_RANK, QK_W),
         w_ukv[..., MLA_NOPE:].reshape(depth, MLA_KV_RANK, BRANCH_W)], axis=-1)
    tile_h = lambda g: jnp.tile(g, (1, N_HEADS)).reshape(depth, 1, -1)
    gq = tile_h(pad_h(f32(mla_q_norm_g))) * (MLA_QK ** -0.5 * LOG2E)
    gk = tile_h(pad_h(f32(mla_k_norm_g)))
    gfq = tile_h(f32(fox_q_norm_g)) * (FOX_HD ** -0.5 * LOG2E)
    gfk = tile_h(f32(fox_k_norm_g))
    fbias = jnp.zeros((depth, LANE), F32)
    augm = np.zeros((8, LANE), np.float32)
    for h in range(N_HEADS):
        a = _aug_lane(h)
        fbias = fbias.at[:, a:a + AUG_W].set(f32(fox_f_bias)[:, h:h + 1])
        for j in range(3):
            augm[j, a + j] = 1.0
            augm[3, a + 3 + j] = 1.0
            augm[4, a + j] = 1.0
            augm[5 + j, a + 3 + j] = 1.0
    fbias = fbias.reshape(depth, 1, LANE)

    lam_re, lam_im = f32(s5_lambda_re), f32(s5_lambda_im)
    step = jnp.exp(f32(s5_log_step))[..., None]
    mag = jnp.exp(lam_re * step)
    a_re, a_im = mag * jnp.cos(lam_im * step), mag * jnp.sin(lam_im * step)
    den = lam_re * lam_re + lam_im * lam_im
    k_re = ((a_re - 1.0) * lam_re + a_im * lam_im) / den
    k_im = (a_im * lam_re - (a_re - 1.0) * lam_im) / den
    b_re, b_im = f32(s5_b_re), f32(s5_b_im)
    bb_re = k_re[..., None] * b_re - k_im[..., None] * b_im
    bb_im = k_re[..., None] * b_im + k_im[..., None] * b_re
    eye = jnp.eye(S5_G, dtype=F32)
    bd_in = lambda a: jnp.einsum("lgph,gk->lghkp", a, eye).reshape(depth, S5_W, S5_N)
    bmat = jnp.concatenate([bd_in(bb_re), bd_in(bb_im)], axis=-1)
    bd_out = lambda a: jnp.einsum("lghp,gk->lgpkh", a, eye).reshape(depth, S5_N, S5_W)
    cmat = jnp.concatenate([bd_out(f32(s5_c_re)), -bd_out(f32(s5_c_im))], axis=1)
    lam = jnp.stack([a_re.reshape(depth, S5_N), a_im.reshape(depth, S5_N)], axis=1)

    bf = lambda a: a.astype(BF16)
    row = lambda a: f32(a).reshape(depth, 1, -1)
    tri = jnp.asarray(np.tril(np.ones((CUMSUM_BLOCK, CUMSUM_BLOCK), np.float32)), dtype=BF16)
    shared = {"ones_qk": _block_ones(2 * LANE, HEAD_PAD), "ones_fox": _block_ones(BRANCH_W, FOX_HD),
              "tri": tri, "augm": jnp.asarray(augm)}
    stacked = {
        "attn_g": row(attn_norm_g), "wa": bf(wa), "wg": bf(wg), "fbias": fbias,
        "qlat_g": row(q_lat_norm_g), "wuq": bf(wuq), "kvlat_g": row(kv_lat_norm_g),
        "wukv": bf(wukv), "gq": gq, "gk": gk, "gfq": gfq, "gfk": gfk,
        "bmat": bf(bmat), "lam": lam, "cmat": bf(cmat), "s5_d": row(s5_d),
        "wglu": bf(s5_w_glu), "bglu": row(s5_b_glu),
        "wbr": bf(w_branch), "wout": bf(w_out),
        "ffn_g": row(ffn_norm_g), "wup": bf(w_up), "conv": f32(ffn_conv_w), "wdown": bf(w_down),
    }
    return [dict({k: (v, None) for k, v in shared.items()},
                 **{k: (v, l) for k, v in stacked.items()}) for l in range(depth)]


def kernel(x, positions, attn_norm_g, w_in, q_lat_norm_g, w_uq, kv_lat_norm_g, w_ukv, mla_q_norm_g, mla_k_norm_g, fox_q_norm_g, fox_k_norm_g, fox_f_bias, s5_lambda_re, s5_lambda_im, s5_b_re, s5_b_im, s5_c_re, s5_c_im, s5_d, s5_log_step, s5_w_glu, s5_b_glu, w_branch, w_out, ffn_norm_g, w_up, ffn_conv_w, w_down):
    batch, seq, d_model = x.shape
    assert d_model == D_MODEL and seq % TK == 0 and seq % TT_S5 == 0
    assert seq % TM_PROJ == 0 and seq % TM_MERGE == 0 and seq % TM_FFN == 0
    assert batch % 8 == 0
    layers = _pack_weights(attn_norm_g, w_in, q_lat_norm_g, w_uq, kv_lat_norm_g, w_ukv,
                           mla_q_norm_g, mla_k_norm_g, fox_q_norm_g, fox_k_norm_g, fox_f_bias,
                           s5_lambda_re, s5_lambda_im, s5_b_re, s5_b_im, s5_c_re, s5_c_im, s5_d,
                           s5_log_step, s5_w_glu, s5_b_glu, w_branch, w_out, ffn_norm_g, w_up,
                           ffn_conv_w, w_down)
    cos, sin = _rope_tables(positions)
    xf = x.astype(F32).reshape(batch * seq, D_MODEL)
    for w in layers:
        qm, km, vm, qf, kf, vf, u = _proj(xf, cos, sin, w, batch, seq)
        o_mla = _attention(qm, km, vm, batch, seq)
        o_fox = _attention(qf, kf, vf, batch, seq)
        o_s5 = _s5(u, w, batch, seq)
        xf = _merge(xf, o_mla, o_fox, o_s5, w, batch, seq)
        xf = _ffn(xf, w, batch, seq)
    return xf.reshape(batch, seq, D_MODEL).astype(x.dtype)
```

```python
import math

import numpy as np
import jax
import jax.numpy as jnp
from jax import lax
from jax.experimental import pallas as pl
from jax.experimental.pallas import tpu as pltpu

F32 = jnp.float32
BF16 = jnp.bfloat16

D_MODEL = 1024
N_HEADS = 4
MLA_NOPE = 64
MLA_ROPE = 32
MLA_QK = MLA_NOPE + MLA_ROPE
MLA_V = 64
MLA_Q_RANK = 384
MLA_KV_RANK = 256
FOX_HD = 64
S5_G = 16
S5_H = 16
S5_P = 64
S5_W = S5_G * S5_H
S5_N = S5_G * S5_P
BRANCH_W = 256
N_BRANCH = 3
D_FF = 2816
CONV_W = 3
ROPE_THETA = 10000.0
EPS = 1e-6
NEG = -1e30
LOG2E = math.log2(math.e)

LANE = 128
HEAD_PAD = 128
QK_W = N_HEADS * HEAD_PAD
C_CQ, C_CKV, C_FQ, C_FK, C_FV, C_U, C_MISC = 0, 384, 640, 896, 1152, 1408, 1664
WA_COLS = 1792
KR_LANE = MLA_NOPE
AUG_W = 6

VMEM_LIMIT = 56 * 1024 * 1024

TM_PROJ = 1024
PROJ_SUB = 512
CUMSUM_BLOCK = 256
TK = 512
TT_S5 = 128
TM_MERGE = 2048
MERGE_SUB = 256
TM_FFN = 1024
FFN_SUB = 512
CONV_HALO = 8


def _const_spec(param):
    arr, layer = param
    if layer is None:
        return pl.BlockSpec(arr.shape, lambda *_: (0,) * arr.ndim, pipeline_mode=pl.Buffered(1))
    zeros = (0,) * (arr.ndim - 1)
    return pl.BlockSpec((None,) + arr.shape[1:], lambda *_: (layer,) + zeros,
                        pipeline_mode=pl.Buffered(1))


def _params(sem):
    return pltpu.CompilerParams(dimension_semantics=sem, vmem_limit_bytes=VMEM_LIMIT)


def _rms(x, gain):
    ms = jnp.mean(x * x, axis=-1, keepdims=True)
    return x * lax.rsqrt(ms + EPS) * gain


def _split3(x):
    hi = x.astype(BF16)
    r = x - hi.astype(F32)
    mid = r.astype(BF16)
    lo = (r - mid.astype(F32)).astype(BF16)
    return hi, mid, lo


def _head_sums(sq, ones_ref):
    sq = sq.astype(BF16)
    ones = ones_ref[...]
    outs = [jnp.dot(sq[:, c0:c0 + 2 * LANE], ones, preferred_element_type=F32)
            for c0 in range(0, sq.shape[1], 2 * LANE)]
    return outs[0] if len(outs) == 1 else jnp.concatenate(outs, axis=1)


def _aug_lane(h):
    return MLA_QK + AUG_W * h if h % 2 == 0 else AUG_W * (h + 1)


def _den_lane(h):
    return MLA_V if h % 2 == 0 else 0


def _pad_value_heads(v):
    lane = lax.broadcasted_iota(jnp.int32, (v.shape[0], LANE), 1)
    tiles = []
    for h in range(N_HEADS):
        pair = v[:, (h // 2) * LANE:(h // 2 + 1) * LANE]
        mine = lane < MLA_V if h % 2 == 0 else lane >= MLA_V
        tiles.append(jnp.where(mine, pair, jnp.where(lane == _den_lane(h), 1.0, 0.0)))
    return jnp.concatenate(tiles, axis=1)


def _rope_table_kernel(pos_ref, inv_ref, sign_ref, cos_ref, sin_ref):
    ang = pos_ref[...] * inv_ref[...]
    cos_ref[...] = jnp.cos(ang)
    sin_ref[...] = jnp.sin(ang) * sign_ref[...]


def _rope_tables(positions):
    m = positions.size
    tm = 1024
    pos = positions.astype(F32).reshape(m, 1)
    inv_freq = ROPE_THETA ** (-jnp.arange(0, MLA_ROPE, 2, dtype=F32) / MLA_ROPE)
    half = MLA_ROPE // 2
    inv_lane = jnp.zeros((LANE,), F32)
    inv_lane = inv_lane.at[MLA_NOPE:MLA_NOPE + half].set(inv_freq)
    inv_lane = inv_lane.at[MLA_NOPE + half:MLA_QK].set(inv_freq)
    sign = np.zeros((LANE,), np.float32)
    sign[MLA_NOPE:MLA_NOPE + half] = -1.0
    sign[MLA_NOPE + half:MLA_QK] = 1.0
    return pl.pallas_call(
        _rope_table_kernel,
        grid=(m // tm,),
        in_specs=[pl.BlockSpec((tm, 1), lambda i: (i, 0)),
                  pl.BlockSpec((1, LANE), lambda i: (0, 0)),
                  pl.BlockSpec((1, LANE), lambda i: (0, 0))],
        out_specs=[pl.BlockSpec((tm, LANE), lambda i: (i, 0))] * 2,
        out_shape=[jax.ShapeDtypeStruct((m, LANE), F32)] * 2,
        compiler_params=_params(("arbitrary",)),
        name="rope_tables",
    )(pos, inv_lane.reshape(1, LANE), jnp.asarray(sign).reshape(1, LANE))


def _proj_kernel(x_ref, *refs):
    carry_ref = refs[-1]

    @pl.when(pl.program_id(1) == 0)
    def _():
        carry_ref[...] = jnp.zeros_like(carry_ref)

    for r0 in range(0, x_ref.shape[0], PROJ_SUB):
        _proj_rows(slice(r0, r0 + PROJ_SUB), x_ref, *refs)


def _proj_rows(rows, x_ref, cos_ref, sin_ref, g_ref, wa_ref, fbias_ref, augm_ref,
               qlat_g_ref, wuq_ref, kvlat_g_ref, wukv_ref, gq_ref, gk_ref, ones_qk_ref,
               gfq_ref, gfk_ref, ones_fox_ref, tri_ref,
               qm_ref, km_ref, vm_ref, qf_ref, kf_ref, vf_ref, u_ref,
               carry_ref):
    tm = PROJ_SUB
    h = _rms(x_ref[rows, :], g_ref[...]).astype(BF16)
    p = jnp.dot(h, wa_ref[...], preferred_element_type=F32)
    misc = p[:, C_MISC:C_MISC + LANE]
    lane = lax.broadcasted_iota(jnp.int32, (tm, LANE), 1)
    cos = cos_ref[rows, :]
    sin = sin_ref[rows, :]

    cq = _rms(p[:, C_CQ:C_CQ + MLA_Q_RANK], qlat_g_ref[...]).astype(BF16)
    q = jnp.dot(cq, wuq_ref[...], preferred_element_type=F32)
    cos4 = jnp.concatenate([cos] * N_HEADS, axis=1)
    sin4 = jnp.concatenate([sin] * N_HEADS, axis=1)
    first_half = lane < (MLA_NOPE + MLA_ROPE // 2)
    r = lax.rsqrt(_head_sums(q * q, ones_qk_ref) * (1.0 / MLA_QK) + EPS)
    qg = q * gq_ref[...]
    partners = []
    for hd in range(N_HEADS):
        blk = qg[:, hd * HEAD_PAD:(hd + 1) * HEAD_PAD]
        partners.append(jnp.where(first_half, pltpu.roll(blk, HEAD_PAD - MLA_ROPE // 2, 1),
                                  pltpu.roll(blk, MLA_ROPE // 2, 1)))
    qm_ref[rows, :] = (r * (qg * cos4 + jnp.concatenate(partners, axis=1) * sin4)).astype(BF16)

    ckv = _rms(p[:, C_CKV:C_CKV + MLA_KV_RANK], kvlat_g_ref[...]).astype(BF16)
    kv = jnp.dot(ckv, wukv_ref[...], preferred_element_type=F32)
    k_rope = jnp.where((lane >= KR_LANE) & (lane < MLA_QK), misc, 0.0)
    k = kv[:, :QK_W] + jnp.concatenate([k_rope] * N_HEADS, axis=1)
    r = lax.rsqrt(_head_sums(k * k, ones_qk_ref) * (1.0 / MLA_QK) + EPS)
    krg = k_rope * gk_ref[:, :LANE]
    partner = jnp.where(first_half, pltpu.roll(krg, HEAD_PAD - MLA_ROPE // 2, 1),
                        pltpu.roll(krg, MLA_ROPE // 2, 1)) * sin
    km_ref[rows, :] = (r * (k * gk_ref[...] * cos4
                            + jnp.concatenate([partner] * N_HEADS, axis=1))).astype(BF16)
    vm_ref[rows, :] = _pad_value_heads(kv[:, QK_W:]).astype(BF16)

    z = misc + fbias_ref[...]
    log_f = jnp.minimum(z, 0.0) - jnp.log1p(jnp.exp(-jnp.abs(z)))
    tri = tri_ref[...]
    parts = _split3(log_f)
    carry = carry_ref[0:1, :]
    c_blocks = []
    for r0 in range(0, tm, CUMSUM_BLOCK):
        cb = carry
        for part in parts:
            cb = cb + jnp.dot(tri, part[r0:r0 + CUMSUM_BLOCK], preferred_element_type=F32)
        carry = cb[CUMSUM_BLOCK - 1:CUMSUM_BLOCK, :]
        c_blocks.append(cb)
    carry_ref[0:1, :] = carry
    c = jnp.concatenate(c_blocks, axis=0) * LOG2E

    fq = p[:, C_FQ:C_FQ + BRANCH_W]
    fq = fq * lax.rsqrt(_head_sums(fq * fq, ones_fox_ref) * (1.0 / FOX_HD) + EPS) * gfq_ref[...]
    fk = p[:, C_FK:C_FK + BRANCH_W]
    fk = fk * lax.rsqrt(_head_sums(fk * fk, ones_fox_ref) * (1.0 / FOX_HD) + EPS) * gfk_ref[...]
    hi = c.astype(BF16).astype(F32)
    rem = c - hi
    mid = rem.astype(BF16).astype(F32)
    lo = rem - mid
    aug_q = hi * augm_ref[0:1, :] + mid * augm_ref[1:2, :] + lo * augm_ref[2:3, :] + augm_ref[3:4, :]
    aug_k = augm_ref[4:5, :] - hi * augm_ref[5:6, :] - mid * augm_ref[6:7, :] - lo * augm_ref[7:8, :]
    q_tiles, k_tiles = [], []
    for hd in range(N_HEADS):
        pair = slice((hd // 2) * LANE, (hd // 2 + 1) * LANE)
        data = lane < FOX_HD if hd % 2 == 0 else lane >= FOX_HD
        in_aug = (lane >= _aug_lane(hd)) & (lane < _aug_lane(hd) + AUG_W)
        q_tiles.append(jnp.where(data, fq[:, pair], jnp.where(in_aug, aug_q, 0.0)))
        k_tiles.append(jnp.where(data, fk[:, pair], jnp.where(in_aug, aug_k, 0.0)))
    qf_ref[rows, :] = jnp.concatenate(q_tiles, axis=1).astype(BF16)
    kf_ref[rows, :] = jnp.concatenate(k_tiles, axis=1).astype(BF16)
    vf_ref[rows, :] = _pad_value_heads(p[:, C_FV:C_FV + BRANCH_W]).astype(BF16)

    u_ref[rows, :] = p[:, C_U:C_U + S5_W]


def _proj(x, cos, sin, w, batch, seq):
    m = x.shape[0]
    tm = TM_PROJ
    nt = seq // tm
    row = lambda b, t: (b * nt + t, 0)
    consts = [w["attn_g"], w["wa"], w["fbias"], w["augm"], w["qlat_g"], w["wuq"], w["kvlat_g"], w["wukv"],
              w["gq"], w["gk"], w["ones_qk"], w["gfq"], w["gfk"], w["ones_fox"], w["tri"]]
    out_shape = [jax.ShapeDtypeStruct((m, QK_W), BF16)] * 6 + [
        jax.ShapeDtypeStruct((m, S5_W), F32)]
    out_specs = [pl.BlockSpec((tm, QK_W), row)] * 6 + [pl.BlockSpec((tm, S5_W), row)]
    return pl.pallas_call(
        _proj_kernel,
        grid=(batch, nt),
        in_specs=[pl.BlockSpec((tm, D_MODEL), row), pl.BlockSpec((tm, LANE), row),
                  pl.BlockSpec((tm, LANE), row)] + [_const_spec(c) for c in consts],
        out_specs=out_specs,
        out_shape=out_shape,
        scratch_shapes=[pltpu.VMEM((8, LANE), F32)],
        compiler_params=_params(("arbitrary", "arbitrary")),
        name="proj",
    )(x, cos, sin, *[c[0] for c in consts])


def _attn_kernel(q_ref, k_ref, v_ref, o_ref, acc_ref, m_ref):
    seq = q_ref.shape[0]
    tk = TK
    causal = (lax.broadcasted_iota(jnp.int32, (tk, tk), 0)
              >= lax.broadcasted_iota(jnp.int32, (tk, tk), 1))
    low_half = lax.broadcasted_iota(jnp.int32, (tk, LANE), 1) < MLA_V

    for j in range(seq // tk):
        r0 = j * tk
        for h in range(N_HEADS):
            hs = slice(h * HEAD_PAD, (h + 1) * HEAD_PAD)
            s = lax.dot_general(q_ref[r0:, hs], k_ref[r0:r0 + tk, hs],
                                (((1,), (1,)), ((), ())), preferred_element_type=F32)
            top = jnp.where(causal, s[:tk], NEG)
            s = top if seq - r0 == tk else jnp.concatenate([top, s[tk:]], axis=0)
            m_cur = jnp.max(s, axis=1, keepdims=True)
            if j == 0:
                m_new = jnp.broadcast_to(m_cur, (seq, LANE))
            else:
                m_prev = m_ref[h, r0:, :]
                m_new = jnp.maximum(m_prev, m_cur)
                alpha = jnp.exp2(m_prev - m_new)
            p = jnp.exp2(s - jnp.concatenate([m_new] * (tk // LANE), axis=1))
            pv = jnp.dot(p.astype(BF16), v_ref[r0:r0 + tk, hs], preferred_element_type=F32)
            if j == 0:
                acc_ref[h] = pv
            else:
                acc_ref[h, r0:, :] = acc_ref[h, r0:, :] * alpha + pv
            if seq - r0 > tk:
                m_ref[h, r0:, :] = m_new
        outs = []
        for h in range(N_HEADS):
            a = acc_ref[h, r0:r0 + tk, :]
            outs.append(a * (1.0 / a[:, _den_lane(h):_den_lane(h) + 1]))
        o_ref[r0:r0 + tk, :] = jnp.concatenate(
            [jnp.where(low_half, outs[0], outs[1]), jnp.where(low_half, outs[2], outs[3])],
            axis=1).astype(o_ref.dtype)


def _attention(q, k, v, batch, seq):
    m = q.shape[0]
    spec = pl.BlockSpec((seq, QK_W), lambda b: (b, 0))
    return pl.pallas_call(
        _attn_kernel,
        grid=(batch,),
        in_specs=[spec, spec, spec],
        out_specs=pl.BlockSpec((seq, BRANCH_W), lambda b: (b, 0)),
        out_shape=jax.ShapeDtypeStruct((m, BRANCH_W), BF16),
        scratch_shapes=[pltpu.VMEM((N_HEADS, seq, LANE), F32),
                        pltpu.VMEM((N_HEADS, seq, LANE), F32)],
        compiler_params=_params(("arbitrary",)),
        name="attn",
    )(q, k, v)


def _s5_kernel(u_ref, bmat_ref, lam_ref, cmat_ref, d_ref, wglu_ref, bglu_ref, o_ref, st_ref):
    batch, tt, _ = u_ref.shape

    @pl.when(pl.program_id(0) == 0)
    def _():
        st_ref[...] = jnp.zeros_like(st_ref)

    u = jnp.swapaxes(u_ref[...], 0, 1).reshape(tt * batch, S5_W)
    half = tt * batch // 2
    u16 = u.astype(BF16)
    bu = jnp.concatenate([jnp.dot(u16[r0:r0 + half], bmat_ref[...], preferred_element_type=F32)
                          for r0 in (0, half)], axis=0)
    lam_re = lam_ref[0:1, :]
    lam_im = lam_ref[1:2, :]
    x_re = st_ref[:, :S5_N]
    x_im = st_ref[:, S5_N:]
    states = []
    for t in range(tt):
        r = slice(t * batch, (t + 1) * batch)
        n_re = lam_re * x_re - lam_im * x_im + bu[r, :S5_N]
        n_im = lam_re * x_im + lam_im * x_re + bu[r, S5_N:]
        x_re, x_im = n_re, n_im
        states.append(jnp.concatenate([n_re, n_im], axis=1).astype(BF16))
    st_ref[:, :S5_N] = x_re
    st_ref[:, S5_N:] = x_im

    st = jnp.concatenate(states, axis=0)
    y = jnp.concatenate([jnp.dot(st[r0:r0 + half], cmat_ref[...], preferred_element_type=F32)
                         for r0 in (0, half)], axis=0)
    y = y + d_ref[...] * u
    y = 0.5 * y * (1.0 + jnp.tanh(math.sqrt(2.0 / math.pi) * (y + 0.044715 * (y * y * y))))
    z = jnp.dot(y.astype(BF16), wglu_ref[...], preferred_element_type=F32) + bglu_ref[...]
    o = (y * jax.nn.sigmoid(z)).reshape(tt, batch, S5_W)
    o_ref[...] = jnp.swapaxes(o, 0, 1).astype(o_ref.dtype)


def _s5(u, w, batch, seq):
    consts = [w["bmat"], w["lam"], w["cmat"], w["s5_d"], w["wglu"], w["bglu"]]
    blk = pl.BlockSpec((batch, TT_S5, S5_W), lambda t: (0, t, 0))
    return pl.pallas_call(
        _s5_kernel,
        grid=(seq // TT_S5,),
        in_specs=[blk] + [_const_spec(c) for c in consts],
        out_specs=blk,
        out_shape=jax.ShapeDtypeStruct((batch, seq, S5_W), BF16),
        scratch_shapes=[pltpu.VMEM((batch, 2 * S5_N), F32)],
        compiler_params=_params(("arbitrary",)),
        name="s5",
    )(u.reshape(batch, seq, S5_W), *[c[0] for c in consts]).reshape(batch * seq, S5_W)


def _merge_kernel(x_ref, om_ref, of_ref, os_ref, g_ref, wg_ref, wbr_ref, wout_ref, o_ref):
    for r0 in range(0, x_ref.shape[0], MERGE_SUB):
        rows = slice(r0, r0 + MERGE_SUB)
        x = x_ref[rows, :]
        h = _rms(x, g_ref[...]).astype(BF16)
        merged = None
        for n, br_ref in enumerate((om_ref, of_ref, os_ref)):
            logits = jnp.dot(h, wg_ref[:, n * D_MODEL:(n + 1) * D_MODEL],
                             preferred_element_type=F32)
            proj = jnp.dot(br_ref[rows, :], wbr_ref[n], preferred_element_type=F32)
            term = jax.nn.sigmoid(logits) * proj
            merged = term if merged is None else merged + term
        o_ref[rows, :] = x + jnp.dot(merged.astype(BF16), wout_ref[...],
                                     preferred_element_type=F32)


def _merge(x, o_mla, o_fox, o_s5, w, batch, seq):
    m = x.shape[0]
    tm = TM_MERGE
    nt = seq // tm
    row = lambda b, t: (b * nt + t, 0)
    consts = [w["attn_g"], w["wg"], w["wbr"], w["wout"]]
    return pl.pallas_call(
        _merge_kernel,
        grid=(batch, nt),
        in_specs=[pl.BlockSpec((tm, D_MODEL), row)] + [pl.BlockSpec((tm, BRANCH_W), row)] * 3
                 + [_const_spec(c) for c in consts],
        out_specs=pl.BlockSpec((tm, D_MODEL), row),
        out_shape=jax.ShapeDtypeStruct((m, D_MODEL), F32),
        compiler_params=_params(("arbitrary", "arbitrary")),
        name="merge",
    )(x, o_mla, o_fox, o_s5, *[c[0] for c in consts])


FFN_CHUNK = 256
FFN_DOWN_CUTS = (0, 512, 1024, 1536, 2048, 2560, D_FF)


def _ffn_kernel(x_ref, g_ref, wup_ref, conv_ref, wdown_ref, o_ref, up_ref, act_ref):
    tm = FFN_SUB

    @pl.when(pl.program_id(1) == 0)
    def _():
        up_ref[0:CONV_HALO, :] = jnp.zeros((CONV_HALO, 2 * D_FF), F32)

    def conv(cols):
        out = None
        for j in range(CONV_W):
            lo = CONV_HALO - (CONV_W - 1) + j
            term = conv_ref[j:j + 1, cols] * up_ref[lo:lo + tm, cols]
            out = term if out is None else out + term
        return out

    def sub_tile(i, carry):
        rows = pl.ds(pl.multiple_of(i * tm, tm), tm)
        x = x_ref[rows, :]
        h = _rms(x, g_ref[...]).astype(BF16)
        up_ref[CONV_HALO:CONV_HALO + tm, :] = jnp.dot(h, wup_ref[...], preferred_element_type=F32)
        out = x
        for c0 in range(0, D_FF, FFN_CHUNK):
            gate = conv(slice(c0, c0 + FFN_CHUNK))
            val = conv(slice(D_FF + c0, D_FF + c0 + FFN_CHUNK))
            act_ref[:, c0:c0 + FFN_CHUNK] = (gate * jax.nn.sigmoid(gate) * val).astype(BF16)
            c1 = c0 + FFN_CHUNK
            if c1 in FFN_DOWN_CUTS:
                k0 = FFN_DOWN_CUTS[FFN_DOWN_CUTS.index(c1) - 1]
                out = out + jnp.dot(act_ref[:, k0:c1], wdown_ref[k0:c1, :],
                                    preferred_element_type=F32)
        up_ref[0:CONV_HALO, :] = up_ref[tm:tm + CONV_HALO, :]
        o_ref[rows, :] = out
        return carry

    lax.fori_loop(0, x_ref.shape[0] // tm, sub_tile, 0)


def _ffn(x, w, batch, seq):
    m = x.shape[0]
    tm = TM_FFN
    nt = seq // tm
    row = lambda b, t: (b * nt + t, 0)
    consts = [w["ffn_g"], w["wup"], w["conv"], w["wdown"]]
    return pl.pallas_call(
        _ffn_kernel,
        grid=(batch, nt),
        in_specs=[pl.BlockSpec((tm, D_MODEL), row)] + [_const_spec(c) for c in consts],
        out_specs=pl.BlockSpec((tm, D_MODEL), row),
        out_shape=jax.ShapeDtypeStruct((m, D_MODEL), F32),
        scratch_shapes=[pltpu.VMEM((FFN_SUB + CONV_HALO, 2 * D_FF), F32),
                        pltpu.VMEM((FFN_SUB, D_FF), BF16)],
        compiler_params=_params(("arbitrary", "arbitrary")),
        name="ffn",
    )(x, *[c[0] for c in consts])


def _block_ones(n, width):
    idx = np.arange(n) // width
    return jnp.asarray((idx[:, None] == idx[None, :]).astype(np.float32), dtype=BF16)


def _pack_weights(attn_norm_g, w_in, q_lat_norm_g, w_uq, kv_lat_norm_g, w_ukv,
                  mla_q_norm_g, mla_k_norm_g, fox_q_norm_g, fox_k_norm_g, fox_f_bias,
                  s5_lambda_re, s5_lambda_im, s5_b_re, s5_b_im, s5_c_re, s5_c_im, s5_d,
                  s5_log_step, s5_w_glu, s5_b_glu, w_branch, w_out, ffn_norm_g, w_up,
                  ffn_conv_w, w_down):
    depth = w_in.shape[0]
    f32 = lambda a: a.astype(F32)
    o = np.cumsum((0, MLA_Q_RANK, MLA_KV_RANK, MLA_ROPE, BRANCH_W, BRANCH_W, BRANCH_W,
                   N_HEADS, S5_W))
    col = lambda i: w_in[:, :, o[i]:o[i + 1]]
    spans = sorted([(KR_LANE, col(2))] + [
        (_aug_lane(h), jnp.repeat(col(6)[:, :, h:h + 1], AUG_W, axis=2)) for h in range(N_HEADS)],
        key=lambda s: s[0])
    misc, lane_pos = [], 0
    for start, piece in spans + [(LANE, None)]:
        if start > lane_pos:
            misc.append(jnp.zeros((depth, D_MODEL, start - lane_pos), w_in.dtype))
        if piece is not None:
            misc.append(piece)
            lane_pos = start + piece.shape[-1]
    wa = jnp.concatenate([col(0), col(1), col(3), col(4), col(5), col(7)] + misc, axis=-1)
    wg = w_in[:, :, o[8]:]

    pad_h = lambda a: jnp.pad(a, [(0, 0)] * (a.ndim - 1) + [(0, HEAD_PAD - a.shape[-1])])
    wuq = pad_h(w_uq).reshape(depth, MLA_Q_RANK, QK_W)
    wukv = jnp.concatenate(
        [pad_h(w_ukv[..., :MLA_NOPE]).reshape(depth, MLA_KV_RANK, QK_W),
         w_ukv[..., MLA_NOPE:].reshape(depth, MLA_KV_RANK, BRANCH_W)], axis=-1)
    tile_h = lambda g: jnp.tile(g, (1, N_HEADS)).reshape(depth, 1, -1)
    gq = tile_h(pad_h(f32(mla_q_norm_g))) * (MLA_QK ** -0.5 * LOG2E)
    gk = tile_h(pad_h(f32(mla_k_norm_g)))
    gfq = tile_h(f32(fox_q_norm_g)) * (FOX_HD ** -0.5 * LOG2E)
    gfk = tile_h(f32(fox_k_norm_g))
    fbias = jnp.zeros((depth, LANE), F32)
    augm = np.zeros((8, LANE), np.float32)
    for h in range(N_HEADS):
        a = _aug_lane(h)
        fbias = fbias.at[:, a:a + AUG_W].set(f32(fox_f_bias)[:, h:h + 1])
        for j in range(3):
            augm[j, a + j] = 1.0
            augm[3, a + 3 + j] = 1.0
            augm[4, a + j] = 1.0
            augm[5 + j, a + 3 + j] = 1.0
    fbias = fbias.reshape(depth, 1, LANE)

    lam_re, lam_im = f32(s5_lambda_re), f32(s5_lambda_im)
    step = jnp.exp(f32(s5_log_step))[..., None]
    mag = jnp.exp(lam_re * step)
    a_re, a_im = mag * jnp.cos(lam_im * step), mag * jnp.sin(lam_im * step)
    den = lam_re * lam_re + lam_im * lam_im
    k_re = ((a_re - 1.0) * lam_re + a_im * lam_im) / den
    k_im = (a_im * lam_re - (a_re - 1.0) * lam_im) / den
    b_re, b_im = f32(s5_b_re), f32(s5_b_im)
    bb_re = k_re[..., None] * b_re - k_im[..., None] * b_im
    bb_im = k_re[..., None] * b_im + k_im[..., None] * b_re
    eye = jnp.eye(S5_G, dtype=F32)
    bd_in = lambda a: jnp.einsum("lgph,gk->lghkp", a, eye).reshape(depth, S5_W, S5_N)
    bmat = jnp.concatenate([bd_in(bb_re), bd_in(bb_im)], axis=-1)
    bd_out = lambda a: jnp.einsum("lghp,gk->lgpkh", a, eye).reshape(depth, S5_N, S5_W)
    cmat = jnp.concatenate([bd_out(f32(s5_c_re)), -bd_out(f32(s5_c_im))], axis=1)
    lam = jnp.stack([a_re.reshape(depth, S5_N), a_im.reshape(depth, S5_N)], axis=1)

    bf = lambda a: a.astype(BF16)
    row = lambda a: f32(a).reshape(depth, 1, -1)
    tri = jnp.asarray(np.tril(np.ones((CUMSUM_BLOCK, CUMSUM_BLOCK), np.float32)), dtype=BF16)
    shared = {"ones_qk": _block_ones(2 * LANE, HEAD_PAD), "ones_fox": _block_ones(BRANCH_W, FOX_HD),
              "tri": tri, "augm": jnp.asarray(augm)}
    stacked = {
        "attn_g": row(attn_norm_g), "wa": bf(wa), "wg": bf(wg), "fbias": fbias,
        "qlat_g": row(q_lat_norm_g), "wuq": bf(wuq), "kvlat_g": row(kv_lat_norm_g),
        "wukv": bf(wukv), "gq": gq, "gk": gk, "gfq": gfq, "gfk": gfk,
        "bmat": bf(bmat), "lam": lam, "cmat": bf(cmat), "s5_d": row(s5_d),
        "wglu": bf(s5_w_glu), "bglu": row(s5_b_glu),
        "wbr": bf(w_branch), "wout": bf(w_out),
        "ffn_g": row(ffn_norm_g), "wup": bf(w_up), "conv": f32(ffn_conv_w), "wdown": bf(w_down),
    }
    return [dict({k: (v, None) for k, v in shared.items()},
                 **{k: (v, l) for k, v in stacked.items()}) for l in range(depth)]


def kernel(x, positions, attn_norm_g, w_in, q_lat_norm_g, w_uq, kv_lat_norm_g, w_ukv, mla_q_norm_g, mla_k_norm_g, fox_q_norm_g, fox_k_norm_g, fox_f_bias, s5_lambda_re, s5_lambda_im, s5_b_re, s5_b_im, s5_c_re, s5_c_im, s5_d, s5_log_step, s5_w_glu, s5_b_glu, w_branch, w_out, ffn_norm_g, w_up, ffn_conv_w, w_down):
    batch, seq, d_model = x.shape
    assert d_model == D_MODEL and seq % TK == 0 and seq % TT_S5 == 0
    assert seq % TM_PROJ == 0 and seq % TM_MERGE == 0 and seq % TM_FFN == 0
    assert batch % 8 == 0
    layers = _pack_weights(attn_norm_g, w_in, q_lat_norm_g, w_uq, kv_lat_norm_g, w_ukv,
                           mla_q_norm_g, mla_k_norm_g, fox_q_norm_g, fox_k_norm_g, fox_f_bias,
                           s5_lambda_re, s5_lambda_im, s5_b_re, s5_b_im, s5_c_re, s5_c_im, s5_d,
                           s5_log_step, s5_w_glu, s5_b_glu, w_branch, w_out, ffn_norm_g, w_up,
                           ffn_conv_w, w_down)
    cos, sin = _rope_tables(positions)
    xf = x.astype(F32).reshape(batch * seq, D_MODEL)
    for w in layers:
        qm, km, vm, qf, kf, vf, u = _proj(xf, cos, sin, w, batch, seq)
        o_mla = _attention(qm, km, vm, batch, seq)
        o_fox = _attention(qf, kf, vf, batch, seq)
        o_s5 = _s5(u, w, batch, seq)
        xf = _merge(xf, o_mla, o_fox, o_s5, w, batch, seq)
        xf = _ffn(xf, w, batch, seq)
    return xf.reshape(batch, seq, D_MODEL).astype(x.dtype)
```
